```python
import jax, jax.numpy as jnp
from jax import lax
import numpy as np

D_MODEL = 2048
BATCH = 4
SEQ = 8192
DEPTH = 1
DEC_BATCH = 2
DEC_SEQ = 8192
PAST_LEN = 128

N_FOURIER_GROUPS = 4
FOURIER_GROUP_DIM = 256
FOURIER_DIM = N_FOURIER_GROUPS * FOURIER_GROUP_DIM
N_HEADS = 8
QK_NOPE_DIM = 128
QK_ROPE_DIM = 64
V_HEAD_DIM = 128
Q_LORA_RANK = 512
KV_LORA_RANK = 512
MLA_DIM = N_HEADS * V_HEAD_DIM
MIX_DIM = FOURIER_DIM + MLA_DIM
IN_DIM = FOURIER_DIM + Q_LORA_RANK + KV_LORA_RANK + QK_ROPE_DIM
ROPE_THETA = 10000.0
Q_BLOCK = 128
N_MEM = 256
N_CROSS_HEADS = 4
CROSS_HEAD_DIM = D_MODEL // N_CROSS_HEADS
D_FF = 5632
CONV_WIDTH = 3
EPS = 1e-6

kernel_name = "hybrid_fourier_mla_encoder"


def rms_norm(x, g):
    x32 = x.astype(jnp.float32)
    y = x32 * lax.rsqrt(jnp.mean(x32 * x32, axis=-1, keepdims=True) + EPS)
    return (y * g.astype(jnp.float32)).astype(x.dtype)


def rope_tables(seq):
    inv = ROPE_THETA ** (-jnp.arange(0, QK_ROPE_DIM, 2, dtype=jnp.float32) / QK_ROPE_DIM)
    ang = jnp.arange(seq, dtype=jnp.float32)[:, None] * inv[None, :]
    return jnp.cos(ang), jnp.sin(ang)


def apply_rope(x, cos, sin):
    half = QK_ROPE_DIM // 2
    x1, x2 = x[..., :half], x[..., half:]
    c = cos.astype(x.dtype)
    s = sin.astype(x.dtype)
    return jnp.concatenate([x1 * c - x2 * s, x1 * s + x2 * c], axis=-1)


def fourier_mix(u):
    b, s, _ = u.shape
    ug = u.reshape(b, s, N_FOURIER_GROUPS, FOURIER_GROUP_DIM).astype(jnp.float32)
    f = jnp.fft.fft2(ug, axes=(1, 3), norm="ortho").real
    return f.reshape(b, s, FOURIER_DIM).astype(u.dtype)


def mla(c_q, c_kv, k_rope, q_norm_g, w_uq, kv_norm_g, w_ukv):
    b, s, _ = c_q.shape
    q = (rms_norm(c_q, q_norm_g) @ w_uq).reshape(b, s, N_HEADS, QK_NOPE_DIM + QK_ROPE_DIM)
    q_nope, q_rope = q[..., :QK_NOPE_DIM], q[..., QK_NOPE_DIM:]
    kv = (rms_norm(c_kv, kv_norm_g) @ w_ukv).reshape(b, s, N_HEADS, QK_NOPE_DIM + V_HEAD_DIM)
    k_nope, v = kv[..., :QK_NOPE_DIM], kv[..., QK_NOPE_DIM:]
    cos, sin = rope_tables(s)
    q_rope = apply_rope(q_rope, cos[:, None, :], sin[:, None, :])
    k_rope = apply_rope(k_rope, cos, sin)
    scale = (QK_NOPE_DIM + QK_ROPE_DIM) ** -0.5
    nb = s // Q_BLOCK
    qn_b = q_nope.reshape(b, nb, Q_BLOCK, N_HEADS, QK_NOPE_DIM).transpose(1, 0, 2, 3, 4)
    qr_b = q_rope.reshape(b, nb, Q_BLOCK, N_HEADS, QK_ROPE_DIM).transpose(1, 0, 2, 3, 4)

    def block(args):
        qn, qr = args
        sc = (jnp.einsum('bqhd,bkhd->bhqk', qn, k_nope)
              + jnp.einsum('bqhr,bkr->bhqk', qr, k_rope))
        p = jax.nn.softmax(sc.astype(jnp.float32) * scale, axis=-1).astype(v.dtype)
        return jnp.einsum('bhqk,bkhd->bqhd', p, v)

    o = lax.map(block, (qn_b, qr_b))
    return o.transpose(1, 0, 2, 3, 4).reshape(b, s, MLA_DIM)


def cross_attention(h, mem_n, w_cq, w_ckv, w_co):
    b, s, _ = h.shape
    m = mem_n.shape[1]
    q = (h @ w_cq).reshape(b, s, N_CROSS_HEADS, CROSS_HEAD_DIM)
    kv = (mem_n @ w_ckv).reshape(b, m, 2, N_CROSS_HEADS, CROSS_HEAD_DIM)
    k, v = kv[:, :, 0], kv[:, :, 1]
    sc = jnp.einsum('bqhd,bkhd->bhqk', q, k).astype(jnp.float32) * (CROSS_HEAD_DIM ** -0.5)
    p = jax.nn.softmax(sc, axis=-1).astype(v.dtype)
    o = jnp.einsum('bhqk,bkhd->bqhd', p, v).reshape(b, s, D_MODEL)
    return o @ w_co


def conv_ffn(h, w_gate, w_up, conv_w, conv_b, w_down):
    g = h @ w_gate
    u = h @ w_up
    gp = jnp.pad(g, ((0, 0), (1, 1), (0, 0)))
    g = gp[:, :-2] * conv_w[0] + gp[:, 1:-1] * conv_w[1] + gp[:, 2:] * conv_w[2] + conv_b
    return (jax.nn.silu(g) * u) @ w_down


def encoder_trunk(x, mem, norm_mix_g, w_in, q_norm_g, w_uq, kv_norm_g, w_ukv,
                  fourier_out_g, mla_out_g, w_out, norm_cross_g, norm_mem_g,
                  w_cq, w_ckv, w_co, norm_ffn_g, w_gate, w_up, conv_w, conv_b,
                  w_down, final_norm_g):
    c1 = FOURIER_DIM
    c2 = c1 + Q_LORA_RANK
    c3 = c2 + KV_LORA_RANK
    for l in range(DEPTH):
        h = rms_norm(x, norm_mix_g[l])
        z = h @ w_in[l]
        f_in, c_q, c_kv, k_r = z[..., :c1], z[..., c1:c2], z[..., c2:c3], z[..., c3:]
        f_out = rms_norm(fourier_mix(f_in), fourier_out_g[l])
        a_out = rms_norm(mla(c_q, c_kv, k_r, q_norm_g[l], w_uq[l], kv_norm_g[l], w_ukv[l]),
                         mla_out_g[l])
        x = x + jnp.concatenate([f_out, a_out], axis=-1) @ w_out[l]
        x = x + cross_attention(rms_norm(x, norm_cross_g[l]), rms_norm(mem, norm_mem_g[l]),
                                w_cq[l], w_ckv[l], w_co[l])
        x = x + conv_ffn(rms_norm(x, norm_ffn_g[l]), w_gate[l], w_up[l], conv_w[l],
                         conv_b[l], w_down[l])
    return rms_norm(x, final_norm_g)


def setup_inputs(seed: int = 0) -> dict:
    key = jax.random.key(seed)
    ks = jax.random.split(key, 32)
    f32 = jnp.float32

    def w(k, shape, fan_in):
        return jax.random.normal(k, shape, f32) * (fan_in ** -0.5)

    def gain(k, shape):
        return 1.0 + 0.02 * jax.random.normal(k, shape, f32)

    L = DEPTH
    return {
        "x_prompt": jax.random.normal(ks[0], (BATCH, SEQ, D_MODEL), f32),
        "x_sample": jax.random.normal(ks[1], (DEC_BATCH, DEC_SEQ, D_MODEL), f32),
        "mem_prompt": jax.random.normal(ks[2], (BATCH, N_MEM, D_MODEL), f32),
        "mem_sample": jax.random.normal(ks[3], (DEC_BATCH, N_MEM, D_MODEL), f32),
        "norm_mix_g": gain(ks[4], (L, D_MODEL)),
        "w_in": w(ks[5], (L, D_MODEL, IN_DIM), D_MODEL),
        "q_norm_g": gain(ks[6], (L, Q_LORA_RANK)),
        "w_uq": w(ks[7], (L, Q_LORA_RANK, N_HEADS * (QK_NOPE_DIM + QK_ROPE_DIM)), Q_LORA_RANK),
        "kv_norm_g": gain(ks[8], (L, KV_LORA_RANK)),
        "w_ukv": w(ks[9], (L, KV_LORA_RANK, N_HEADS * (QK_NOPE_DIM + V_HEAD_DIM)), KV_LORA_RANK),
        "fourier_out_g": gain(ks[10], (L, FOURIER_DIM)),
        "mla_out_g": gain(ks[11], (L, MLA_DIM)),
        "w_out": w(ks[12], (L, MIX_DIM, D_MODEL), MIX_DIM),
        "norm_cross_g": gain(ks[13], (L, D_MODEL)),
        "norm_mem_g": gain(ks[14], (L, D_MODEL)),
        "w_cq": w(ks[15], (L, D_MODEL, D_MODEL), D_MODEL),
        "w_ckv": w(ks[16], (L, D_MODEL, 2 * D_MODEL), D_MODEL),
        "w_co": w(ks[17], (L, D_MODEL, D_MODEL), D_MODEL),
        "norm_ffn_g": gain(ks[18], (L, D_MODEL)),
        "w_gate": w(ks[19], (L, D_MODEL, D_FF), D_MODEL),
        "w_up": w(ks[20], (L, D_MODEL, D_FF), D_MODEL),
        "conv_w": w(ks[21], (L, CONV_WIDTH, D_FF), CONV_WIDTH),
        "conv_b": 0.01 * jax.random.normal(ks[22], (L, D_FF), f32),
        "w_down": w(ks[23], (L, D_FF, D_MODEL), D_FF),
        "final_norm_g": gain(ks[24], (D_MODEL,)),
    }


def reference(x_prompt, x_sample, mem_prompt, mem_sample, norm_mix_g, w_in, q_norm_g, w_uq,
              kv_norm_g, w_ukv, fourier_out_g, mla_out_g, w_out, norm_cross_g, norm_mem_g,
              w_cq, w_ckv, w_co, norm_ffn_g, w_gate, w_up, conv_w, conv_b, w_down,
              final_norm_g):
    y_prompt = encoder_trunk(x_prompt, mem_prompt, norm_mix_g, w_in, q_norm_g, w_uq, kv_norm_g,
                             w_ukv, fourier_out_g, mla_out_g, w_out, norm_cross_g, norm_mem_g,
                             w_cq, w_ckv, w_co, norm_ffn_g, w_gate, w_up, conv_w, conv_b,
                             w_down, final_norm_g)
    y_sample = encoder_trunk(x_sample, mem_sample, norm_mix_g, w_in, q_norm_g, w_uq, kv_norm_g,
                             w_ukv, fourier_out_g, mla_out_g, w_out, norm_cross_g, norm_mem_g,
                             w_cq, w_ckv, w_co, norm_ffn_g, w_gate, w_up, conv_w, conv_b,
                             w_down, final_norm_g)
    return (y_prompt, y_sample)
```

```python
import functools

import numpy as np
import jax
import jax.numpy as jnp
from jax import lax
from jax.experimental import pallas as pl
from jax.experimental.pallas import tpu as pltpu

F32 = jnp.float32
BF16 = jnp.bfloat16

D_MODEL = 2048
N_FOURIER_GROUPS = 4
FOURIER_GROUP_DIM = 256
FOURIER_DIM = N_FOURIER_GROUPS * FOURIER_GROUP_DIM
N_HEADS = 8
QK_NOPE_DIM = 128
QK_ROPE_DIM = 64
QK_DIM = QK_NOPE_DIM + QK_ROPE_DIM
V_HEAD_DIM = 128
Q_LORA_RANK = 512
KV_LORA_RANK = 512
MLA_DIM = N_HEADS * V_HEAD_DIM
ROPE_THETA = 10000.0
N_CROSS_HEADS = 4
CROSS_HEAD_DIM = D_MODEL // N_CROSS_HEADS
D_FF = 5632
EPS = 1e-6

V7X_VMEM_LIMIT_BYTES = 56 * 1024 * 1024
DFT_N1 = 128
HALO = 16


def _params(*sem):
    return pltpu.CompilerParams(dimension_semantics=sem, vmem_limit_bytes=V7X_VMEM_LIMIT_BYTES)


def _const_spec(shape):
    zeros = (0,) * len(shape)
    return pl.BlockSpec(shape, lambda *_: zeros, pipeline_mode=pl.Buffered(1))


def _rms(x, g):
    return x * lax.rsqrt(jnp.mean(x * x, axis=-1, keepdims=True) + EPS) * g


def _dot(a, b):
    return jnp.dot(a, b, preferred_element_type=F32)


def _dot_nt(a, b):
    return lax.dot_general(a, b, (((1,), (1,)), ((), ())), preferred_element_type=F32)


def _in_proj_kernel(x_ref, g_ref, win_ref, qg_ref, wuq_ref, kvg_ref, wukv_ref, dft_ref, cos_ref,
                    sin_ref, ab_ref, q_ref, k_ref, v_ref):
    h = _rms(x_ref[...], g_ref[...]).astype(BF16)
    z = _dot(h, win_ref[...])

    for g in range(N_FOURIER_GROUPS):
        u = z[:, g * FOURIER_GROUP_DIM:(g + 1) * FOURIER_GROUP_DIM].astype(BF16)
        ab = _dot(u, dft_ref[...])
        ab_ref[g, 0] = ab[:, :FOURIER_GROUP_DIM].astype(BF16)
        ab_ref[g, 1] = ab[:, FOURIER_GROUP_DIM:].astype(BF16)

    c0 = FOURIER_DIM
    c_q = z[:, c0:c0 + Q_LORA_RANK]
    c_kv = z[:, c0 + Q_LORA_RANK:c0 + Q_LORA_RANK + KV_LORA_RANK]
    c1 = c0 + Q_LORA_RANK + KV_LORA_RANK
    k_r = z[:, c1:c1 + QK_ROPE_DIM]
    k_r_sw = z[:, c1 + QK_ROPE_DIM:c1 + 2 * QK_ROPE_DIM]

    cos = cos_ref[...]
    sin = sin_ref[...]
    k_rope = (k_r * cos[:, :QK_ROPE_DIM] + k_r_sw * sin[:, :QK_ROPE_DIM]).astype(BF16)

    hq = _rms(c_q, qg_ref[...]).astype(BF16)
    qe = _dot(hq, wuq_ref[...])
    n_rope = N_HEADS * QK_ROPE_DIM
    cos_h = jnp.concatenate([cos] * (n_rope // 128), axis=1)
    sin_h = jnp.concatenate([sin] * (n_rope // 128), axis=1)
    q_nope_w = N_HEADS * QK_NOPE_DIM
    q_rope = qe[:, q_nope_w:q_nope_w + n_rope] * cos_h + qe[:, q_nope_w + n_rope:] * sin_h

    hkv = _rms(c_kv, kvg_ref[...]).astype(BF16)
    kv = _dot(hkv, wukv_ref[...])

    scale = QK_DIM ** -0.5
    for hd in range(N_HEADS):
        q_ref[hd, :, 0:QK_NOPE_DIM] = (qe[:, hd * QK_NOPE_DIM:(hd + 1) * QK_NOPE_DIM] * scale).astype(BF16)
        q_ref[hd, :, QK_NOPE_DIM:QK_DIM] = (
            q_rope[:, hd * QK_ROPE_DIM:(hd + 1) * QK_ROPE_DIM] * scale).astype(BF16)
        k_ref[hd, :, 0:QK_NOPE_DIM] = kv[:, hd * QK_NOPE_DIM:(hd + 1) * QK_NOPE_DIM].astype(BF16)
        k_ref[hd, :, QK_NOPE_DIM:QK_DIM] = k_rope
        v_ref[hd] = kv[:, q_nope_w + hd * V_HEAD_DIM:q_nope_w + (hd + 1) * V_HEAD_DIM].astype(BF16)


def _in_proj(x, g_mix, w_in_ext, q_g, w_uq_ext, kv_g, w_ukv_p, dft_c, cos2, sin2, *, tm):
    b, s, _ = x.shape
    grid = (b, s // tm)
    return pl.pallas_call(
        _in_proj_kernel,
        grid=grid,
        in_specs=[
            pl.BlockSpec((None, tm, D_MODEL), lambda bi, i: (bi, i, 0)),
            _const_spec(g_mix.shape),
            _const_spec(w_in_ext.shape),
            _const_spec(q_g.shape),
            _const_spec(w_uq_ext.shape),
            _const_spec(kv_g.shape),
            _const_spec(w_ukv_p.shape),
            _const_spec(dft_c.shape),
            pl.BlockSpec((tm, 128), lambda bi, i: (i, 0)),
            pl.BlockSpec((tm, 128), lambda bi, i: (i, 0)),
        ],
        out_specs=[
            pl.BlockSpec((None, N_FOURIER_GROUPS, 2, tm, FOURIER_GROUP_DIM), lambda bi, i: (bi, 0, 0, i, 0)),
            pl.BlockSpec((None, N_HEADS, tm, QK_DIM), lambda bi, i: (bi, 0, i, 0)),
            pl.BlockSpec((None, N_HEADS, tm, QK_DIM), lambda bi, i: (bi, 0, i, 0)),
            pl.BlockSpec((None, N_HEADS, tm, V_HEAD_DIM), lambda bi, i: (bi, 0, i, 0)),
        ],
        out_shape=[
            jax.ShapeDtypeStruct((b, N_FOURIER_GROUPS, 2, s, FOURIER_GROUP_DIM), BF16),
            jax.ShapeDtypeStruct((b, N_HEADS, s, QK_DIM), BF16),
            jax.ShapeDtypeStruct((b, N_HEADS, s, QK_DIM), BF16),
            jax.ShapeDtypeStruct((b, N_HEADS, s, V_HEAD_DIM), BF16),
        ],
        compiler_params=_params("parallel", "parallel"),
        name="in_proj",
    )(x, g_mix, w_in_ext, q_g, w_uq_ext, kv_g, w_ukv_p, dft_c, cos2, sin2)


def _fourier_a_kernel(m_ref, x_ref, o_ref):
    two, n1, tc = x_ref.shape
    x = x_ref[...].reshape(two * n1, tc)
    o_ref[...] = _dot(m_ref[...], x).astype(BF16).reshape(two, n1, tc)


def _fourier_a(ab, m1, *, tc):
    b, g, two, n1, cols = ab.shape
    spec = pl.BlockSpec((None, None, two, n1, tc), lambda bi, gi, ci: (bi, gi, 0, 0, ci))
    return pl.pallas_call(
        _fourier_a_kernel,
        grid=(b, g, cols // tc),
        in_specs=[_const_spec(m1.shape), spec],
        out_specs=spec,
        out_shape=jax.ShapeDtypeStruct(ab.shape, BF16),
        compiler_params=_params("parallel", "parallel", "parallel"),
        name="fourier_a",
    )(m1, ab)


def _fourier_b_kernel(e_ref, g_ref, o_ref, *, scale):
    kb = e_ref.shape[0]
    for j in range(kb):
        x = jnp.concatenate([g_ref[0, j], g_ref[1, j]], axis=0)
        o_ref[:, j, :] = _dot(e_ref[j], x) * scale


def _fourier_b(gc, e, *, kb, scale):
    b, g, two, n1, n2, c = gc.shape
    return pl.pallas_call(
        functools.partial(_fourier_b_kernel, scale=scale),
        grid=(b, g, n1 // kb),
        in_specs=[
            pl.BlockSpec((kb, n2, 2 * n2), lambda bi, gi, ki: (ki, 0, 0)),
            pl.BlockSpec((None, None, two, kb, n2, c), lambda bi, gi, ki: (bi, gi, 0, ki, 0, 0)),
        ],
        out_specs=pl.BlockSpec((None, n2, kb, c), lambda bi, gi, ki: (bi, 0, ki, gi)),
        out_shape=jax.ShapeDtypeStruct((b, n2, n1, g * c), F32),
        compiler_params=_params("parallel", "parallel", "parallel"),
        name="fourier_b",
    )(e, gc)


def _attn_kernel(q_ref, k_ref, v_ref, o_ref, *, tk):
    q = q_ref[...]
    tq = q.shape[0]
    nk = k_ref.shape[0] // tk

    def body(j, carry):
        m, l, acc = carry
        start = pl.multiple_of(j * tk, tk)
        s = _dot_nt(q, k_ref[pl.ds(start, tk), :])
        m_new = jnp.maximum(m, jnp.max(s, axis=-1, keepdims=True))
        p = jnp.exp(s - m_new)
        alpha = jnp.exp(m - m_new)
        l = alpha * l + jnp.sum(p, axis=-1, keepdims=True)
        acc = alpha * acc + _dot(p.astype(BF16), v_ref[pl.ds(start, tk), :])
        return m_new, l, acc

    init = (jnp.full((tq, 1), -jnp.inf, F32), jnp.zeros((tq, 1), F32), jnp.zeros((tq, V_HEAD_DIM), F32))
    _, l, acc = lax.fori_loop(0, nk, body, init)
    o_ref[...] = acc / l


def _attention(q, k, v, *, tq, tk):
    b, h, s, _ = q.shape
    return pl.pallas_call(
        functools.partial(_attn_kernel, tk=tk),
        grid=(b, h, s // tq),
        in_specs=[
            pl.BlockSpec((None, None, tq, QK_DIM), lambda bi, hi, i: (bi, hi, i, 0)),
            pl.BlockSpec((None, None, s, QK_DIM), lambda bi, hi, i: (bi, hi, 0, 0)),
            pl.BlockSpec((None, None, s, V_HEAD_DIM), lambda bi, hi, i: (bi, hi, 0, 0)),
        ],
        out_specs=pl.BlockSpec((None, tq, V_HEAD_DIM), lambda bi, hi, i: (bi, i, hi)),
        out_shape=jax.ShapeDtypeStruct((b, s, h * V_HEAD_DIM), F32),
        compiler_params=_params("parallel", "parallel", "parallel"),
        name="attention",
    )(q, k, v)


def _out_proj_kernel(x_ref, f_ref, a_ref, gf_ref, ga_ref, w_ref, o_ref):
    hf = _rms(f_ref[...], gf_ref[...]).astype(BF16)
    ha = _rms(a_ref[...], ga_ref[...]).astype(BF16)
    o_ref[...] = x_ref[...] + _dot(hf, w_ref[0:FOURIER_DIM, :]) + _dot(ha, w_ref[FOURIER_DIM:, :])


def _out_proj(x, f, a, g_f, g_a, w_out, *, tm):
    t = x.shape[0]
    return pl.pallas_call(
        _out_proj_kernel,
        grid=(t // tm,),
        in_specs=[
            pl.BlockSpec((tm, D_MODEL), lambda i: (i, 0)),
            pl.BlockSpec((tm, FOURIER_DIM), lambda i: (i, 0)),
            pl.BlockSpec((tm, MLA_DIM), lambda i: (i, 0)),
            _const_spec(g_f.shape),
            _const_spec(g_a.shape),
            _const_spec(w_out.shape),
        ],
        out_specs=pl.BlockSpec((tm, D_MODEL), lambda i: (i, 0)),
        out_shape=jax.ShapeDtypeStruct((t, D_MODEL), F32),
        compiler_params=_params("parallel"),
        name="out_proj",
    )(x, f, a, g_f, g_a, w_out)


def _mem_kv_kernel(m_ref, g_ref, w_ref, o_ref):
    o_ref[...] = _dot(_rms(m_ref[...], g_ref[...]).astype(BF16), w_ref[...]).astype(BF16)


def _mem_kv(mem, g_mem, w_ckv, *, tn):
    b, m, _ = mem.shape
    n = w_ckv.shape[1]
    return pl.pallas_call(
        _mem_kv_kernel,
        grid=(b, n // tn),
        in_specs=[
            pl.BlockSpec((None, m, D_MODEL), lambda bi, j: (bi, 0, 0)),
            _const_spec(g_mem.shape),
            pl.BlockSpec((D_MODEL, tn), lambda bi, j: (0, j)),
        ],
        out_specs=pl.BlockSpec((None, m, tn), lambda bi, j: (bi, 0, j)),
        out_shape=jax.ShapeDtypeStruct((b, m, n), BF16),
        compiler_params=_params("parallel", "parallel"),
        name="mem_kv",
    )(mem, g_mem, w_ckv)


def _cross_kernel(x_ref, kv_ref, gc_ref, wq_ref, wo_ref, gn_ref, o_ref, h_ref):
    x = x_ref[...]
    h = _rms(x, gc_ref[...]).astype(BF16)
    q = _dot(h, wq_ref[...]) * (CROSS_HEAD_DIM ** -0.5)
    heads = []
    for hd in range(N_CROSS_HEADS):
        lo = hd * CROSS_HEAD_DIM
        s = _dot_nt(q[:, lo:lo + CROSS_HEAD_DIM].astype(BF16), kv_ref[:, lo:lo + CROSS_HEAD_DIM])
        p = jnp.exp(s - jnp.max(s, axis=-1, keepdims=True))
        p = p / jnp.sum(p, axis=-1, keepdims=True)
        heads.append(_dot(p.astype(BF16), kv_ref[:, D_MODEL + lo:D_MODEL + lo + CROSS_HEAD_DIM]).astype(BF16))
    y = x + _dot(jnp.concatenate(heads, axis=1), wo_ref[...])
    o_ref[...] = y
    h_ref[...] = _rms(y, gn_ref[...]).astype(BF16)


def _cross(x, kvm, g_cross, w_cq, w_co, g_ffn, *, tm):
    b, s, _ = x.shape
    m = kvm.shape[1]
    spec = pl.BlockSpec((None, tm, D_MODEL), lambda bi, i: (bi, i, 0))
    return pl.pallas_call(
        _cross_kernel,
        grid=(b, s // tm),
        in_specs=[
            spec,
            pl.BlockSpec((None, m, 2 * D_MODEL), lambda bi, i: (bi, 0, 0)),
            _const_spec(g_cross.shape),
            _const_spec(w_cq.shape),
            _const_spec(w_co.shape),
            _const_spec(g_ffn.shape),
        ],
        out_specs=[spec, spec],
        out_shape=[jax.ShapeDtypeStruct((b, s, D_MODEL), F32), jax.ShapeDtypeStruct((b, s, D_MODEL), BF16)],
        compiler_params=_params("parallel", "parallel"),
        name="cross",
    )(x, kvm, g_cross, w_cq, w_co, g_ffn)


def _ffn_up_kernel(h_ref, prev_ref, next_ref, wg_ref, wu_ref, cw_ref, cb_ref, o_ref, hext_ref, *, seq):
    i = pl.program_id(0)
    tm = h_ref.shape[0]

    @pl.when(pl.program_id(1) == 0)
    def _():
        first = (i * tm) % seq == 0
        last = ((i + 1) * tm) % seq == 0
        prev = prev_ref[...]
        nxt = next_ref[...]
        hext_ref[0:HALO, :] = jnp.where(first, jnp.zeros_like(prev), prev)
        hext_ref[HALO:HALO + tm, :] = h_ref[...]
        hext_ref[HALO + tm:, :] = jnp.where(last, jnp.zeros_like(nxt), nxt)

    g = _dot(hext_ref[...], wg_ref[...])
    u = _dot(h_ref[...], wu_ref[...])
    rows = g.shape[0]
    g_prev = pltpu.roll(g, 1, axis=0)[HALO:HALO + tm]
    g_next = pltpu.roll(g, rows - 1, axis=0)[HALO:HALO + tm]
    cw = cw_ref[...]
    c = g_prev * cw[0:1] + g[HALO:HALO + tm] * cw[1:2] + g_next * cw[2:3] + cb_ref[...]
    o_ref[...] = (c / (1.0 + jnp.exp(-c)) * u).astype(BF16)


def _ffn_up(h, w_gate, w_up, conv_w, conv_b, *, seq, tm, tf):
    t = h.shape[0]
    hb = tm // HALO
    n_halo = t // HALO
    return pl.pallas_call(
        functools.partial(_ffn_up_kernel, seq=seq),
        grid=(t // tm, D_FF // tf),
        in_specs=[
            pl.BlockSpec((tm, D_MODEL), lambda i, j: (i, 0)),
            pl.BlockSpec((HALO, D_MODEL), lambda i, j: (jnp.maximum(i * hb - 1, 0), 0)),
            pl.BlockSpec((HALO, D_MODEL), lambda i, j: (jnp.minimum((i + 1) * hb, n_halo - 1), 0)),
            pl.BlockSpec((D_MODEL, tf), lambda i, j: (0, j)),
            pl.BlockSpec((D_MODEL, tf), lambda i, j: (0, j)),
            pl.BlockSpec((3, tf), lambda i, j: (0, j)),
            pl.BlockSpec((1, tf), lambda i, j: (0, j)),
        ],
        out_specs=pl.BlockSpec((tm, tf), lambda i, j: (i, j)),
        out_shape=jax.ShapeDtypeStruct((t, D_FF), BF16),
        scratch_shapes=[pltpu.VMEM((tm + 2 * HALO, D_MODEL), BF16)],
        compiler_params=_params("parallel", "arbitrary"),
        name="ffn_up",
    )(h, h, h, w_gate, w_up, conv_w, conv_b)


def _ffn_down_kernel(a_ref, w_ref, x_ref, g_ref, o_ref, acc_ref):
    k = pl.program_id(1)

    @pl.when(k == 0)
    def _():
        acc_ref[...] = jnp.zeros_like(acc_ref)

    acc_ref[...] += _dot(a_ref[...], w_ref[...])

    @pl.when(k == pl.num_programs(1) - 1)
    def _():
        o_ref[...] = _rms(x_ref[...] + acc_ref[...], g_ref[...])


def _ffn_down(act, w_down, x, g_final, *, tm, tk):
    t = x.shape[0]
    return pl.pallas_call(
        _ffn_down_kernel,
        grid=(t // tm, D_FF // tk),
        in_specs=[
            pl.BlockSpec((tm, tk), lambda i, k: (i, k)),
            pl.BlockSpec((tk, D_MODEL), lambda i, k: (k, 0)),
            pl.BlockSpec((tm, D_MODEL), lambda i, k: (i, 0)),
            _const_spec(g_final.shape),
        ],
        out_specs=pl.BlockSpec((tm, D_MODEL), lambda i, k: (i, 0)),
        out_shape=jax.ShapeDtypeStruct((t, D_MODEL), F32),
        scratch_shapes=[pltpu.VMEM((tm, D_MODEL), F32)],
        compiler_params=_params("parallel", "arbitrary"),
        name="ffn_down",
    )(act, w_down, x, g_final)


def _dft_tables(seq):
    n1 = DFT_N1
    n2 = seq // n1
    c = np.arange(FOURIER_GROUP_DIM)
    ang = 2.0 * np.pi * ((c[:, None] * c[None, :]) % FOURIER_GROUP_DIM) / FOURIER_GROUP_DIM
    dft_c = np.concatenate([np.cos(ang), np.sin(ang)], axis=1)

    j = np.arange(n1)
    a1 = 2.0 * np.pi * ((j[:, None] * j[None, :]) % n1) / n1
    c1, s1 = np.cos(a1), np.sin(a1)
    m1 = np.block([[c1, -s1], [-s1, -c1]])

    k1 = np.arange(n1)[:, None, None]
    k2 = np.arange(n2)[None, :, None]
    m2 = np.arange(n2)[None, None, :]
    a2 = 2.0 * np.pi * ((m2 * (k1 + n1 * k2)) % seq) / seq
    e = np.concatenate([np.cos(a2), np.sin(a2)], axis=2)
    return (jnp.asarray(dft_c, BF16), jnp.asarray(m1, BF16), jnp.asarray(e, BF16))


def _rope_tables(seq):
    inv = ROPE_THETA ** (-jnp.arange(0, QK_ROPE_DIM, 2, dtype=F32) / QK_ROPE_DIM)
    ang = jnp.arange(seq, dtype=F32)[:, None] * inv[None, :]
    cos, sin = jnp.cos(ang), jnp.sin(ang)
    return jnp.concatenate([cos, cos, cos, cos], axis=1), jnp.concatenate([-sin, sin, -sin, sin], axis=1)


def _prep_weights(w_in, w_uq, w_ukv):
    half = QK_ROPE_DIM // 2
    c1 = FOURIER_DIM + Q_LORA_RANK + KV_LORA_RANK
    w_in_ext = jnp.concatenate(
        [w_in, w_in[:, c1 + half:c1 + QK_ROPE_DIM], w_in[:, c1:c1 + half]], axis=1).astype(BF16)
    q3 = w_uq.reshape(Q_LORA_RANK, N_HEADS, QK_DIM)
    rope = q3[:, :, QK_NOPE_DIM:]
    rope_sw = jnp.concatenate([rope[:, :, half:], rope[:, :, :half]], axis=2)
    w_uq_ext = jnp.concatenate(
        [q3[:, :, :QK_NOPE_DIM].reshape(Q_LORA_RANK, -1), rope.reshape(Q_LORA_RANK, -1),
         rope_sw.reshape(Q_LORA_RANK, -1)], axis=1).astype(BF16)
    kv3 = w_ukv.reshape(KV_LORA_RANK, N_HEADS, QK_NOPE_DIM + V_HEAD_DIM)
    w_ukv_p = jnp.concatenate(
        [kv3[:, :, :QK_NOPE_DIM].reshape(KV_LORA_RANK, -1), kv3[:, :, QK_NOPE_DIM:].reshape(KV_LORA_RANK, -1)],
        axis=1).astype(BF16)
    return w_in_ext, w_uq_ext, w_ukv_p


def _tile(n, pref):
    return pref if n % pref == 0 else n


def _trunk(x, mem, p, tables):
    b, s, _ = x.shape
    t = b * s
    n1 = DFT_N1
    n2 = s // n1
    dft_c, m1, e, cos2, sin2 = tables

    ab, q, k, v = _in_proj(x, p["g_mix"], p["w_in_ext"], p["g_q"], p["w_uq_ext"], p["g_kv"], p["w_ukv_p"],
                           dft_c, cos2, sin2, tm=_tile(s, 256))

    cols = n2 * FOURIER_GROUP_DIM
    gc = _fourier_a(ab.reshape(b, N_FOURIER_GROUPS, 2, n1, cols), m1, tc=_tile(cols, 4096))
    f = _fourier_b(gc.reshape(b, N_FOURIER_GROUPS, 2, n1, n2, FOURIER_GROUP_DIM), e, kb=32,
                   scale=float((s * FOURIER_GROUP_DIM) ** -0.5))

    a = _attention(q, k, v, tq=_tile(s, 512), tk=_tile(s, 512))

    x1 = _out_proj(x.reshape(t, D_MODEL), f.reshape(t, FOURIER_DIM), a.reshape(t, MLA_DIM),
                   p["g_f"], p["g_a"], p["w_out"], tm=_tile(t, 512))

    kvm = _mem_kv(mem, p["g_mem"], p["w_ckv"], tn=1024)
    x2, hf = _cross(x1.reshape(b, s, D_MODEL), kvm, p["g_cross"], p["w_cq"], p["w_co"], p["g_ffn"],
                    tm=_tile(s, 256))

    act = _ffn_up(hf.reshape(t, D_MODEL), p["w_gate"], p["w_up"], p["conv_w"], p["conv_b"], seq=s,
                  tm=_tile(s, 1024), tf=512)
    y = _ffn_down(act, p["w_down"], x2.reshape(t, D_MODEL), p["g_final"], tm=_tile(t, 512), tk=512)
    return y.reshape(b, s, D_MODEL)


@jax.jit
def kernel(x_prompt, x_sample, mem_prompt, mem_sample, norm_mix_g, w_in, q_norm_g, w_uq, kv_norm_g, w_ukv,
           fourier_out_g, mla_out_g, w_out, norm_cross_g, norm_mem_g, w_cq, w_ckv, w_co, norm_ffn_g, w_gate,
           w_up, conv_w, conv_b, w_down, final_norm_g):
    assert norm_mix_g.shape[0] == 1, "single-layer trunk"
    assert x_prompt.shape[1] % DFT_N1 == 0 and x_sample.shape[1] % DFT_N1 == 0
    w_in_ext, w_uq_ext, w_ukv_p = _prep_weights(w_in[0], w_uq[0], w_ukv[0])
    row = lambda g: g.reshape(1, -1)
    p = dict(
        g_mix=row(norm_mix_g[0]), w_in_ext=w_in_ext, g_q=row(q_norm_g[0]), w_uq_ext=w_uq_ext,
        g_kv=row(kv_norm_g[0]), w_ukv_p=w_ukv_p, g_f=row(fourier_out_g[0]), g_a=row(mla_out_g[0]),
        w_out=w_out[0].astype(BF16), g_cross=row(norm_cross_g[0]), g_mem=row(norm_mem_g[0]),
        w_cq=w_cq[0].astype(BF16), w_ckv=w_ckv[0].astype(BF16), w_co=w_co[0].astype(BF16),
        g_ffn=row(norm_ffn_g[0]), w_gate=w_gate[0].astype(BF16), w_up=w_up[0].astype(BF16),
        conv_w=conv_w[0], conv_b=row(conv_b[0]), w_down=w_down[0].astype(BF16), g_final=row(final_norm_g),
    )
    outs = []
    for x, mem in ((x_prompt, mem_prompt), (x_sample, mem_sample)):
        s = x.shape[1]
        tables = _dft_tables(s) + _rope_tables(s)
        outs.append(_trunk(x, mem, p, tables))
    return tuple(outs)
```

```python
import functools

import numpy as np
import jax
import jax.numpy as jnp
from jax import lax
from jax.experimental import pallas as pl
from jax.experimental.pallas import tpu as pltpu

F32 = jnp.float32
BF16 = jnp.bfloat16

D_MODEL = 2048
N_FOURIER_GROUPS = 4
FOURIER_GROUP_DIM = 256
FOURIER_DIM = N_FOURIER_GROUPS * FOURIER_GROUP_DIM
N_HEADS = 8
QK_NOPE_DIM = 128
QK_ROPE_DIM = 64
QK_DIM = QK_NOPE_DIM + QK_ROPE_DIM
V_HEAD_DIM = 128
Q_LORA_RANK = 512
KV_LORA_RANK = 512
MLA_DIM = N_HEADS * V_HEAD_DIM
ROPE_THETA = 10000.0
N_CROSS_HEADS = 4
CROSS_HEAD_DIM = D_MODEL // N_CROSS_HEADS
D_FF = 5632
EPS = 1e-6
LOG2_E = 1.4426950408889634

V7X_VMEM_LIMIT_BYTES = 56 * 1024 * 1024
DFT_N1 = 128
HALO = 16


def _params(*sem):
    return pltpu.CompilerParams(dimension_semantics=sem, vmem_limit_bytes=V7X_VMEM_LIMIT_BYTES)


def _const_spec(shape):
    zeros = (0,) * len(shape)
    return pl.BlockSpec(shape, lambda *_: zeros, pipeline_mode=pl.Buffered(1))


def _rms(x, g):
    return x * lax.rsqrt(jnp.mean(x * x, axis=-1, keepdims=True) + EPS) * g


def _dot(a, b):
    return jnp.dot(a, b, preferred_element_type=F32)


def _dot_nt(a, b):
    return lax.dot_general(a, b, (((1,), (1,)), ((), ())), preferred_element_type=F32)


def _in_proj_kernel(x_ref, g_ref, win_ref, qg_ref, wuq_ref, kvg_ref, wk_ref, wvt_ref, dft_ref, cos_ref,
                    sin_ref, ab_ref, q_ref, k_ref, vt_ref):
    h = _rms(x_ref[...], g_ref[...]).astype(BF16)
    z = _dot(h, win_ref[...])

    for g in range(N_FOURIER_GROUPS):
        u = z[:, g * FOURIER_GROUP_DIM:(g + 1) * FOURIER_GROUP_DIM].astype(BF16)
        ab = _dot(u, dft_ref[...])
        ab_ref[g, 0] = ab[:, :FOURIER_GROUP_DIM].astype(BF16)
        ab_ref[g, 1] = ab[:, FOURIER_GROUP_DIM:].astype(BF16)

    c0 = FOURIER_DIM
    c_q = z[:, c0:c0 + Q_LORA_RANK]
    c_kv = z[:, c0 + Q_LORA_RANK:c0 + Q_LORA_RANK + KV_LORA_RANK]
    c1 = c0 + Q_LORA_RANK + KV_LORA_RANK
    k_r = z[:, c1:c1 + QK_ROPE_DIM]
    k_r_sw = z[:, c1 + QK_ROPE_DIM:c1 + 2 * QK_ROPE_DIM]

    cos = cos_ref[...]
    sin = sin_ref[...]
    k_rope = (k_r * cos[:, :QK_ROPE_DIM] + k_r_sw * sin[:, :QK_ROPE_DIM]).astype(BF16)

    hq = _rms(c_q, qg_ref[...]).astype(BF16)
    qe = _dot(hq, wuq_ref[...])
    n_rope = N_HEADS * QK_ROPE_DIM
    cos_h = jnp.concatenate([cos] * (n_rope // 128), axis=1)
    sin_h = jnp.concatenate([sin] * (n_rope // 128), axis=1)
    q_nope_w = N_HEADS * QK_NOPE_DIM
    q_rope = qe[:, q_nope_w:q_nope_w + n_rope] * cos_h + qe[:, q_nope_w + n_rope:] * sin_h

    hkv = _rms(c_kv, kvg_ref[...]).astype(BF16)
    k_nope = _dot(hkv, wk_ref[...])
    v_t = _dot_nt(wvt_ref[...], hkv)

    scale = QK_DIM ** -0.5 * LOG2_E
    for hd in range(N_HEADS):
        q_ref[hd, :, 0:QK_NOPE_DIM] = (qe[:, hd * QK_NOPE_DIM:(hd + 1) * QK_NOPE_DIM] * scale).astype(BF16)
        q_ref[hd, :, QK_NOPE_DIM:QK_DIM] = (
            q_rope[:, hd * QK_ROPE_DIM:(hd + 1) * QK_ROPE_DIM] * scale).astype(BF16)
        k_ref[hd, :, 0:QK_NOPE_DIM] = k_nope[:, hd * QK_NOPE_DIM:(hd + 1) * QK_NOPE_DIM].astype(BF16)
        k_ref[hd, :, QK_NOPE_DIM:QK_DIM] = k_rope
        vt_ref[hd] = v_t[hd * V_HEAD_DIM:(hd + 1) * V_HEAD_DIM, :].astype(BF16)


def _in_proj(x, g_mix, w_in_ext, q_g, w_uq_ext, kv_g, w_k, w_vt, dft_c, cos2, sin2, *, tm, tk):
    b, s, _ = x.shape
    grid = (b, s // tm)
    r = tk // tm
    return pl.pallas_call(
        _in_proj_kernel,
        grid=grid,
        in_specs=[
            pl.BlockSpec((None, tm, D_MODEL), lambda bi, i: (bi, i, 0)),
            _const_spec(g_mix.shape),
            _const_spec(w_in_ext.shape),
            _const_spec(q_g.shape),
            _const_spec(w_uq_ext.shape),
            _const_spec(kv_g.shape),
            _const_spec(w_k.shape),
            _const_spec(w_vt.shape),
            _const_spec(dft_c.shape),
            pl.BlockSpec((tm, 128), lambda bi, i: (i, 0)),
            pl.BlockSpec((tm, 128), lambda bi, i: (i, 0)),
        ],
        out_specs=[
            pl.BlockSpec((None, N_FOURIER_GROUPS, 2, tm, FOURIER_GROUP_DIM), lambda bi, i: (bi, 0, 0, i, 0)),
            pl.BlockSpec((None, N_HEADS, tm, QK_DIM), lambda bi, i: (bi, 0, i, 0)),
            pl.BlockSpec((None, N_HEADS, tm, QK_DIM), lambda bi, i: (bi, 0, i, 0)),
            pl.BlockSpec((None, N_HEADS, None, V_HEAD_DIM, tm), lambda bi, i: (bi, 0, i // r, 0, i % r)),
        ],
        out_shape=[
            jax.ShapeDtypeStruct((b, N_FOURIER_GROUPS, 2, s, FOURIER_GROUP_DIM), BF16),
            jax.ShapeDtypeStruct((b, N_HEADS, s, QK_DIM), BF16),
            jax.ShapeDtypeStruct((b, N_HEADS, s, QK_DIM), BF16),
            jax.ShapeDtypeStruct((b, N_HEADS, s // tk, V_HEAD_DIM, tk), BF16),
        ],
        compiler_params=_params("parallel", "parallel"),
        name="in_proj",
    )(x, g_mix, w_in_ext, q_g, w_uq_ext, kv_g, w_k, w_vt, dft_c, cos2, sin2)


def _fourier_a_kernel(m_ref, x_ref, o_ref):
    two, n1, tc = x_ref.shape
    x = x_ref[...].reshape(two * n1, tc)
    o_ref[...] = _dot(m_ref[...], x).astype(BF16).reshape(two, n1, tc)


def _fourier_a(ab, m1, *, tc):
    b, g, two, n1, cols = ab.shape
    spec = pl.BlockSpec((None, None, two, n1, tc), lambda bi, gi, ci: (bi, gi, 0, 0, ci))
    return pl.pallas_call(
        _fourier_a_kernel,
        grid=(b, g, cols // tc),
        in_specs=[_const_spec(m1.shape), spec],
        out_specs=spec,
        out_shape=jax.ShapeDtypeStruct(ab.shape, BF16),
        compiler_params=_params("parallel", "parallel", "parallel"),
        name="fourier_a",
    )(m1, ab)


def _fourier_b_kernel(e_ref, g_ref, o_ref, *, scale):
    kb = e_ref.shape[0]
    for j in range(kb):
        x = jnp.concatenate([g_ref[0, j], g_ref[1, j]], axis=0)
        o_ref[:, j, :] = _dot(e_ref[j], x) * scale


def _fourier_b(gc, e, *, kb, scale):
    b, g, two, n1, n2, c = gc.shape
    return pl.pallas_call(
        functools.partial(_fourier_b_kernel, scale=scale),
        grid=(b, g, n1 // kb),
        in_specs=[
            pl.BlockSpec((kb, n2, 2 * n2), lambda bi, gi, ki: (ki, 0, 0)),
            pl.BlockSpec((None, None, two, kb, n2, c), lambda bi, gi, ki: (bi, gi, 0, ki, 0, 0)),
        ],
        out_specs=pl.BlockSpec((None, n2, kb, c), lambda bi, gi, ki: (bi, 0, ki, gi)),
        out_shape=jax.ShapeDtypeStruct((b, n2, n1, g * c), F32),
        compiler_params=_params("parallel", "parallel", "parallel"),
        name="fourier_b",
    )(e, gc)


def _attn_kernel(q_ref, k_ref, vt_ref, o_ref, sa_ref, sb_ref, acc_ref):
    q = q_ref[...]
    tq = q.shape[0]
    nk, _, tk = vt_ref.shape

    def scores(j, s_ref):
        start = pl.multiple_of(j * tk, tk)
        s = _dot_nt(k_ref[pl.ds(start, tk), :], q)
        s_ref[...] = s
        return jnp.max(s, axis=0, keepdims=True)

    def update(j, s_ref, cmax, m, l):
        m_new = jnp.maximum(m, cmax)
        p = jnp.exp2(s_ref[...] - m_new)
        alpha = jnp.exp2(m - m_new)
        l = alpha * l + jnp.sum(p, axis=0, keepdims=True)
        acc_ref[...] = alpha * acc_ref[...] + _dot(vt_ref[j], p.astype(BF16))
        return m_new, l

    def pair(j, ca, m, l, more):
        cb = scores(j + 1, sb_ref)
        m, l = update(j, sa_ref, ca, m, l)
        if more:
            ca = scores(j + 2, sa_ref)
        m, l = update(j + 1, sb_ref, cb, m, l)
        return ca, m, l

    acc_ref[...] = jnp.zeros_like(acc_ref)
    carry = (scores(0, sa_ref), jnp.full((1, tq), -jnp.inf, F32), jnp.zeros((1, tq), F32))
    carry = lax.fori_loop(0, nk // 2 - 1, lambda i, c: pair(2 * i, *c, True), carry)
    _, _, l = pair(nk - 2, *carry, False)
    o_ref[...] = (acc_ref[...] / l).T


def _attention(q, k, vt, *, tq):
    b, h, s, _ = q.shape
    nk, _, tk = vt.shape[2:]
    assert nk % 2 == 0
    return pl.pallas_call(
        _attn_kernel,
        grid=(b, h, s // tq),
        in_specs=[
            pl.BlockSpec((None, None, tq, QK_DIM), lambda bi, hi, i: (bi, hi, i, 0)),
            pl.BlockSpec((None, None, s, QK_DIM), lambda bi, hi, i: (bi, hi, 0, 0)),
            pl.BlockSpec((None, None, nk, V_HEAD_DIM, tk), lambda bi, hi, i: (bi, hi, 0, 0, 0)),
        ],
        out_specs=pl.BlockSpec((None, tq, V_HEAD_DIM), lambda bi, hi, i: (bi, i, hi)),
        out_shape=jax.ShapeDtypeStruct((b, s, h * V_HEAD_DIM), F32),
        scratch_shapes=[pltpu.VMEM((tk, tq), F32), pltpu.VMEM((tk, tq), F32), pltpu.VMEM((V_HEAD_DIM, tq), F32)],
        compiler_params=_params("parallel", "parallel", "parallel"),
        name="attention",
    )(q, k, vt)


def _out_proj_kernel(x_ref, f_ref, a_ref, gf_ref, ga_ref, w_ref, o_ref):
    hf = _rms(f_ref[...], gf_ref[...]).astype(BF16)
    ha = _rms(a_ref[...], ga_ref[...]).astype(BF16)
    o_ref[...] = x_ref[...] + _dot(hf, w_ref[0:FOURIER_DIM, :]) + _dot(ha, w_ref[FOURIER_DIM:, :])


def _out_proj(x, f, a, g_f, g_a, w_out, *, tm):
    t = x.shape[0]
    return pl.pallas_call(
        _out_proj_kernel,
        grid=(t // tm,),
        in_specs=[
            pl.BlockSpec((tm, D_MODEL), lambda i: (i, 0)),
            pl.BlockSpec((tm, FOURIER_DIM), lambda i: (i, 0)),
            pl.BlockSpec((tm, MLA_DIM), lambda i: (i, 0)),
            _const_spec(g_f.shape),
            _const_spec(g_a.shape),
            _const_spec(w_out.shape),
        ],
        out_specs=pl.BlockSpec((tm, D_MODEL), lambda i: (i, 0)),
        out_shape=jax.ShapeDtypeStruct((t, D_MODEL), F32),
        compiler_params=_params("parallel"),
        name="out_proj",
    )(x, f, a, g_f, g_a, w_out)


def _mem_kv_kernel(m_ref, g_ref, w_ref, o_ref):
    o_ref[...] = _dot(_rms(m_ref[...], g_ref[...]).astype(BF16), w_ref[...]).astype(BF16)


def _mem_kv(mem, g_mem, w_ckv, *, tn):
    b, m, _ = mem.shape
    n = w_ckv.shape[1]
    return pl.pallas_call(
        _mem_kv_kernel,
        grid=(b, n // tn),
        in_specs=[
            pl.BlockSpec((None, m, D_MODEL), lambda bi, j: (bi, 0, 0)),
            _const_spec(g_mem.shape),
            pl.BlockSpec((D_MODEL, tn), lambda bi, j: (0, j)),
        ],
        out_specs=pl.BlockSpec((None, m, tn), lambda bi, j: (bi, 0, j)),
        out_shape=jax.ShapeDtypeStruct((b, m, n), BF16),
        compiler_params=_params("parallel", "parallel"),
        name="mem_kv",
    )(mem, g_mem, w_ckv)


def _cross_kernel(x_ref, kv_ref, gc_ref, wq_ref, wo_ref, gn_ref, o_ref, h_ref):
    x = x_ref[...]
    h = _rms(x, gc_ref[...]).astype(BF16)
    q = _dot(h, wq_ref[...]) * (CROSS_HEAD_DIM ** -0.5)
    heads = []
    for hd in range(N_CROSS_HEADS):
        lo = hd * CROSS_HEAD_DIM
        s = _dot_nt(q[:, lo:lo + CROSS_HEAD_DIM].astype(BF16), kv_ref[:, lo:lo + CROSS_HEAD_DIM])
        p = jnp.exp(s - jnp.max(s, axis=-1, keepdims=True))
        p = p / jnp.sum(p, axis=-1, keepdims=True)
        heads.append(_dot(p.astype(BF16), kv_ref[:, D_MODEL + lo:D_MODEL + lo + CROSS_HEAD_DIM]).astype(BF16))
    y = x + _dot(jnp.concatenate(heads, axis=1), wo_ref[...])
    o_ref[...] = y
    h_ref[...] = _rms(y, gn_ref[...]).astype(BF16)


def _cross(x, kvm, g_cross, w_cq, w_co, g_ffn, *, tm):
    b, s, _ = x.shape
    m = kvm.shape[1]
    spec = pl.BlockSpec((None, tm, D_MODEL), lambda bi, i: (bi, i, 0))
    return pl.pallas_call(
        _cross_kernel,
        grid=(b, s // tm),
        in_specs=[
            spec,
            pl.BlockSpec((None, m, 2 * D_MODEL), lambda bi, i: (bi, 0, 0)),
            _const_spec(g_cross.shape),
            _const_spec(w_cq.shape),
            _const_spec(w_co.shape),
            _const_spec(g_ffn.shape),
        ],
        out_specs=[spec, spec],
        out_shape=[jax.ShapeDtypeStruct((b, s, D_MODEL), F32), jax.ShapeDtypeStruct((b, s, D_MODEL), BF16)],
        compiler_params=_params("parallel", "parallel"),
        name="cross",
    )(x, kvm, g_cross, w_cq, w_co, g_ffn)


def _ffn_up_kernel(h_ref, prev_ref, next_ref, wg_ref, wu_ref, cw_ref, cb_ref, o_ref, hext_ref, *, seq):
    i = pl.program_id(0)
    tm = h_ref.shape[0]

    @pl.when(pl.program_id(1) == 0)
    def _():
        first = (i * tm) % seq == 0
        last = ((i + 1) * tm) % seq == 0
        prev = prev_ref[...]
        nxt = next_ref[...]
        hext_ref[0:HALO, :] = jnp.where(first, jnp.zeros_like(prev), prev)
        hext_ref[HALO:HALO + tm, :] = h_ref[...]
        hext_ref[HALO + tm:, :] = jnp.where(last, jnp.zeros_like(nxt), nxt)

    g = _dot(hext_ref[...], wg_ref[...])
    u = _dot(h_ref[...], wu_ref[...])
    rows = g.shape[0]
    g_prev = pltpu.roll(g, 1, axis=0)[HALO:HALO + tm]
    g_next = pltpu.roll(g, rows - 1, axis=0)[HALO:HALO + tm]
    cw = cw_ref[...]
    c = g_prev * cw[0:1] + g[HALO:HALO + tm] * cw[1:2] + g_next * cw[2:3] + cb_ref[...]
    o_ref[...] = (c / (1.0 + jnp.exp(-c)) * u).astype(BF16)


def _ffn_up(h, w_gate, w_up, conv_w, conv_b, *, seq, tm, tf):
    t = h.shape[0]
    hb = tm // HALO
    n_halo = t // HALO
    return pl.pallas_call(
        functools.partial(_ffn_up_kernel, seq=seq),
        grid=(t // tm, D_FF // tf),
        in_specs=[
            pl.BlockSpec((tm, D_MODEL), lambda i, j: (i, 0)),
            pl.BlockSpec((HALO, D_MODEL), lambda i, j: (jnp.maximum(i * hb - 1, 0), 0)),
            pl.BlockSpec((HALO, D_MODEL), lambda i, j: (jnp.minimum((i + 1) * hb, n_halo - 1), 0)),
            pl.BlockSpec((D_MODEL, tf), lambda i, j: (0, j)),
            pl.BlockSpec((D_MODEL, tf), lambda i, j: (0, j)),
            pl.BlockSpec((3, tf), lambda i, j: (0, j)),
            pl.BlockSpec((1, tf), lambda i, j: (0, j)),
        ],
        out_specs=pl.BlockSpec((tm, tf), lambda i, j: (i, j)),
        out_shape=jax.ShapeDtypeStruct((t, D_FF), BF16),
        scratch_shapes=[pltpu.VMEM((tm + 2 * HALO, D_MODEL), BF16)],
        compiler_params=_params("parallel", "arbitrary"),
        name="ffn_up",
    )(h, h, h, w_gate, w_up, conv_w, conv_b)


def _ffn_down_kernel(a_ref, w_ref, x_ref, g_ref, o_ref, acc_ref):
    k = pl.program_id(1)

    @pl.when(k == 0)
    def _():
        acc_ref[...] = jnp.zeros_like(acc_ref)

    acc_ref[...] += _dot(a_ref[...], w_ref[...])

    @pl.when(k == pl.num_programs(1) - 1)
    def _():
        o_ref[...] = _rms(x_ref[...] + acc_ref[...], g_ref[...])


def _ffn_down(act, w_down, x, g_final, *, tm, tk):
    t = x.shape[0]
    return pl.pallas_call(
        _ffn_down_kernel,
        grid=(t // tm, D_FF // tk),
        in_specs=[
            pl.BlockSpec((tm, tk), lambda i, k: (i, k)),
            pl.BlockSpec((tk, D_MODEL), lambda i, k: (k, 0)),
            pl.BlockSpec((tm, D_MODEL), lambda i, k: (i, 0)),
            _const_spec(g_final.shape),
        ],
        out_specs=pl.BlockSpec((tm, D_MODEL), lambda i, k: (i, 0)),
        out_shape=jax.ShapeDtypeStruct((t, D_MODEL), F32),
        scratch_shapes=[pltpu.VMEM((tm, D_MODEL), F32)],
        compiler_params=_params("parallel", "arbitrary"),
        name="ffn_down",
    )(act, w_down, x, g_final)


def _dft_tables(seq):
    n1 = DFT_N1
    n2 = seq // n1
    c = np.arange(FOURIER_GROUP_DIM)
    ang = 2.0 * np.pi * ((c[:, None] * c[None, :]) % FOURIER_GROUP_DIM) / FOURIER_GROUP_DIM
    dft_c = np.concatenate([np.cos(ang), np.sin(ang)], axis=1)

    j = np.arange(n1)
    a1 = 2.0 * np.pi * ((j[:, None] * j[None, :]) % n1) / n1
    c1, s1 = np.cos(a1), np.sin(a1)
    m1 = np.block([[c1, -s1], [-s1, -c1]])

    k1 = np.arange(n1)[:, None, None]
    k2 = np.arange(n2)[None, :, None]
    m2 = np.arange(n2)[None, None, :]
    a2 = 2.0 * np.pi * ((m2 * (k1 + n1 * k2)) % seq) / seq
    e = np.concatenate([np.cos(a2), np.sin(a2)], axis=2)
    return (jnp.asarray(dft_c, BF16), jnp.asarray(m1, BF16), jnp.asarray(e, BF16))


def _rope_tables(seq):
    inv = ROPE_THETA ** (-jnp.arange(0, QK_ROPE_DIM, 2, dtype=F32) / QK_ROPE_DIM)
    ang = jnp.arange(seq, dtype=F32)[:, None] * inv[None, :]
    cos, sin = jnp.cos(ang), jnp.sin(ang)
    return jnp.concatenate([cos, cos, cos, cos], axis=1), jnp.concatenate([-sin, sin, -sin, sin], axis=1)


def _prep_weights(w_in, w_uq, w_ukv):
    half = QK_ROPE_DIM // 2
    c1 = FOURIER_DIM + Q_LORA_RANK + KV_LORA_RANK
    w_in_ext = jnp.concatenate(
        [w_in, w_in[:, c1 + half:c1 + QK_ROPE_DIM], w_in[:, c1:c1 + half]], axis=1).astype(BF16)
    q3 = w_uq.reshape(Q_LORA_RANK, N_HEADS, QK_DIM)
    rope = q3[:, :, QK_NOPE_DIM:]
    rope_sw = jnp.concatenate([rope[:, :, half:], rope[:, :, :half]], axis=2)
    w_uq_ext = jnp.concatenate(
        [q3[:, :, :QK_NOPE_DIM].reshape(Q_LORA_RANK, -1), rope.reshape(Q_LORA_RANK, -1),
         rope_sw.reshape(Q_LORA_RANK, -1)], axis=1).astype(BF16)
    kv3 = w_ukv.reshape(KV_LORA_RANK, N_HEADS, QK_NOPE_DIM + V_HEAD_DIM)
    w_k = kv3[:, :, :QK_NOPE_DIM].reshape(KV_LORA_RANK, -1).astype(BF16)
    w_vt = kv3[:, :, QK_NOPE_DIM:].reshape(KV_LORA_RANK, -1).T.astype(BF16)
    return w_in_ext, w_uq_ext, w_k, w_vt


def _tile(n, pref):
    return pref if n % pref == 0 else n


def _trunk(x, mem, p, tables):
    b, s, _ = x.shape
    t = b * s
    n1 = DFT_N1
    n2 = s // n1
    dft_c, m1, e, cos2, sin2 = tables

    ab, q, k, vt = _in_proj(x, p["g_mix"], p["w_in_ext"], p["g_q"], p["w_uq_ext"], p["g_kv"], p["w_k"],
                            p["w_vt"], dft_c, cos2, sin2, tm=_tile(s, 256), tk=_tile(s, 512))

    cols = n2 * FOURIER_GROUP_DIM
    gc = _fourier_a(ab.reshape(b, N_FOURIER_GROUPS, 2, n1, cols), m1, tc=_tile(cols, 4096))
    f = _fourier_b(gc.reshape(b, N_FOURIER_GROUPS, 2, n1, n2, FOURIER_GROUP_DIM), e, kb=32,
                   scale=float((s * FOURIER_GROUP_DIM) ** -0.5))

    a = _attention(q, k, vt, tq=_tile(s, 512))

    x1 = _out_proj(x.reshape(t, D_MODEL), f.reshape(t, FOURIER_DIM), a.reshape(t, MLA_DIM),
                   p["g_f"], p["g_a"], p["w_out"], tm=_tile(t, 512))

    kvm = _mem_kv(mem, p["g_mem"], p["w_ckv"], tn=1024)
    x2, hf = _cross(x1.reshape(b, s, D_MODEL), kvm, p["g_cross"], p["w_cq"], p["w_co"], p["g_ffn"],
                    tm=_tile(s, 256))

    act = _ffn_up(hf.reshape(t, D_MODEL), p["w_gate"], p["w_up"], p["conv_w"], p["conv_b"], seq=s,
                  tm=_tile(s, 1024), tf=512)
    y = _ffn_down(act, p["w_down"], x2.reshape(t, D_MODEL), p["g_final"], tm=_tile(t, 512), tk=512)
    return y.reshape(b, s, D_MODEL)


@jax.jit
def kernel(x_prompt, x_sample, mem_prompt, mem_sample, norm_mix_g, w_in, q_norm_g, w_uq, kv_norm_g, w_ukv,
           fourier_out_g, mla_out_g, w_out, norm_cross_g, norm_mem_g, w_cq, w_ckv, w_co, norm_ffn_g, w_gate,
           w_up, conv_w, conv_b, w_down, final_norm_g):
    assert norm_mix_g.shape[0] == 1, "single-layer trunk"
    assert x_prompt.shape[1] % DFT_N1 == 0 and x_sample.shape[1] % DFT_N1 == 0
    w_in_ext, w_uq_ext, w_k, w_vt = _prep_weights(w_in[0], w_uq[0], w_ukv[0])
    row = lambda g: g.reshape(1, -1)
    p = dict(
        g_mix=row(norm_mix_g[0]), w_in_ext=w_in_ext, g_q=row(q_norm_g[0]), w_uq_ext=w_uq_ext,
        g_kv=row(kv_norm_g[0]), w_k=w_k, w_vt=w_vt, g_f=row(fourier_out_g[0]), g_a=row(mla_out_g[0]),
        w_out=w_out[0].astype(BF16), g_cross=row(norm_cross_g[0]), g_mem=row(norm_mem_g[0]),
        w_cq=w_cq[0].astype(BF16), w_ckv=w_ckv[0].astype(BF16), w_co=w_co[0].astype(BF16),
        g_ffn=row(norm_ffn_g[0]), w_gate=w_gate[0].astype(BF16), w_up=w_up[0].astype(BF16),
        conv_w=conv_w[0], conv_b=row(conv_b[0]), w_down=w_down[0].astype(BF16), g_final=row(final_norm_g),
    )
    outs = []
    for x, mem in ((x_prompt, mem_prompt), (x_sample, mem_sample)):
        s = x.shape[1]
        tables = _dft_tables(s) + _rope_tables(s)
        outs.append(_trunk(x, mem, p, tables))
    return tuple(outs)
```

```python
import functools

import numpy as np
import jax
import jax.numpy as jnp
from jax import lax
from jax.experimental import pallas as pl
from jax.experimental.pallas import tpu as pltpu

F32 = jnp.float32
BF16 = jnp.bfloat16

D_MODEL = 2048
N_FOURIER_GROUPS = 4
FOURIER_GROUP_DIM = 256
FOURIER_DIM = N_FOURIER_GROUPS * FOURIER_GROUP_DIM
N_HEADS = 8
QK_NOPE_DIM = 128
QK_ROPE_DIM = 64
QK_DIM = QK_NOPE_DIM + QK_ROPE_DIM
V_HEAD_DIM = 128
Q_LORA_RANK = 512
KV_LORA_RANK = 512
MLA_DIM = N_HEADS * V_HEAD_DIM
ROPE_THETA = 10000.0
N_CROSS_HEADS = 4
CROSS_HEAD_DIM = D_MODEL // N_CROSS_HEADS
D_FF = 5632
EPS = 1e-6
LOG2_E = 1.4426950408889634

V7X_VMEM_LIMIT_BYTES = 56 * 1024 * 1024
DFT_N1 = 128
HALO = 16


def _params(*sem):
    return pltpu.CompilerParams(dimension_semantics=sem, vmem_limit_bytes=V7X_VMEM_LIMIT_BYTES)


def _const_spec(shape):
    zeros = (0,) * len(shape)
    return pl.BlockSpec(shape, lambda *_: zeros, pipeline_mode=pl.Buffered(1))


def _rms(x, g):
    return x * lax.rsqrt(jnp.mean(x * x, axis=-1, keepdims=True) + EPS) * g


def _dot(a, b):
    return jnp.dot(a, b, preferred_element_type=F32)


def _dot_nt(a, b):
    return lax.dot_general(a, b, (((1,), (1,)), ((), ())), preferred_element_type=F32)


def _in_proj_kernel(x_ref, g_ref, win_ref, qg_ref, wuq_ref, kvg_ref, wk_ref, wvt_ref, dft_ref, cos_ref,
                    sin_ref, ab_ref, q_ref, k_ref, vt_ref):
    h = _rms(x_ref[...], g_ref[...]).astype(BF16)
    z = _dot(h, win_ref[...])

    for g in range(N_FOURIER_GROUPS):
        u = z[:, g * FOURIER_GROUP_DIM:(g + 1) * FOURIER_GROUP_DIM].astype(BF16)
        ab = _dot(u, dft_ref[...])
        ab_ref[g, 0] = ab[:, :FOURIER_GROUP_DIM].astype(BF16)
        ab_ref[g, 1] = ab[:, FOURIER_GROUP_DIM:].astype(BF16)

    c0 = FOURIER_DIM
    c_q = z[:, c0:c0 + Q_LORA_RANK]
    c_kv = z[:, c0 + Q_LORA_RANK:c0 + Q_LORA_RANK + KV_LORA_RANK]
    c1 = c0 + Q_LORA_RANK + KV_LORA_RANK
    k_r = z[:, c1:c1 + QK_ROPE_DIM]
    k_r_sw = z[:, c1 + QK_ROPE_DIM:c1 + 2 * QK_ROPE_DIM]

    cos = cos_ref[...]
    sin = sin_ref[...]
    k_rope = (k_r * cos[:, :QK_ROPE_DIM] + k_r_sw * sin[:, :QK_ROPE_DIM]).astype(BF16)

    hq = _rms(c_q, qg_ref[...]).astype(BF16)
    qe = _dot(hq, wuq_ref[...])
    n_rope = N_HEADS * QK_ROPE_DIM
    cos_h = jnp.concatenate([cos] * (n_rope // 128), axis=1)
    sin_h = jnp.concatenate([sin] * (n_rope // 128), axis=1)
    q_nope_w = N_HEADS * QK_NOPE_DIM
    q_rope = qe[:, q_nope_w:q_nope_w + n_rope] * cos_h + qe[:, q_nope_w + n_rope:] * sin_h

    hkv = _rms(c_kv, kvg_ref[...]).astype(BF16)
    k_nope = _dot(hkv, wk_ref[...])
    v_t = _dot_nt(wvt_ref[...], hkv)

    scale = QK_DIM ** -0.5 * LOG2_E
    for hd in range(N_HEADS):
        q_ref[hd, :, 0:QK_NOPE_DIM] = (qe[:, hd * QK_NOPE_DIM:(hd + 1) * QK_NOPE_DIM] * scale).astype(BF16)
        q_ref[hd, :, QK_NOPE_DIM:QK_DIM] = (
            q_rope[:, hd * QK_ROPE_DIM:(hd + 1) * QK_ROPE_DIM] * scale).astype(BF16)
        k_ref[hd, :, 0:QK_NOPE_DIM] = k_nope[:, hd * QK_NOPE_DIM:(hd + 1) * QK_NOPE_DIM].astype(BF16)
        k_ref[hd, :, QK_NOPE_DIM:QK_DIM] = k_rope
        vt_ref[hd] = v_t[hd * V_HEAD_DIM:(hd + 1) * V_HEAD_DIM, :].astype(BF16)


def _in_proj(x, g_mix, w_in_ext, q_g, w_uq_ext, kv_g, w_k, w_vt, dft_c, cos2, sin2, *, tm, tk):
    b, s, _ = x.shape
    grid = (b, s // tm)
    r = tk // tm
    return pl.pallas_call(
        _in_proj_kernel,
        grid=grid,
        in_specs=[
            pl.BlockSpec((None, tm, D_MODEL), lambda bi, i: (bi, i, 0)),
            _const_spec(g_mix.shape),
            _const_spec(w_in_ext.shape),
            _const_spec(q_g.shape),
            _const_spec(w_uq_ext.shape),
            _const_spec(kv_g.shape),
            _const_spec(w_k.shape),
            _const_spec(w_vt.shape),
            _const_spec(dft_c.shape),
            pl.BlockSpec((tm, 128), lambda bi, i: (i, 0)),
            pl.BlockSpec((tm, 128), lambda bi, i: (i, 0)),
        ],
        out_specs=[
            pl.BlockSpec((None, N_FOURIER_GROUPS, 2, tm, FOURIER_GROUP_DIM), lambda bi, i: (bi, 0, 0, i, 0)),
            pl.BlockSpec((None, N_HEADS, tm, QK_DIM), lambda bi, i: (bi, 0, i, 0)),
            pl.BlockSpec((None, N_HEADS, tm, QK_DIM), lambda bi, i: (bi, 0, i, 0)),
            pl.BlockSpec((None, N_HEADS, None, V_HEAD_DIM, tm), lambda bi, i: (bi, 0, i // r, 0, i % r)),
        ],
        out_shape=[
            jax.ShapeDtypeStruct((b, N_FOURIER_GROUPS, 2, s, FOURIER_GROUP_DIM), BF16),
            jax.ShapeDtypeStruct((b, N_HEADS, s, QK_DIM), BF16),
            jax.ShapeDtypeStruct((b, N_HEADS, s, QK_DIM), BF16),
            jax.ShapeDtypeStruct((b, N_HEADS, s // tk, V_HEAD_DIM, tk), BF16),
        ],
        compiler_params=_params("parallel", "parallel"),
        name="in_proj",
    )(x, g_mix, w_in_ext, q_g, w_uq_ext, kv_g, w_k, w_vt, dft_c, cos2, sin2)


def _fourier_a_kernel(m_ref, x_ref, o_ref):
    two, n1, tc = x_ref.shape
    x = x_ref[...].reshape(two * n1, tc)
    o_ref[...] = _dot(m_ref[...], x).astype(BF16).reshape(two, n1, tc)


def _fourier_a(ab, m1, *, tc):
    b, g, two, n1, cols = ab.shape
    spec = pl.BlockSpec((None, None, two, n1, tc), lambda bi, gi, ci: (bi, gi, 0, 0, ci))
    return pl.pallas_call(
        _fourier_a_kernel,
        grid=(b, g, cols // tc),
        in_specs=[_const_spec(m1.shape), spec],
        out_specs=spec,
        out_shape=jax.ShapeDtypeStruct(ab.shape, BF16),
        compiler_params=_params("parallel", "parallel", "parallel"),
        name="fourier_a",
    )(m1, ab)


def _fourier_b_kernel(e_ref, g_ref, o_ref, *, scale):
    kb = e_ref.shape[0]
    for j in range(kb):
        x = jnp.concatenate([g_ref[0, j], g_ref[1, j]], axis=0)
        o_ref[:, j, :] = _dot(e_ref[j], x) * scale


def _fourier_b(gc, e, *, kb, scale):
    b, g, two, n1, n2, c = gc.shape
    return pl.pallas_call(
        functools.partial(_fourier_b_kernel, scale=scale),
        grid=(b, g, n1 // kb),
        in_specs=[
            pl.BlockSpec((kb, n2, 2 * n2), lambda bi, gi, ki: (ki, 0, 0)),
            pl.BlockSpec((None, None, two, kb, n2, c), lambda bi, gi, ki: (bi, gi, 0, ki, 0, 0)),
        ],
        out_specs=pl.BlockSpec((None, n2, kb, c), lambda bi, gi, ki: (bi, 0, ki, gi)),
        out_shape=jax.ShapeDtypeStruct((b, n2, n1, g * c), F32),
        compiler_params=_params("parallel", "parallel", "parallel"),
        name="fourier_b",
    )(e, gc)


def _attn_kernel(q_ref, k_ref, vt_ref, o_ref, sa_ref, sb_ref, acc_ref, *, group_size):
    q = q_ref[...]
    tq = q.shape[0]
    nk, _, tk = vt_ref.shape

    def scores(j, s_ref):
        start = pl.multiple_of(j * tk, tk)
        s = _dot_nt(k_ref[pl.ds(start, tk), :], q)
        s_ref[...] = s
        return jnp.max(s, axis=0, keepdims=True)

    def update(j, s_ref, cmax, m, l):
        m_new = jnp.maximum(m, cmax)
        p = jnp.exp2(s_ref[...] - m_new)
        alpha = jnp.exp2(m - m_new)
        l = alpha * l + jnp.sum(p, axis=0, keepdims=True)
        acc_ref[...] = alpha * acc_ref[...] + _dot(vt_ref[j], p.astype(BF16))
        return m_new, l

    bufs = (sa_ref, sb_ref)

    def group(j, cmax, m, l, more):
        for t in range(group_size):
            nxt = scores(j + t + 1, bufs[(t + 1) % 2]) if (more or t + 1 < group_size) else None
            m, l = update(j + t, bufs[t % 2], cmax, m, l)
            cmax = nxt
        return cmax, m, l

    acc_ref[...] = jnp.zeros_like(acc_ref)
    carry = (scores(0, sa_ref), jnp.full((1, tq), -jnp.inf, F32), jnp.zeros((1, tq), F32))
    carry = lax.fori_loop(0, nk // group_size - 1, lambda i, c: group(group_size * i, *c, True), carry)
    _, _, l = group(nk - group_size, *carry, False)
    o_ref[...] = (acc_ref[...] / l).T


def _attention(q, k, vt, *, tq, group_size):
    b, h, s, _ = q.shape
    nk, _, tk = vt.shape[2:]
    assert group_size % 2 == 0 and nk % group_size == 0
    return pl.pallas_call(
        functools.partial(_attn_kernel, group_size=group_size),
        grid=(b, h, s // tq),
        in_specs=[
            pl.BlockSpec((None, None, tq, QK_DIM), lambda bi, hi, i: (bi, hi, i, 0)),
            pl.BlockSpec((None, None, s, QK_DIM), lambda bi, hi, i: (bi, hi, 0, 0)),
            pl.BlockSpec((None, None, nk, V_HEAD_DIM, tk), lambda bi, hi, i: (bi, hi, 0, 0, 0)),
        ],
        out_specs=pl.BlockSpec((None, tq, V_HEAD_DIM), lambda bi, hi, i: (bi, i, hi)),
        out_shape=jax.ShapeDtypeStruct((b, s, h * V_HEAD_DIM), F32),
        scratch_shapes=[pltpu.VMEM((tk, tq), F32), pltpu.VMEM((tk, tq), F32), pltpu.VMEM((V_HEAD_DIM, tq), F32)],
        compiler_params=_params("parallel", "parallel", "parallel"),
        name="attention",
    )(q, k, vt)


def _out_proj_kernel(x_ref, f_ref, a_ref, gf_ref, ga_ref, w_ref, o_ref):
    hf = _rms(f_ref[...], gf_ref[...]).astype(BF16)
    ha = _rms(a_ref[...], ga_ref[...]).astype(BF16)
    o_ref[...] = x_ref[...] + _dot(hf, w_ref[0:FOURIER_DIM, :]) + _dot(ha, w_ref[FOURIER_DIM:, :])


def _out_proj(x, f, a, g_f, g_a, w_out, *, tm):
    t = x.shape[0]
    return pl.pallas_call(
        _out_proj_kernel,
        grid=(t // tm,),
        in_specs=[
            pl.BlockSpec((tm, D_MODEL), lambda i: (i, 0)),
            pl.BlockSpec((tm, FOURIER_DIM), lambda i: (i, 0)),
            pl.BlockSpec((tm, MLA_DIM), lambda i: (i, 0)),
            _const_spec(g_f.shape),
            _const_spec(g_a.shape),
            _const_spec(w_out.shape),
        ],
        out_specs=pl.BlockSpec((tm, D_MODEL), lambda i: (i, 0)),
        out_shape=jax.ShapeDtypeStruct((t, D_MODEL), F32),
        compiler_params=_params("parallel"),
        name="out_proj",
    )(x, f, a, g_f, g_a, w_out)


def _mem_kv_kernel(m_ref, g_ref, w_ref, o_ref):
    o_ref[...] = _dot(_rms(m_ref[...], g_ref[...]).astype(BF16), w_ref[...]).astype(BF16)


def _mem_kv(mem, g_mem, w_ckv, *, tn):
    b, m, _ = mem.shape
    n = w_ckv.shape[1]
    return pl.pallas_call(
        _mem_kv_kernel,
        grid=(b, n // tn),
        in_specs=[
            pl.BlockSpec((None, m, D_MODEL), lambda bi, j: (bi, 0, 0)),
            _const_spec(g_mem.shape),
            pl.BlockSpec((D_MODEL, tn), lambda bi, j: (0, j)),
        ],
        out_specs=pl.BlockSpec((None, m, tn), lambda bi, j: (bi, 0, j)),
        out_shape=jax.ShapeDtypeStruct((b, m, n), BF16),
        compiler_params=_params("parallel", "parallel"),
        name="mem_kv",
    )(mem, g_mem, w_ckv)


def _cross_kernel(x_ref, kv_ref, gc_ref, wq_ref, wo_ref, gn_ref, o_ref, h_ref):
    x = x_ref[...]
    h = _rms(x, gc_ref[...]).astype(BF16)
    q = _dot(h, wq_ref[...]) * (CROSS_HEAD_DIM ** -0.5)
    heads = []
    for hd in range(N_CROSS_HEADS):
        lo = hd * CROSS_HEAD_DIM
        s = _dot_nt(q[:, lo:lo + CROSS_HEAD_DIM].astype(BF16), kv_ref[:, lo:lo + CROSS_HEAD_DIM])
        p = jnp.exp(s - jnp.max(s, axis=-1, keepdims=True))
        p = p / jnp.sum(p, axis=-1, keepdims=True)
        heads.append(_dot(p.astype(BF16), kv_ref[:, D_MODEL + lo:D_MODEL + lo + CROSS_HEAD_DIM]).astype(BF16))
    y = x + _dot(jnp.concatenate(heads, axis=1), wo_ref[...])
    o_ref[...] = y
    h_ref[...] = _rms(y, gn_ref[...]).astype(BF16)


def _cross(x, kvm, g_cross, w_cq, w_co, g_ffn, *, tm):
    b, s, _ = x.shape
    m = kvm.shape[1]
    spec = pl.BlockSpec((None, tm, D_MODEL), lambda bi, i: (bi, i, 0))
    return pl.pallas_call(
        _cross_kernel,
        grid=(b, s // tm),
        in_specs=[
            spec,
            pl.BlockSpec((None, m, 2 * D_MODEL), lambda bi, i: (bi, 0, 0)),
            _const_spec(g_cross.shape),
            _const_spec(w_cq.shape),
            _const_spec(w_co.shape),
            _const_spec(g_ffn.shape),
        ],
        out_specs=[spec, spec],
        out_shape=[jax.ShapeDtypeStruct((b, s, D_MODEL), F32), jax.ShapeDtypeStruct((b, s, D_MODEL), BF16)],
        compiler_params=_params("parallel", "parallel"),
        name="cross",
    )(x, kvm, g_cross, w_cq, w_co, g_ffn)


def _ffn_up_kernel(h_ref, prev_ref, next_ref, wg_ref, wu_ref, cw_ref, cb_ref, o_ref, hext_ref, *, seq):
    i = pl.program_id(0)
    tm = h_ref.shape[0]

    @pl.when(pl.program_id(1) == 0)
    def _():
        first = (i * tm) % seq == 0
        last = ((i + 1) * tm) % seq == 0
        prev = prev_ref[...]
        nxt = next_ref[...]
        hext_ref[0:HALO, :] = jnp.where(first, jnp.zeros_like(prev), prev)
        hext_ref[HALO:HALO + tm, :] = h_ref[...]
        hext_ref[HALO + tm:, :] = jnp.where(last, jnp.zeros_like(nxt), nxt)

    g = _dot(hext_ref[...], wg_ref[...])
    u = _dot(h_ref[...], wu_ref[...])
    rows = g.shape[0]
    g_prev = pltpu.roll(g, 1, axis=0)[HALO:HALO + tm]
    g_next = pltpu.roll(g, rows - 1, axis=0)[HALO:HALO + tm]
    cw = cw_ref[...]
    c = g_prev * cw[0:1] + g[HALO:HALO + tm] * cw[1:2] + g_next * cw[2:3] + cb_ref[...]
    o_ref[...] = (c / (1.0 + jnp.exp(-c)) * u).astype(BF16)


def _ffn_up(h, w_gate, w_up, conv_w, conv_b, *, seq, tm, tf):
    t = h.shape[0]
    hb = tm // HALO
    n_halo = t // HALO
    return pl.pallas_call(
        functools.partial(_ffn_up_kernel, seq=seq),
        grid=(t // tm, D_FF // tf),
        in_specs=[
            pl.BlockSpec((tm, D_MODEL), lambda i, j: (i, 0)),
            pl.BlockSpec((HALO, D_MODEL), lambda i, j: (jnp.maximum(i * hb - 1, 0), 0)),
            pl.BlockSpec((HALO, D_MODEL), lambda i, j: (jnp.minimum((i + 1) * hb, n_halo - 1), 0)),
            pl.BlockSpec((D_MODEL, tf), lambda i, j: (0, j)),
            pl.BlockSpec((D_MODEL, tf), lambda i, j: (0, j)),
            pl.BlockSpec((3, tf), lambda i, j: (0, j)),
            pl.BlockSpec((1, tf), lambda i, j: (0, j)),
        ],
        out_specs=pl.BlockSpec((tm, tf), lambda i, j: (i, j)),
        out_shape=jax.ShapeDtypeStruct((t, D_FF), BF16),
        scratch_shapes=[pltpu.VMEM((tm + 2 * HALO, D_MODEL), BF16)],
        compiler_params=_params("parallel", "arbitrary"),
        name="ffn_up",
    )(h, h, h, w_gate, w_up, conv_w, conv_b)


def _ffn_down_kernel(a_ref, w_ref, x_ref, g_ref, o_ref):
    o_ref[...] = _rms(x_ref[...] + _dot(a_ref[...], w_ref[...]), g_ref[...])


def _ffn_down(act, w_down, x, g_final, *, tm):
    t = x.shape[0]
    return pl.pallas_call(
        _ffn_down_kernel,
        grid=(t // tm,),
        in_specs=[
            pl.BlockSpec((tm, D_FF), lambda i: (i, 0)),
            _const_spec(w_down.shape),
            pl.BlockSpec((tm, D_MODEL), lambda i: (i, 0)),
            _const_spec(g_final.shape),
        ],
        out_specs=pl.BlockSpec((tm, D_MODEL), lambda i: (i, 0)),
        out_shape=jax.ShapeDtypeStruct((t, D_MODEL), F32),
        compiler_params=_params("parallel"),
        name="ffn_down",
    )(act, w_down, x, g_final)


def _dft_tables(seq):
    n1 = DFT_N1
    n2 = seq // n1
    c = np.arange(FOURIER_GROUP_DIM)
    ang = 2.0 * np.pi * ((c[:, None] * c[None, :]) % FOURIER_GROUP_DIM) / FOURIER_GROUP_DIM
    dft_c = np.concatenate([np.cos(ang), np.sin(ang)], axis=1)

    j = np.arange(n1)
    a1 = 2.0 * np.pi * ((j[:, None] * j[None, :]) % n1) / n1
    c1, s1 = np.cos(a1), np.sin(a1)
    m1 = np.block([[c1, -s1], [-s1, -c1]])

    k1 = np.arange(n1)[:, None, None]
    k2 = np.arange(n2)[None, :, None]
    m2 = np.arange(n2)[None, None, :]
    a2 = 2.0 * np.pi * ((m2 * (k1 + n1 * k2)) % seq) / seq
    e = np.concatenate([np.cos(a2), np.sin(a2)], axis=2)
    return (jnp.asarray(dft_c, BF16), jnp.asarray(m1, BF16), jnp.asarray(e, BF16))


def _rope_tables(seq):
    inv = ROPE_THETA ** (-jnp.arange(0, QK_ROPE_DIM, 2, dtype=F32) / QK_ROPE_DIM)
    ang = jnp.arange(seq, dtype=F32)[:, None] * inv[None, :]
    cos, sin = jnp.cos(ang), jnp.sin(ang)
    return jnp.concatenate([cos, cos, cos, cos], axis=1), jnp.concatenate([-sin, sin, -sin, sin], axis=1)


def _prep_weights(w_in, w_uq, w_ukv):
    half = QK_ROPE_DIM // 2
    c1 = FOURIER_DIM + Q_LORA_RANK + KV_LORA_RANK
    w_in_ext = jnp.concatenate(
        [w_in, w_in[:, c1 + half:c1 + QK_ROPE_DIM], w_in[:, c1:c1 + half]], axis=1).astype(BF16)
    q3 = w_uq.reshape(Q_LORA_RANK, N_HEADS, QK_DIM)
    rope = q3[:, :, QK_NOPE_DIM:]
    rope_sw = jnp.concatenate([rope[:, :, half:], rope[:, :, :half]], axis=2)
    w_uq_ext = jnp.concatenate(
        [q3[:, :, :QK_NOPE_DIM].reshape(Q_LORA_RANK, -1), rope.reshape(Q_LORA_RANK, -1),
         rope_sw.reshape(Q_LORA_RANK, -1)], axis=1).astype(BF16)
    kv3 = w_ukv.reshape(KV_LORA_RANK, N_HEADS, QK_NOPE_DIM + V_HEAD_DIM)
    w_k = kv3[:, :, :QK_NOPE_DIM].reshape(KV_LORA_RANK, -1).astype(BF16)
    w_vt = kv3[:, :, QK_NOPE_DIM:].reshape(KV_LORA_RANK, -1).T.astype(BF16)
    return w_in_ext, w_uq_ext, w_k, w_vt


def _tile(n, pref):
    return pref if n % pref == 0 else n


def _trunk(x, mem, p, tables):
    b, s, _ = x.shape
    t = b * s
    n1 = DFT_N1
    n2 = s // n1
    dft_c, m1, e, cos2, sin2 = tables

    ab, q, k, vt = _in_proj(x, p["g_mix"], p["w_in_ext"], p["g_q"], p["w_uq_ext"], p["g_kv"], p["w_k"],
                            p["w_vt"], dft_c, cos2, sin2, tm=_tile(s, 256), tk=_tile(s, 512))

    cols = n2 * FOURIER_GROUP_DIM
    gc = _fourier_a(ab.reshape(b, N_FOURIER_GROUPS, 2, n1, cols), m1, tc=_tile(cols, 4096))
    f = _fourier_b(gc.reshape(b, N_FOURIER_GROUPS, 2, n1, n2, FOURIER_GROUP_DIM), e, kb=32,
                   scale=float((s * FOURIER_GROUP_DIM) ** -0.5))

    nk = s // _tile(s, 512)
    a = _attention(q, k, vt, tq=_tile(s, 1024), group_size=4 if nk % 4 == 0 else 2)

    x1 = _out_proj(x.reshape(t, D_MODEL), f.reshape(t, FOURIER_DIM), a.reshape(t, MLA_DIM),
                   p["g_f"], p["g_a"], p["w_out"], tm=_tile(t, 512))

    kvm = _mem_kv(mem, p["g_mem"], p["w_ckv"], tn=1024)
    x2, hf = _cross(x1.reshape(b, s, D_MODEL), kvm, p["g_cross"], p["w_cq"], p["w_co"], p["g_ffn"],
                    tm=_tile(s, 256))

    act = _ffn_up(hf.reshape(t, D_MODEL), p["w_gate"], p["w_up"], p["conv_w"], p["conv_b"], seq=s,
                  tm=_tile(s, 1024), tf=512)
    y = _ffn_down(act, p["w_down"], x2.reshape(t, D_MODEL), p["g_final"], tm=_tile(t, 256))
    return y.reshape(b, s, D_MODEL)


@jax.jit
def kernel(x_prompt, x_sample, mem_prompt, mem_sample, norm_mix_g, w_in, q_norm_g, w_uq, kv_norm_g, w_ukv,
           fourier_out_g, mla_out_g, w_out, norm_cross_g, norm_mem_g, w_cq, w_ckv, w_co, norm_ffn_g, w_gate,
           w_up, conv_w, conv_b, w_down, final_norm_g):
    assert norm_mix_g.shape[0] == 1, "single-layer trunk"
    assert x_prompt.shape[1] % DFT_N1 == 0 and x_sample.shape[1] % DFT_N1 == 0
    w_in_ext, w_uq_ext, w_k, w_vt = _prep_weights(w_in[0], w_uq[0], w_ukv[0])
    row = lambda g: g.reshape(1, -1)
    p = dict(
        g_mix=row(norm_mix_g[0]), w_in_ext=w_in_ext, g_q=row(q_norm_g[0]), w_uq_ext=w_uq_ext,
        g_kv=row(kv_norm_g[0]), w_k=w_k, w_vt=w_vt, g_f=row(fourier_out_g[0]), g_a=row(mla_out_g[0]),
        w_out=w_out[0].astype(BF16), g_cross=row(norm_cross_g[0]), g_mem=row(norm_mem_g[0]),
        w_cq=w_cq[0].astype(BF16), w_ckv=w_ckv[0].astype(BF16), w_co=w_co[0].astype(BF16),
        g_ffn=row(norm_ffn_g[0]), w_gate=w_gate[0].astype(BF16), w_up=w_up[0].astype(BF16),
        conv_w=conv_w[0], conv_b=row(conv_b[0]), w_down=w_down[0].astype(BF16), g_final=row(final_norm_g),
    )
    outs = []
    for x, mem in ((x_prompt, mem_prompt), (x_sample, mem_sample)):
        s = x.shape[1]
        tables = _dft_tables(s) + _rope_tables(s)
        outs.append(_trunk(x, mem, p, tables))
    return tuple(outs)
```

```python
import functools

import numpy as np
import jax
import jax.numpy as jnp
from jax import lax
from jax.experimental import pallas as pl
from jax.experimental.pallas import tpu as pltpu

F32 = jnp.float32
BF16 = jnp.bfloat16

D_MODEL = 2048
N_FOURIER_GROUPS = 4
FOURIER_GROUP_DIM = 256
FOURIER_DIM = N_FOURIER_GROUPS * FOURIER_GROUP_DIM
N_HEADS = 8
QK_NOPE_DIM = 128
QK_ROPE_DIM = 64
QK_DIM = QK_NOPE_DIM + QK_ROPE_DIM
V_HEAD_DIM = 128
Q_LORA_RANK = 512
KV_LORA_RANK = 512
MLA_DIM = N_HEADS * V_HEAD_DIM
ROPE_THETA = 10000.0
N_CROSS_HEADS = 4
CROSS_HEAD_DIM = D_MODEL // N_CROSS_HEADS
D_FF = 5632
EPS = 1e-6
LOG2_E = 1.4426950408889634

V7X_VMEM_LIMIT_BYTES = 56 * 1024 * 1024
DFT_N1 = 128
HALO = 16


def _params(*sem):
    return pltpu.CompilerParams(dimension_semantics=sem, vmem_limit_bytes=V7X_VMEM_LIMIT_BYTES)


def _const_spec(shape):
    zeros = (0,) * len(shape)
    return pl.BlockSpec(shape, lambda *_: zeros, pipeline_mode=pl.Buffered(1))


def _rms(x, g):
    return x * lax.rsqrt(jnp.mean(x * x, axis=-1, keepdims=True) + EPS) * g


def _dot(a, b):
    return jnp.dot(a, b, preferred_element_type=F32)


def _dot_nt(a, b):
    return lax.dot_general(a, b, (((1,), (1,)), ((), ())), preferred_element_type=F32)


def _in_proj_kernel(x_ref, g_ref, win_ref, qg_ref, wuqt_ref, kvg_ref, wk_ref, wvt_ref, dft_ref, cos_ref,
                    sin_ref, cost_ref, sint_ref, ab_ref, qt_ref, k_ref, vt_ref):
    h = _rms(x_ref[...], g_ref[...]).astype(BF16)
    z = _dot(h, win_ref[...])

    for g in range(N_FOURIER_GROUPS):
        u = z[:, g * FOURIER_GROUP_DIM:(g + 1) * FOURIER_GROUP_DIM].astype(BF16)
        ab = _dot(u, dft_ref[...])
        ab_ref[g, 0] = ab[:, :FOURIER_GROUP_DIM].astype(BF16)
        ab_ref[g, 1] = ab[:, FOURIER_GROUP_DIM:].astype(BF16)

    c0 = FOURIER_DIM
    c_q = z[:, c0:c0 + Q_LORA_RANK]
    c_kv = z[:, c0 + Q_LORA_RANK:c0 + Q_LORA_RANK + KV_LORA_RANK]
    c1 = c0 + Q_LORA_RANK + KV_LORA_RANK
    k_r = z[:, c1:c1 + QK_ROPE_DIM]
    k_r_sw = z[:, c1 + QK_ROPE_DIM:c1 + 2 * QK_ROPE_DIM]

    cos = cos_ref[...]
    sin = sin_ref[...]
    k_rope = (k_r * cos[:, :QK_ROPE_DIM] + k_r_sw * sin[:, :QK_ROPE_DIM]).astype(BF16)

    hq = _rms(c_q, qg_ref[...]).astype(BF16)
    qe_t = _dot_nt(wuqt_ref[...], hq)
    n_rope = N_HEADS * QK_ROPE_DIM
    q_nope_w = N_HEADS * QK_NOPE_DIM
    cos_t = jnp.concatenate([cost_ref[...]] * N_HEADS, axis=0)
    sin_t = jnp.concatenate([sint_ref[...]] * N_HEADS, axis=0)
    q_rope_t = qe_t[q_nope_w:q_nope_w + n_rope] * cos_t + qe_t[q_nope_w + n_rope:] * sin_t

    hkv = _rms(c_kv, kvg_ref[...]).astype(BF16)
    k_nope = _dot(hkv, wk_ref[...])
    v_t = _dot_nt(wvt_ref[...], hkv)

    scale = QK_DIM ** -0.5 * LOG2_E
    for hd in range(N_HEADS):
        qt_ref[hd, 0:QK_NOPE_DIM, :] = (qe_t[hd * QK_NOPE_DIM:(hd + 1) * QK_NOPE_DIM] * scale).astype(BF16)
        qt_ref[hd, QK_NOPE_DIM:QK_DIM, :] = (
            q_rope_t[hd * QK_ROPE_DIM:(hd + 1) * QK_ROPE_DIM] * scale).astype(BF16)
        k_ref[hd, :, 0:QK_NOPE_DIM] = k_nope[:, hd * QK_NOPE_DIM:(hd + 1) * QK_NOPE_DIM].astype(BF16)
        k_ref[hd, :, QK_NOPE_DIM:QK_DIM] = k_rope
        vt_ref[hd] = v_t[hd * V_HEAD_DIM:(hd + 1) * V_HEAD_DIM, :].astype(BF16)


def _in_proj(x, g_mix, w_in_ext, q_g, w_uqt_ext, kv_g, w_k, w_vt, dft_c, cos2, sin2, cos_t, sin_t, *, tm, tk):
    b, s, _ = x.shape
    grid = (b, s // tm)
    r = tk // tm
    return pl.pallas_call(
        _in_proj_kernel,
        grid=grid,
        in_specs=[
            pl.BlockSpec((None, tm, D_MODEL), lambda bi, i: (bi, i, 0)),
            _const_spec(g_mix.shape),
            _const_spec(w_in_ext.shape),
            _const_spec(q_g.shape),
            _const_spec(w_uqt_ext.shape),
            _const_spec(kv_g.shape),
            _const_spec(w_k.shape),
            _const_spec(w_vt.shape),
            _const_spec(dft_c.shape),
            pl.BlockSpec((tm, 128), lambda bi, i: (i, 0)),
            pl.BlockSpec((tm, 128), lambda bi, i: (i, 0)),
            pl.BlockSpec((QK_ROPE_DIM, tm), lambda bi, i: (0, i)),
            pl.BlockSpec((QK_ROPE_DIM, tm), lambda bi, i: (0, i)),
        ],
        out_specs=[
            pl.BlockSpec((None, N_FOURIER_GROUPS, 2, tm, FOURIER_GROUP_DIM), lambda bi, i: (bi, 0, 0, i, 0)),
            pl.BlockSpec((None, N_HEADS, QK_DIM, tm), lambda bi, i: (bi, 0, 0, i)),
            pl.BlockSpec((None, N_HEADS, tm, QK_DIM), lambda bi, i: (bi, 0, i, 0)),
            pl.BlockSpec((None, N_HEADS, None, V_HEAD_DIM, tm), lambda bi, i: (bi, 0, i // r, 0, i % r)),
        ],
        out_shape=[
            jax.ShapeDtypeStruct((b, N_FOURIER_GROUPS, 2, s, FOURIER_GROUP_DIM), BF16),
            jax.ShapeDtypeStruct((b, N_HEADS, QK_DIM, s), BF16),
            jax.ShapeDtypeStruct((b, N_HEADS, s, QK_DIM), BF16),
            jax.ShapeDtypeStruct((b, N_HEADS, s // tk, V_HEAD_DIM, tk), BF16),
        ],
        compiler_params=_params("parallel", "parallel"),
        name="in_proj",
    )(x, g_mix, w_in_ext, q_g, w_uqt_ext, kv_g, w_k, w_vt, dft_c, cos2, sin2, cos_t, sin_t)


def _fourier_a_kernel(m_ref, x_ref, o_ref):
    two, n1, tc = x_ref.shape
    x = x_ref[...].reshape(two * n1, tc)
    o_ref[...] = _dot(m_ref[...], x).astype(BF16).reshape(two, n1, tc)


def _fourier_a(ab, m1, *, tc):
    b, g, two, n1, cols = ab.shape
    spec = pl.BlockSpec((None, None, two, n1, tc), lambda bi, gi, ci: (bi, gi, 0, 0, ci))
    return pl.pallas_call(
        _fourier_a_kernel,
        grid=(b, g, cols // tc),
        in_specs=[_const_spec(m1.shape), spec],
        out_specs=spec,
        out_shape=jax.ShapeDtypeStruct(ab.shape, BF16),
        compiler_params=_params("parallel", "parallel", "parallel"),
        name="fourier_a",
    )(m1, ab)


def _fourier_b_kernel(e_ref, g_ref, o_ref, *, scale):
    kb = e_ref.shape[0]
    for j in range(kb):
        x = jnp.concatenate([g_ref[0, j], g_ref[1, j]], axis=0)
        o_ref[:, j, :] = _dot(e_ref[j], x) * scale


def _fourier_b(gc, e, *, kb, scale):
    b, g, two, n1, n2, c = gc.shape
    return pl.pallas_call(
        functools.partial(_fourier_b_kernel, scale=scale),
        grid=(b, g, n1 // kb),
        in_specs=[
            pl.BlockSpec((kb, n2, 2 * n2), lambda bi, gi, ki: (ki, 0, 0)),
            pl.BlockSpec((None, None, two, kb, n2, c), lambda bi, gi, ki: (bi, gi, 0, ki, 0, 0)),
        ],
        out_specs=pl.BlockSpec((None, n2, kb, c), lambda bi, gi, ki: (bi, 0, ki, gi)),
        out_shape=jax.ShapeDtypeStruct((b, n2, n1, g * c), F32),
        compiler_params=_params("parallel", "parallel", "parallel"),
        name="fourier_b",
    )(e, gc)


def _attn_kernel(qt_ref, k_ref, vt_ref, o_ref, sa_ref, sb_ref, acc_ref, *, group_size):
    qt = qt_ref[...]
    tq = qt.shape[1]
    nk, _, tk = vt_ref.shape

    def scores(j, s_ref):
        start = pl.multiple_of(j * tk, tk)
        s = _dot(k_ref[pl.ds(start, tk), :], qt)
        s_ref[...] = s
        return jnp.max(s, axis=0, keepdims=True)

    def update(j, s_ref, cmax, m, l):
        m_new = jnp.maximum(m, cmax)
        p = jnp.exp2(s_ref[...] - m_new)
        alpha = jnp.exp2(m - m_new)
        l = alpha * l + jnp.sum(p, axis=0, keepdims=True)
        acc_ref[...] = alpha * acc_ref[...] + _dot(vt_ref[j], p.astype(BF16))
        return m_new, l

    bufs = (sa_ref, sb_ref)

    def group(j, cmax, m, l, more):
        for t in range(group_size):
            nxt = scores(j + t + 1, bufs[(t + 1) % 2]) if (more or t + 1 < group_size) else None
            m, l = update(j + t, bufs[t % 2], cmax, m, l)
            cmax = nxt
        return cmax, m, l

    acc_ref[...] = jnp.zeros_like(acc_ref)
    carry = (scores(0, sa_ref), jnp.full((1, tq), -jnp.inf, F32), jnp.zeros((1, tq), F32))
    carry = lax.fori_loop(0, nk // group_size - 1, lambda i, c: group(group_size * i, *c, True), carry)
    _, _, l = group(nk - group_size, *carry, False)
    o_ref[...] = (acc_ref[...] / l).T


def _attention(qt, k, vt, *, tq, group_size):
    b, h, s, _ = k.shape
    nk, _, tk = vt.shape[2:]
    assert group_size % 2 == 0 and nk % group_size == 0
    return pl.pallas_call(
        functools.partial(_attn_kernel, group_size=group_size),
        grid=(b, h, s // tq),
        in_specs=[
            pl.BlockSpec((None, None, QK_DIM, tq), lambda bi, hi, i: (bi, hi, 0, i)),
            pl.BlockSpec((None, None, s, QK_DIM), lambda bi, hi, i: (bi, hi, 0, 0)),
            pl.BlockSpec((None, None, nk, V_HEAD_DIM, tk), lambda bi, hi, i: (bi, hi, 0, 0, 0)),
        ],
        out_specs=pl.BlockSpec((None, tq, V_HEAD_DIM), lambda bi, hi, i: (bi, i, hi)),
        out_shape=jax.ShapeDtypeStruct((b, s, h * V_HEAD_DIM), F32),
        scratch_shapes=[pltpu.VMEM((tk, tq), F32), pltpu.VMEM((tk, tq), F32), pltpu.VMEM((V_HEAD_DIM, tq), F32)],
        compiler_params=_params("parallel", "parallel", "parallel"),
        name="attention",
    )(qt, k, vt)


def _out_proj_kernel(x_ref, f_ref, a_ref, gf_ref, ga_ref, w_ref, o_ref):
    hf = _rms(f_ref[...], gf_ref[...]).astype(BF16)
    ha = _rms(a_ref[...], ga_ref[...]).astype(BF16)
    o_ref[...] = x_ref[...] + _dot(hf, w_ref[0:FOURIER_DIM, :]) + _dot(ha, w_ref[FOURIER_DIM:, :])


def _out_proj(x, f, a, g_f, g_a, w_out, *, tm):
    t = x.shape[0]
    return pl.pallas_call(
        _out_proj_kernel,
        grid=(t // tm,),
        in_specs=[
            pl.BlockSpec((tm, D_MODEL), lambda i: (i, 0)),
            pl.BlockSpec((tm, FOURIER_DIM), lambda i: (i, 0)),
            pl.BlockSpec((tm, MLA_DIM), lambda i: (i, 0)),
            _const_spec(g_f.shape),
            _const_spec(g_a.shape),
            _const_spec(w_out.shape),
        ],
        out_specs=pl.BlockSpec((tm, D_MODEL), lambda i: (i, 0)),
        out_shape=jax.ShapeDtypeStruct((t, D_MODEL), F32),
        compiler_params=_params("parallel"),
        name="out_proj",
    )(x, f, a, g_f, g_a, w_out)


def _mem_kv_kernel(m_ref, g_ref, w_ref, o_ref):
    o_ref[...] = _dot(_rms(m_ref[...], g_ref[...]).astype(BF16), w_ref[...]).astype(BF16)


def _mem_kv(mem, g_mem, w_ckv, *, tn):
    b, m, _ = mem.shape
    n = w_ckv.shape[1]
    return pl.pallas_call(
        _mem_kv_kernel,
        grid=(b, n // tn),
        in_specs=[
            pl.BlockSpec((None, m, D_MODEL), lambda bi, j: (bi, 0, 0)),
            _const_spec(g_mem.shape),
            pl.BlockSpec((D_MODEL, tn), lambda bi, j: (0, j)),
        ],
        out_specs=pl.BlockSpec((None, m, tn), lambda bi, j: (bi, 0, j)),
        out_shape=jax.ShapeDtypeStruct((b, m, n), BF16),
        compiler_params=_params("parallel", "parallel"),
        name="mem_kv",
    )(mem, g_mem, w_ckv)


def _cross_kernel(x_ref, kv_ref, gc_ref, wq_ref, wo_ref, gn_ref, o_ref, h_ref):
    x = x_ref[...]
    h = _rms(x, gc_ref[...]).astype(BF16)
    q = _dot(h, wq_ref[...]) * (CROSS_HEAD_DIM ** -0.5)
    heads = []
    for hd in range(N_CROSS_HEADS):
        lo = hd * CROSS_HEAD_DIM
        s = _dot_nt(q[:, lo:lo + CROSS_HEAD_DIM].astype(BF16), kv_ref[:, lo:lo + CROSS_HEAD_DIM])
        p = jnp.exp(s - jnp.max(s, axis=-1, keepdims=True))
        p = p / jnp.sum(p, axis=-1, keepdims=True)
        heads.append(_dot(p.astype(BF16), kv_ref[:, D_MODEL + lo:D_MODEL + lo + CROSS_HEAD_DIM]).astype(BF16))
    y = x + _dot(jnp.concatenate(heads, axis=1), wo_ref[...])
    o_ref[...] = y
    h_ref[...] = _rms(y, gn_ref[...]).astype(BF16)


def _cross(x, kvm, g_cross, w_cq, w_co, g_ffn, *, tm):
    b, s, _ = x.shape
    m = kvm.shape[1]
    spec = pl.BlockSpec((None, tm, D_MODEL), lambda bi, i: (bi, i, 0))
    return pl.pallas_call(
        _cross_kernel,
        grid=(b, s // tm),
        in_specs=[
            spec,
            pl.BlockSpec((None, m, 2 * D_MODEL), lambda bi, i: (bi, 0, 0)),
            _const_spec(g_cross.shape),
            _const_spec(w_cq.shape),
            _const_spec(w_co.shape),
            _const_spec(g_ffn.shape),
        ],
        out_specs=[spec, spec],
        out_shape=[jax.ShapeDtypeStruct((b, s, D_MODEL), F32), jax.ShapeDtypeStruct((b, s, D_MODEL), BF16)],
        compiler_params=_params("parallel", "parallel"),
        name="cross",
    )(x, kvm, g_cross, w_cq, w_co, g_ffn)


def _ffn_up_kernel(h_ref, prev_ref, next_ref, wg_ref, wu_ref, cw_ref, cb_ref, o_ref, hext_ref, *, seq):
    i = pl.program_id(0)
    tm = h_ref.shape[0]

    @pl.when(pl.program_id(1) == 0)
    def _():
        first = (i * tm) % seq == 0
        last = ((i + 1) * tm) % seq == 0
        prev = prev_ref[...]
        nxt = next_ref[...]
        hext_ref[0:HALO, :] = jnp.where(first, jnp.zeros_like(prev), prev)
        hext_ref[HALO:HALO + tm, :] = h_ref[...]
        hext_ref[HALO + tm:, :] = jnp.where(last, jnp.zeros_like(nxt), nxt)

    g = _dot(hext_ref[...], wg_ref[...])
    u = _dot(h_ref[...], wu_ref[...])
    rows = g.shape[0]
    g_prev = pltpu.roll(g, 1, axis=0)[HALO:HALO + tm]
    g_next = pltpu.roll(g, rows - 1, axis=0)[HALO:HALO + tm]
    cw = cw_ref[...]
    c = g_prev * cw[0:1] + g[HALO:HALO + tm] * cw[1:2] + g_next * cw[2:3] + cb_ref[...]
    o_ref[...] = (c / (1.0 + jnp.exp(-c)) * u).astype(BF16)


def _ffn_up(h, w_gate, w_up, conv_w, conv_b, *, seq, tm, tf):
    t = h.shape[0]
    hb = tm // HALO
    n_halo = t // HALO
    return pl.pallas_call(
        functools.partial(_ffn_up_kernel, seq=seq),
        grid=(t // tm, D_FF // tf),
        in_specs=[
            pl.BlockSpec((tm, D_MODEL), lambda i, j: (i, 0)),
            pl.BlockSpec((HALO, D_MODEL), lambda i, j: (jnp.maximum(i * hb - 1, 0), 0)),
            pl.BlockSpec((HALO, D_MODEL), lambda i, j: (jnp.minimum((i + 1) * hb, n_halo - 1), 0)),
            pl.BlockSpec((D_MODEL, tf), lambda i, j: (0, j)),
            pl.BlockSpec((D_MODEL, tf), lambda i, j: (0, j)),
            pl.BlockSpec((3, tf), lambda i, j: (0, j)),
            pl.BlockSpec((1, tf), lambda i, j: (0, j)),
        ],
        out_specs=pl.BlockSpec((tm, tf), lambda i, j: (i, j)),
        out_shape=jax.ShapeDtypeStruct((t, D_FF), BF16),
        scratch_shapes=[pltpu.VMEM((tm + 2 * HALO, D_MODEL), BF16)],
        compiler_params=_params("parallel", "arbitrary"),
        name="ffn_up",
    )(h, h, h, w_gate, w_up, conv_w, conv_b)


def _ffn_down_kernel(a_ref, w_ref, x_ref, g_ref, o_ref):
    o_ref[...] = _rms(x_ref[...] + _dot(a_ref[...], w_ref[...]), g_ref[...])


def _ffn_down(act, w_down, x, g_final, *, tm):
    t = x.shape[0]
    return pl.pallas_call(
        _ffn_down_kernel,
        grid=(t // tm,),
        in_specs=[
            pl.BlockSpec((tm, D_FF), lambda i: (i, 0)),
            _const_spec(w_down.shape),
            pl.BlockSpec((tm, D_MODEL), lambda i: (i, 0)),
            _const_spec(g_final.shape),
        ],
        out_specs=pl.BlockSpec((tm, D_MODEL), lambda i: (i, 0)),
        out_shape=jax.ShapeDtypeStruct((t, D_MODEL), F32),
        compiler_params=_params("parallel"),
        name="ffn_down",
    )(act, w_down, x, g_final)


def _dft_tables(seq):
    n1 = DFT_N1
    n2 = seq // n1
    c = np.arange(FOURIER_GROUP_DIM)
    ang = 2.0 * np.pi * ((c[:, None] * c[None, :]) % FOURIER_GROUP_DIM) / FOURIER_GROUP_DIM
    dft_c = np.concatenate([np.cos(ang), np.sin(ang)], axis=1)

    j = np.arange(n1)
    a1 = 2.0 * np.pi * ((j[:, None] * j[None, :]) % n1) / n1
    c1, s1 = np.cos(a1), np.sin(a1)
    m1 = np.block([[c1, -s1], [-s1, -c1]])

    k1 = np.arange(n1)[:, None, None]
    k2 = np.arange(n2)[None, :, None]
    m2 = np.arange(n2)[None, None, :]
    a2 = 2.0 * np.pi * ((m2 * (k1 + n1 * k2)) % seq) / seq
    e = np.concatenate([np.cos(a2), np.sin(a2)], axis=2)
    return (jnp.asarray(dft_c, BF16), jnp.asarray(m1, BF16), jnp.asarray(e, BF16))


def _rope_tables(seq):
    inv = ROPE_THETA ** (-jnp.arange(0, QK_ROPE_DIM, 2, dtype=F32) / QK_ROPE_DIM)
    ang = jnp.arange(seq, dtype=F32)[:, None] * inv[None, :]
    cos, sin = jnp.cos(ang), jnp.sin(ang)
    return (jnp.concatenate([cos, cos, cos, cos], axis=1), jnp.concatenate([-sin, sin, -sin, sin], axis=1),
            jnp.concatenate([cos.T, cos.T], axis=0), jnp.concatenate([-sin.T, sin.T], axis=0))


def _prep_weights(w_in, w_uq, w_ukv):
    half = QK_ROPE_DIM // 2
    c1 = FOURIER_DIM + Q_LORA_RANK + KV_LORA_RANK
    w_in_ext = jnp.concatenate(
        [w_in, w_in[:, c1 + half:c1 + QK_ROPE_DIM], w_in[:, c1:c1 + half]], axis=1).astype(BF16)
    q3 = w_uq.reshape(Q_LORA_RANK, N_HEADS, QK_DIM)
    rope = q3[:, :, QK_NOPE_DIM:]
    rope_sw = jnp.concatenate([rope[:, :, half:], rope[:, :, :half]], axis=2)
    w_uqt_ext = jnp.concatenate(
        [q3[:, :, :QK_NOPE_DIM].reshape(Q_LORA_RANK, -1), rope.reshape(Q_LORA_RANK, -1),
         rope_sw.reshape(Q_LORA_RANK, -1)], axis=1).T.astype(BF16)
    kv3 = w_ukv.reshape(KV_LORA_RANK, N_HEADS, QK_NOPE_DIM + V_HEAD_DIM)
    w_k = kv3[:, :, :QK_NOPE_DIM].reshape(KV_LORA_RANK, -1).astype(BF16)
    w_vt = kv3[:, :, QK_NOPE_DIM:].reshape(KV_LORA_RANK, -1).T.astype(BF16)
    return w_in_ext, w_uqt_ext, w_k, w_vt


def _tile(n, pref):
    return pref if n % pref == 0 else n


def _trunk(x, mem, p, tables):
    b, s, _ = x.shape
    t = b * s
    n1 = DFT_N1
    n2 = s // n1
    dft_c, m1, e, cos2, sin2, cos_t, sin_t = tables

    ab, qt, k, vt = _in_proj(x, p["g_mix"], p["w_in_ext"], p["g_q"], p["w_uqt_ext"], p["g_kv"], p["w_k"],
                             p["w_vt"], dft_c, cos2, sin2, cos_t, sin_t, tm=_tile(s, 256), tk=_tile(s, 512))

    cols = n2 * FOURIER_GROUP_DIM
    gc = _fourier_a(ab.reshape(b, N_FOURIER_GROUPS, 2, n1, cols), m1, tc=_tile(cols, 4096))
    f = _fourier_b(gc.reshape(b, N_FOURIER_GROUPS, 2, n1, n2, FOURIER_GROUP_DIM), e, kb=32,
                   scale=float((s * FOURIER_GROUP_DIM) ** -0.5))

    nk = s // _tile(s, 512)
    a = _attention(qt, k, vt, tq=_tile(s, 1024), group_size=4 if nk % 4 == 0 else 2)

    x1 = _out_proj(x.reshape(t, D_MODEL), f.reshape(t, FOURIER_DIM), a.reshape(t, MLA_DIM),
                   p["g_f"], p["g_a"], p["w_out"], tm=_tile(t, 512))

    kvm = _mem_kv(mem, p["g_mem"], p["w_ckv"], tn=1024)
    x2, hf = _cross(x1.reshape(b, s, D_MODEL), kvm, p["g_cross"], p["w_cq"], p["w_co"], p["g_ffn"],
                    tm=_tile(s, 256))

    act = _ffn_up(hf.reshape(t, D_MODEL), p["w_gate"], p["w_up"], p["conv_w"], p["conv_b"], seq=s,
                  tm=_tile(s, 1024), tf=512)
    y = _ffn_down(act, p["w_down"], x2.reshape(t, D_MODEL), p["g_final"], tm=_tile(t, 256))
    return y.reshape(b, s, D_MODEL)


@jax.jit
def kernel(x_prompt, x_sample, mem_prompt, mem_sample, norm_mix_g, w_in, q_norm_g, w_uq, kv_norm_g, w_ukv,
           fourier_out_g, mla_out_g, w_out, norm_cross_g, norm_mem_g, w_cq, w_ckv, w_co, norm_ffn_g, w_gate,
           w_up, conv_w, conv_b, w_down, final_norm_g):
    assert norm_mix_g.shape[0] == 1, "single-layer trunk"
    assert x_prompt.shape[1] % DFT_N1 == 0 and x_sample.shape[1] % DFT_N1 == 0
    w_in_ext, w_uqt_ext, w_k, w_vt = _prep_weights(w_in[0], w_uq[0], w_ukv[0])
    row = lambda g: g.reshape(1, -1)
    p = dict(
        g_mix=row(norm_mix_g[0]), w_in_ext=w_in_ext, g_q=row(q_norm_g[0]), w_uqt_ext=w_uqt_ext,
        g_kv=row(kv_norm_g[0]), w_k=w_k, w_vt=w_vt, g_f=row(fourier_out_g[0]), g_a=row(mla_out_g[0]),
        w_out=w_out[0].astype(BF16), g_cross=row(norm_cross_g[0]), g_mem=row(norm_mem_g[0]),
        w_cq=w_cq[0].astype(BF16), w_ckv=w_ckv[0].astype(BF16), w_co=w_co[0].astype(BF16),
        g_ffn=row(norm_ffn_g[0]), w_gate=w_gate[0].astype(BF16), w_up=w_up[0].astype(BF16),
        conv_w=conv_w[0], conv_b=row(conv_b[0]), w_down=w_down[0].astype(BF16), g_final=row(final_norm_g),
    )
    outs = []
    for x, mem in ((x_prompt, mem_prompt), (x_sample, mem_sample)):
        s = x.shape[1]
        tables = _dft_tables(s) + _rope_tables(s)
        outs.append(_trunk(x, mem, p, tables))
    return tuple(outs)
```

```python
import functools

import numpy as np
import jax
import jax.numpy as jnp
from jax import lax
from jax.experimental import pallas as pl
from jax.experimental.pallas import tpu as pltpu

F32 = jnp.float32
BF16 = jnp.bfloat16

D_MODEL = 2048
N_FOURIER_GROUPS = 4
FOURIER_GROUP_DIM = 256
FOURIER_DIM = N_FOURIER_GROUPS * FOURIER_GROUP_DIM
N_HEADS = 8
QK_NOPE_DIM = 128
QK_ROPE_DIM = 64
QK_DIM = QK_NOPE_DIM + QK_ROPE_DIM
V_HEAD_DIM = 128
V_EXT_DIM = V_HEAD_DIM + 16
Q_LORA_RANK = 512
KV_LORA_RANK = 512
MLA_DIM = N_HEADS * V_HEAD_DIM
ROPE_THETA = 10000.0
N_CROSS_HEADS = 4
CROSS_HEAD_DIM = D_MODEL // N_CROSS_HEADS
D_FF = 5632
EPS = 1e-6
LOG2_E = 1.4426950408889634

V7X_VMEM_LIMIT_BYTES = 56 * 1024 * 1024
DFT_N1 = 128
HALO = 16


def _params(*sem):
    return pltpu.CompilerParams(dimension_semantics=sem, vmem_limit_bytes=V7X_VMEM_LIMIT_BYTES)


def _const_spec(shape):
    zeros = (0,) * len(shape)
    return pl.BlockSpec(shape, lambda *_: zeros, pipeline_mode=pl.Buffered(1))


def _rms(x, g):
    return x * lax.rsqrt(jnp.mean(x * x, axis=-1, keepdims=True) + EPS) * g


def _dot(a, b):
    return jnp.dot(a, b, preferred_element_type=F32)


def _dot_nt(a, b):
    return lax.dot_general(a, b, (((1,), (1,)), ((), ())), preferred_element_type=F32)


def _in_proj_kernel(x_ref, g_ref, win_ref, qg_ref, wuqt_ref, kvg_ref, wk_ref, wvt_ref, dft_ref, cos_ref,
                    sin_ref, cost_ref, sint_ref, ab_ref, qt_ref, k_ref, vt_ref):
    h = _rms(x_ref[...], g_ref[...]).astype(BF16)
    z = _dot(h, win_ref[...])

    for g in range(N_FOURIER_GROUPS):
        u = z[:, g * FOURIER_GROUP_DIM:(g + 1) * FOURIER_GROUP_DIM].astype(BF16)
        ab = _dot(u, dft_ref[...])
        ab_ref[g, 0] = ab[:, :FOURIER_GROUP_DIM].astype(BF16)
        ab_ref[g, 1] = ab[:, FOURIER_GROUP_DIM:].astype(BF16)

    c0 = FOURIER_DIM
    c_q = z[:, c0:c0 + Q_LORA_RANK]
    c_kv = z[:, c0 + Q_LORA_RANK:c0 + Q_LORA_RANK + KV_LORA_RANK]
    c1 = c0 + Q_LORA_RANK + KV_LORA_RANK
    k_r = z[:, c1:c1 + QK_ROPE_DIM]
    k_r_sw = z[:, c1 + QK_ROPE_DIM:c1 + 2 * QK_ROPE_DIM]

    cos = cos_ref[...]
    sin = sin_ref[...]
    k_rope = (k_r * cos[:, :QK_ROPE_DIM] + k_r_sw * sin[:, :QK_ROPE_DIM]).astype(BF16)

    hq = _rms(c_q, qg_ref[...]).astype(BF16)
    qe_t = _dot_nt(wuqt_ref[...], hq)
    n_rope = N_HEADS * QK_ROPE_DIM
    q_nope_w = N_HEADS * QK_NOPE_DIM
    cos_t = jnp.concatenate([cost_ref[...]] * N_HEADS, axis=0)
    sin_t = jnp.concatenate([sint_ref[...]] * N_HEADS, axis=0)
    q_rope_t = qe_t[q_nope_w:q_nope_w + n_rope] * cos_t + qe_t[q_nope_w + n_rope:] * sin_t

    hkv = _rms(c_kv, kvg_ref[...]).astype(BF16)
    k_nope = _dot(hkv, wk_ref[...])
    v_t = _dot_nt(wvt_ref[...], hkv)

    scale = QK_DIM ** -0.5 * LOG2_E
    for hd in range(N_HEADS):
        qt_ref[hd, 0:QK_NOPE_DIM, :] = (qe_t[hd * QK_NOPE_DIM:(hd + 1) * QK_NOPE_DIM] * scale).astype(BF16)
        qt_ref[hd, QK_NOPE_DIM:QK_DIM, :] = (
            q_rope_t[hd * QK_ROPE_DIM:(hd + 1) * QK_ROPE_DIM] * scale).astype(BF16)
        k_ref[hd, :, 0:QK_NOPE_DIM] = k_nope[:, hd * QK_NOPE_DIM:(hd + 1) * QK_NOPE_DIM].astype(BF16)
        k_ref[hd, :, QK_NOPE_DIM:QK_DIM] = k_rope
        vt_ref[hd, 0:V_HEAD_DIM, :] = v_t[hd * V_HEAD_DIM:(hd + 1) * V_HEAD_DIM, :].astype(BF16)
        pad_row = lax.broadcasted_iota(jnp.int32, (V_EXT_DIM - V_HEAD_DIM, v_t.shape[1]), 0)
        vt_ref[hd, V_HEAD_DIM:, :] = jnp.where(pad_row == 0, 1.0, 0.0).astype(BF16)


def _in_proj(x, g_mix, w_in_ext, q_g, w_uqt_ext, kv_g, w_k, w_vt, dft_c, cos2, sin2, cos_t, sin_t, *, tm, tk):
    b, s, _ = x.shape
    grid = (b, s // tm)
    r = tk // tm
    return pl.pallas_call(
        _in_proj_kernel,
        grid=grid,
        in_specs=[
            pl.BlockSpec((None, tm, D_MODEL), lambda bi, i: (bi, i, 0)),
            _const_spec(g_mix.shape),
            _const_spec(w_in_ext.shape),
            _const_spec(q_g.shape),
            _const_spec(w_uqt_ext.shape),
            _const_spec(kv_g.shape),
            _const_spec(w_k.shape),
            _const_spec(w_vt.shape),
            _const_spec(dft_c.shape),
            pl.BlockSpec((tm, 128), lambda bi, i: (i, 0)),
            pl.BlockSpec((tm, 128), lambda bi, i: (i, 0)),
            pl.BlockSpec((QK_ROPE_DIM, tm), lambda bi, i: (0, i)),
            pl.BlockSpec((QK_ROPE_DIM, tm), lambda bi, i: (0, i)),
        ],
        out_specs=[
            pl.BlockSpec((None, N_FOURIER_GROUPS, 2, tm, FOURIER_GROUP_DIM), lambda bi, i: (bi, 0, 0, i, 0)),
            pl.BlockSpec((None, N_HEADS, QK_DIM, tm), lambda bi, i: (bi, 0, 0, i)),
            pl.BlockSpec((None, N_HEADS, tm, QK_DIM), lambda bi, i: (bi, 0, i, 0)),
            pl.BlockSpec((None, N_HEADS, None, V_EXT_DIM, tm), lambda bi, i: (bi, 0, i // r, 0, i % r)),
        ],
        out_shape=[
            jax.ShapeDtypeStruct((b, N_FOURIER_GROUPS, 2, s, FOURIER_GROUP_DIM), BF16),
            jax.ShapeDtypeStruct((b, N_HEADS, QK_DIM, s), BF16),
            jax.ShapeDtypeStruct((b, N_HEADS, s, QK_DIM), BF16),
            jax.ShapeDtypeStruct((b, N_HEADS, s // tk, V_EXT_DIM, tk), BF16),
        ],
        compiler_params=_params("parallel", "parallel"),
        name="in_proj",
    )(x, g_mix, w_in_ext, q_g, w_uqt_ext, kv_g, w_k, w_vt, dft_c, cos2, sin2, cos_t, sin_t)


def _fourier_a_kernel(m_ref, x_ref, o_ref):
    two, n1, tc = x_ref.shape
    x = x_ref[...].reshape(two * n1, tc)
    o_ref[...] = _dot(m_ref[...], x).astype(BF16).reshape(two, n1, tc)


def _fourier_a(ab, m1, *, tc):
    b, g, two, n1, cols = ab.shape
    spec = pl.BlockSpec((None, None, two, n1, tc), lambda bi, gi, ci: (bi, gi, 0, 0, ci))
    return pl.pallas_call(
        _fourier_a_kernel,
        grid=(b, g, cols // tc),
        in_specs=[_const_spec(m1.shape), spec],
        out_specs=spec,
        out_shape=jax.ShapeDtypeStruct(ab.shape, BF16),
        compiler_params=_params("parallel", "parallel", "parallel"),
        name="fourier_a",
    )(m1, ab)


def _fourier_b_kernel(e_ref, g_ref, o_ref, *, scale):
    kb = e_ref.shape[0]
    for j in range(kb):
        x = jnp.concatenate([g_ref[0, j], g_ref[1, j]], axis=0)
        o_ref[:, j, :] = _dot(e_ref[j], x) * scale


def _fourier_b(gc, e, *, kb, scale):
    b, g, two, n1, n2, c = gc.shape
    return pl.pallas_call(
        functools.partial(_fourier_b_kernel, scale=scale),
        grid=(b, g, n1 // kb),
        in_specs=[
            pl.BlockSpec((kb, n2, 2 * n2), lambda bi, gi, ki: (ki, 0, 0)),
            pl.BlockSpec((None, None, two, kb, n2, c), lambda bi, gi, ki: (bi, gi, 0, ki, 0, 0)),
        ],
        out_specs=pl.BlockSpec((None, n2, kb, c), lambda bi, gi, ki: (bi, 0, ki, gi)),
        out_shape=jax.ShapeDtypeStruct((b, n2, n1, g * c), F32),
        compiler_params=_params("parallel", "parallel", "parallel"),
        name="fourier_b",
    )(e, gc)


def _attn_kernel(qt_ref, k_ref, vt_ref, o_ref, sa_ref, sb_ref, acc_ref, *, group_size):
    qt = qt_ref[...]
    tq = qt.shape[1]
    nk, _, tk = vt_ref.shape

    def scores(j, s_ref):
        start = pl.multiple_of(j * tk, tk)
        s = _dot(k_ref[pl.ds(start, tk), :], qt)
        s_ref[...] = s
        return jnp.max(s, axis=0, keepdims=True)

    def update(j, s_ref, cmax, m):
        m_new = jnp.maximum(m, cmax)
        p = jnp.exp2(s_ref[...] - m_new)
        acc_ref[...] = jnp.exp2(m - m_new) * acc_ref[...] + _dot(vt_ref[j], p.astype(BF16))
        return m_new

    bufs = (sa_ref, sb_ref)

    def group(j, cmax, m, more):
        for t in range(group_size):
            nxt = scores(j + t + 1, bufs[(t + 1) % 2]) if (more or t + 1 < group_size) else None
            m = update(j + t, bufs[t % 2], cmax, m)
            cmax = nxt
        return cmax, m

    acc_ref[...] = jnp.zeros_like(acc_ref)
    carry = (scores(0, sa_ref), jnp.full((1, tq), -jnp.inf, F32))
    carry = lax.fori_loop(0, nk // group_size - 1, lambda i, c: group(group_size * i, *c, True), carry)
    group(nk - group_size, *carry, False)
    acc = acc_ref[...]
    o_ref[...] = (acc[0:V_HEAD_DIM] / acc[V_HEAD_DIM:V_HEAD_DIM + 1]).T


def _attention(qt, k, vt, *, tq, group_size):
    b, h, s, _ = k.shape
    nk, _, tk = vt.shape[2:]
    assert group_size % 2 == 0 and nk % group_size == 0
    return pl.pallas_call(
        functools.partial(_attn_kernel, group_size=group_size),
        grid=(b, h, s // tq),
        in_specs=[
            pl.BlockSpec((None, None, QK_DIM, tq), lambda bi, hi, i: (bi, hi, 0, i)),
            pl.BlockSpec((None, None, s, QK_DIM), lambda bi, hi, i: (bi, hi, 0, 0)),
            pl.BlockSpec((None, None, nk, V_EXT_DIM, tk), lambda bi, hi, i: (bi, hi, 0, 0, 0)),
        ],
        out_specs=pl.BlockSpec((None, tq, V_HEAD_DIM), lambda bi, hi, i: (bi, i, hi)),
        out_shape=jax.ShapeDtypeStruct((b, s, h * V_HEAD_DIM), F32),
        scratch_shapes=[pltpu.VMEM((tk, tq), F32), pltpu.VMEM((tk, tq), F32), pltpu.VMEM((V_EXT_DIM, tq), F32)],
        compiler_params=_params("parallel", "parallel", "parallel"),
        name="attention",
    )(qt, k, vt)


def _out_proj_kernel(x_ref, f_ref, a_ref, gf_ref, ga_ref, w_ref, o_ref):
    hf = _rms(f_ref[...], gf_ref[...]).astype(BF16)
    ha = _rms(a_ref[...], ga_ref[...]).astype(BF16)
    o_ref[...] = x_ref[...] + _dot(hf, w_ref[0:FOURIER_DIM, :]) + _dot(ha, w_ref[FOURIER_DIM:, :])


def _out_proj(x, f, a, g_f, g_a, w_out, *, tm):
    t = x.shape[0]
    return pl.pallas_call(
        _out_proj_kernel,
        grid=(t // tm,),
        in_specs=[
            pl.BlockSpec((tm, D_MODEL), lambda i: (i, 0)),
            pl.BlockSpec((tm, FOURIER_DIM), lambda i: (i, 0)),
            pl.BlockSpec((tm, MLA_DIM), lambda i: (i, 0)),
            _const_spec(g_f.shape),
            _const_spec(g_a.shape),
            _const_spec(w_out.shape),
        ],
        out_specs=pl.BlockSpec((tm, D_MODEL), lambda i: (i, 0)),
        out_shape=jax.ShapeDtypeStruct((t, D_MODEL), F32),
        compiler_params=_params("parallel"),
        name="out_proj",
    )(x, f, a, g_f, g_a, w_out)


def _mem_kv_kernel(m_ref, g_ref, w_ref, o_ref):
    o_ref[...] = _dot(_rms(m_ref[...], g_ref[...]).astype(BF16), w_ref[...]).astype(BF16)


def _mem_kv(mem, g_mem, w_ckv, *, tn):
    b, m, _ = mem.shape
    n = w_ckv.shape[1]
    return pl.pallas_call(
        _mem_kv_kernel,
        grid=(b, n // tn),
        in_specs=[
            pl.BlockSpec((None, m, D_MODEL), lambda bi, j: (bi, 0, 0)),
            _const_spec(g_mem.shape),
            pl.BlockSpec((D_MODEL, tn), lambda bi, j: (0, j)),
        ],
        out_specs=pl.BlockSpec((None, m, tn), lambda bi, j: (bi, 0, j)),
        out_shape=jax.ShapeDtypeStruct((b, m, n), BF16),
        compiler_params=_params("parallel", "parallel"),
        name="mem_kv",
    )(mem, g_mem, w_ckv)


def _cross_kernel(x_ref, kv_ref, gc_ref, wq_ref, wo_ref, gn_ref, o_ref, h_ref):
    x = x_ref[...]
    h = _rms(x, gc_ref[...]).astype(BF16)
    q = _dot(h, wq_ref[...]) * (CROSS_HEAD_DIM ** -0.5)
    heads = []
    for hd in range(N_CROSS_HEADS):
        lo = hd * CROSS_HEAD_DIM
        s = _dot_nt(q[:, lo:lo + CROSS_HEAD_DIM].astype(BF16), kv_ref[:, lo:lo + CROSS_HEAD_DIM])
        p = jnp.exp(s - jnp.max(s, axis=-1, keepdims=True))
        p = p / jnp.sum(p, axis=-1, keepdims=True)
        heads.append(_dot(p.astype(BF16), kv_ref[:, D_MODEL + lo:D_MODEL + lo + CROSS_HEAD_DIM]).astype(BF16))
    y = x + _dot(jnp.concatenate(heads, axis=1), wo_ref[...])
    o_ref[...] = y
    h_ref[...] = _rms(y, gn_ref[...]).astype(BF16)


def _cross(x, kvm, g_cross, w_cq, w_co, g_ffn, *, tm):
    b, s, _ = x.shape
    m = kvm.shape[1]
    spec = pl.BlockSpec((None, tm, D_MODEL), lambda bi, i: (bi, i, 0))
    return pl.pallas_call(
        _cross_kernel,
        grid=(b, s // tm),
        in_specs=[
            spec,
            pl.BlockSpec((None, m, 2 * D_MODEL), lambda bi, i: (bi, 0, 0)),
            _const_spec(g_cross.shape),
            _const_spec(w_cq.shape),
            _const_spec(w_co.shape),
            _const_spec(g_ffn.shape),
        ],
        out_specs=[spec, spec],
        out_shape=[jax.ShapeDtypeStruct((b, s, D_MODEL), F32), jax.ShapeDtypeStruct((b, s, D_MODEL), BF16)],
        compiler_params=_params("parallel", "parallel"),
        name="cross",
    )(x, kvm, g_cross, w_cq, w_co, g_ffn)


def _ffn_up_kernel(h_ref, prev_ref, next_ref, wg_ref, wu_ref, cw_ref, cb_ref, o_ref, hext_ref, *, seq):
    i = pl.program_id(0)
    tm = h_ref.shape[0]

    @pl.when(pl.program_id(1) == 0)
    def _():
        first = (i * tm) % seq == 0
        last = ((i + 1) * tm) % seq == 0
        prev = prev_ref[...]
        nxt = next_ref[...]
        hext_ref[0:HALO, :] = jnp.where(first, jnp.zeros_like(prev), prev)
        hext_ref[HALO:HALO + tm, :] = h_ref[...]
        hext_ref[HALO + tm:, :] = jnp.where(last, jnp.zeros_like(nxt), nxt)

    g = _dot(hext_ref[...], wg_ref[...])
    u = _dot(h_ref[...], wu_ref[...])
    rows = g.shape[0]
    g_prev = pltpu.roll(g, 1, axis=0)[HALO:HALO + tm]
    g_next = pltpu.roll(g, rows - 1, axis=0)[HALO:HALO + tm]
    cw = cw_ref[...]
    c = g_prev * cw[0:1] + g[HALO:HALO + tm] * cw[1:2] + g_next * cw[2:3] + cb_ref[...]
    o_ref[...] = (c / (1.0 + jnp.exp(-c)) * u).astype(BF16)


def _ffn_up(h, w_gate, w_up, conv_w, conv_b, *, seq, tm, tf):
    t = h.shape[0]
    hb = tm // HALO
    n_halo = t // HALO
    return pl.pallas_call(
        functools.partial(_ffn_up_kernel, seq=seq),
        grid=(t // tm, D_FF // tf),
        in_specs=[
            pl.BlockSpec((tm, D_MODEL), lambda i, j: (i, 0)),
            pl.BlockSpec((HALO, D_MODEL), lambda i, j: (jnp.maximum(i * hb - 1, 0), 0)),
            pl.BlockSpec((HALO, D_MODEL), lambda i, j: (jnp.minimum((i + 1) * hb, n_halo - 1), 0)),
            pl.BlockSpec((D_MODEL, tf), lambda i, j: (0, j)),
            pl.BlockSpec((D_MODEL, tf), lambda i, j: (0, j)),
            pl.BlockSpec((3, tf), lambda i, j: (0, j)),
            pl.BlockSpec((1, tf), lambda i, j: (0, j)),
        ],
        out_specs=pl.BlockSpec((tm, tf), lambda i, j: (i, j)),
        out_shape=jax.ShapeDtypeStruct((t, D_FF), BF16),
        scratch_shapes=[pltpu.VMEM((tm + 2 * HALO, D_MODEL), BF16)],
        compiler_params=_params("parallel", "arbitrary"),
        name="ffn_up",
    )(h, h, h, w_gate, w_up, conv_w, conv_b)


def _ffn_down_kernel(a_ref, w_ref, x_ref, g_ref, o_ref):
    o_ref[...] = _rms(x_ref[...] + _dot(a_ref[...], w_ref[...]), g_ref[...])


def _ffn_down(act, w_down, x, g_final, *, tm):
    t = x.shape[0]
    return pl.pallas_call(
        _ffn_down_kernel,
        grid=(t // tm,),
        in_specs=[
            pl.BlockSpec((tm, D_FF), lambda i: (i, 0)),
            _const_spec(w_down.shape),
            pl.BlockSpec((tm, D_MODEL), lambda i: (i, 0)),
            _const_spec(g_final.shape),
        ],
        out_specs=pl.BlockSpec((tm, D_MODEL), lambda i: (i, 0)),
        out_shape=jax.ShapeDtypeStruct((t, D_MODEL), F32),
        compiler_params=_params("parallel"),
        name="ffn_down",
    )(act, w_down, x, g_final)


def _dft_tables(seq):
    n1 = DFT_N1
    n2 = seq // n1
    c = np.arange(FOURIER_GROUP_DIM)
    ang = 2.0 * np.pi * ((c[:, None] * c[None, :]) % FOURIER_GROUP_DIM) / FOURIER_GROUP_DIM
    dft_c = np.concatenate([np.cos(ang), np.sin(ang)], axis=1)

    j = np.arange(n1)
    a1 = 2.0 * np.pi * ((j[:, None] * j[None, :]) % n1) / n1
    c1, s1 = np.cos(a1), np.sin(a1)
    m1 = np.block([[c1, -s1], [-s1, -c1]])

    k1 = np.arange(n1)[:, None, None]
    k2 = np.arange(n2)[None, :, None]
    m2 = np.arange(n2)[None, None, :]
    a2 = 2.0 * np.pi * ((m2 * (k1 + n1 * k2)) % seq) / seq
    e = np.concatenate([np.cos(a2), np.sin(a2)], axis=2)
    return (jnp.asarray(dft_c, BF16), jnp.asarray(m1, BF16), jnp.asarray(e, BF16))


def _rope_tables(seq):
    inv = ROPE_THETA ** (-jnp.arange(0, QK_ROPE_DIM, 2, dtype=F32) / QK_ROPE_DIM)
    ang = jnp.arange(seq, dtype=F32)[:, None] * inv[None, :]
    cos, sin = jnp.cos(ang), jnp.sin(ang)
    return (jnp.concatenate([cos, cos, cos, cos], axis=1), jnp.concatenate([-sin, sin, -sin, sin], axis=1),
            jnp.concatenate([cos.T, cos.T], axis=0), jnp.concatenate([-sin.T, sin.T], axis=0))


def _prep_weights(w_in, w_uq, w_ukv):
    half = QK_ROPE_DIM // 2
    c1 = FOURIER_DIM + Q_LORA_RANK + KV_LORA_RANK
    w_in_ext = jnp.concatenate(
        [w_in, w_in[:, c1 + half:c1 + QK_ROPE_DIM], w_in[:, c1:c1 + half]], axis=1).astype(BF16)
    q3 = w_uq.reshape(Q_LORA_RANK, N_HEADS, QK_DIM)
    rope = q3[:, :, QK_NOPE_DIM:]
    rope_sw = jnp.concatenate([rope[:, :, half:], rope[:, :, :half]], axis=2)
    w_uqt_ext = jnp.concatenate(
        [q3[:, :, :QK_NOPE_DIM].reshape(Q_LORA_RANK, -1), rope.reshape(Q_LORA_RANK, -1),
         rope_sw.reshape(Q_LORA_RANK, -1)], axis=1).T.astype(BF16)
    kv3 = w_ukv.reshape(KV_LORA_RANK, N_HEADS, QK_NOPE_DIM + V_HEAD_DIM)
    w_k = kv3[:, :, :QK_NOPE_DIM].reshape(KV_LORA_RANK, -1).astype(BF16)
    w_vt = kv3[:, :, QK_NOPE_DIM:].reshape(KV_LORA_RANK, -1).T.astype(BF16)
    return w_in_ext, w_uqt_ext, w_k, w_vt


def _tile(n, pref):
    return pref if n % pref == 0 else n


def _tiles(s):
    attn_k = _tile(s, 512)
    return dict(
        in_proj=_tile(s, 256),
        attn_q=_tile(s, 1024), attn_k=attn_k, attn_group=4 if (s // attn_k) % 4 == 0 else 2,
        fourier_cols=_tile(s // DFT_N1 * FOURIER_GROUP_DIM, 4096), fourier_kb=32,
        out_proj=_tile(s, 512), mem_kv_cols=1024, cross=_tile(s, 256),
        ffn_up=_tile(s, 1024), ffn_cols=512, ffn_down=_tile(s, 256),
    )


def _trunk(x, mem, p, tables):
    b, s, _ = x.shape
    t = b * s
    n1 = DFT_N1
    n2 = s // n1
    ts = _tiles(s)
    dft_c, m1, e, cos2, sin2, cos_t, sin_t = tables

    ab, qt, k, vt = _in_proj(x, p["g_mix"], p["w_in_ext"], p["g_q"], p["w_uqt_ext"], p["g_kv"], p["w_k"],
                             p["w_vt"], dft_c, cos2, sin2, cos_t, sin_t, tm=ts["in_proj"], tk=ts["attn_k"])

    gc = _fourier_a(ab.reshape(b, N_FOURIER_GROUPS, 2, n1, n2 * FOURIER_GROUP_DIM), m1, tc=ts["fourier_cols"])
    f = _fourier_b(gc.reshape(b, N_FOURIER_GROUPS, 2, n1, n2, FOURIER_GROUP_DIM), e, kb=ts["fourier_kb"],
                   scale=float((s * FOURIER_GROUP_DIM) ** -0.5))

    a = _attention(qt, k, vt, tq=ts["attn_q"], group_size=ts["attn_group"])

    x1 = _out_proj(x.reshape(t, D_MODEL), f.reshape(t, FOURIER_DIM), a.reshape(t, MLA_DIM),
                   p["g_f"], p["g_a"], p["w_out"], tm=ts["out_proj"])

    kvm = _mem_kv(mem, p["g_mem"], p["w_ckv"], tn=ts["mem_kv_cols"])
    x2, hf = _cross(x1.reshape(b, s, D_MODEL), kvm, p["g_cross"], p["w_cq"], p["w_co"], p["g_ffn"],
                    tm=ts["cross"])

    act = _ffn_up(hf.reshape(t, D_MODEL), p["w_gate"], p["w_up"], p["conv_w"], p["conv_b"], seq=s,
                  tm=ts["ffn_up"], tf=ts["ffn_cols"])
    y = _ffn_down(act, p["w_down"], x2.reshape(t, D_MODEL), p["g_final"], tm=ts["ffn_down"])
    return y.reshape(b, s, D_MODEL)


@jax.jit
def kernel(x_prompt, x_sample, mem_prompt, mem_sample, norm_mix_g, w_in, q_norm_g, w_uq, kv_norm_g, w_ukv,
           fourier_out_g, mla_out_g, w_out, norm_cross_g, norm_mem_g, w_cq, w_ckv, w_co, norm_ffn_g, w_gate,
           w_up, conv_w, conv_b, w_down, final_norm_g):
    assert norm_mix_g.shape[0] == 1, "single-layer trunk"
    assert x_prompt.shape[1] % DFT_N1 == 0 and x_sample.shape[1] % DFT_N1 == 0
    w_in_ext, w_uqt_ext, w_k, w_vt = _prep_weights(w_in[0], w_uq[0], w_ukv[0])
    row = lambda g: g.reshape(1, -1)
    p = dict(
        g_mix=row(norm_mix_g[0]), w_in_ext=w_in_ext, g_q=row(q_norm_g[0]), w_uqt_ext=w_uqt_ext,
        g_kv=row(kv_norm_g[0]), w_k=w_k, w_vt=w_vt, g_f=row(fourier_out_g[0]), g_a=row(mla_out_g[0]),
        w_out=w_out[0].astype(BF16), g_cross=row(norm_cross_g[0]), g_mem=row(norm_mem_g[0]),
        w_cq=w_cq[0].astype(BF16), w_ckv=w_ckv[0].astype(BF16), w_co=w_co[0].astype(BF16),
        g_ffn=row(norm_ffn_g[0]), w_gate=w_gate[0].astype(BF16), w_up=w_up[0].astype(BF16),
        conv_w=conv_w[0], conv_b=row(conv_b[0]), w_down=w_down[0].astype(BF16), g_final=row(final_norm_g),
    )
    outs = []
    for x, mem in ((x_prompt, mem_prompt), (x_sample, mem_sample)):
        s = x.shape[1]
        tables = _dft_tables(s) + _rope_tables(s)
        outs.append(_trunk(x, mem, p, tables))
    return tuple(outs)
```

```python
import functools

import numpy as np
import jax
import jax.numpy as jnp
from jax import lax
from jax.experimental import pallas as pl
from jax.experimental.pallas import tpu as pltpu

F32 = jnp.float32
BF16 = jnp.bfloat16

D_MODEL = 2048
N_FOURIER_GROUPS = 4
FOURIER_GROUP_DIM = 256
FOURIER_DIM = N_FOURIER_GROUPS * FOURIER_GROUP_DIM
N_HEADS = 8
QK_NOPE_DIM = 128
QK_ROPE_DIM = 64
QK_DIM = QK_NOPE_DIM + QK_ROPE_DIM
V_HEAD_DIM = 128
V_EXT_DIM = V_HEAD_DIM + 16
Q_LORA_RANK = 512
KV_LORA_RANK = 512
MLA_DIM = N_HEADS * V_HEAD_DIM
ROPE_THETA = 10000.0
N_CROSS_HEADS = 4
CROSS_HEAD_DIM = D_MODEL // N_CROSS_HEADS
D_FF = 5632
EPS = 1e-6
LOG2_E = 1.4426950408889634

V7X_VMEM_LIMIT_BYTES = 56 * 1024 * 1024
MXU_DIM = 256
DFT_N1 = MXU_DIM // 2
HALO = 16


def _params(*sem):
    return pltpu.CompilerParams(dimension_semantics=sem, vmem_limit_bytes=V7X_VMEM_LIMIT_BYTES)


def _const_spec(shape):
    zeros = (0,) * len(shape)
    return pl.BlockSpec(shape, lambda *_: zeros, pipeline_mode=pl.Buffered(1))


def _rms(x, g):
    return x * lax.rsqrt(jnp.mean(x * x, axis=-1, keepdims=True) + EPS) * g


def _dot(a, b):
    return jnp.dot(a, b, preferred_element_type=F32)


def _dot_nt(a, b):
    return lax.dot_general(a, b, (((1,), (1,)), ((), ())), preferred_element_type=F32)


def _in_proj_kernel(x_ref, g_ref, win_ref, qg_ref, wuqt_ref, kvg_ref, wk_ref, wvt_ref, dft_ref, cos_ref,
                    sin_ref, cost_ref, sint_ref, ab_ref, qt_ref, k_ref, vt_ref):
    h = _rms(x_ref[...], g_ref[...]).astype(BF16)
    z = _dot(h, win_ref[...])

    for g in range(N_FOURIER_GROUPS):
        u = z[:, g * FOURIER_GROUP_DIM:(g + 1) * FOURIER_GROUP_DIM].astype(BF16)
        ab = _dot(u, dft_ref[...])
        ab_ref[g, 0] = ab[:, :FOURIER_GROUP_DIM]
        ab_ref[g, 1] = ab[:, FOURIER_GROUP_DIM:]

    c0 = FOURIER_DIM
    c_q = z[:, c0:c0 + Q_LORA_RANK]
    c_kv = z[:, c0 + Q_LORA_RANK:c0 + Q_LORA_RANK + KV_LORA_RANK]
    c1 = c0 + Q_LORA_RANK + KV_LORA_RANK
    k_r = z[:, c1:c1 + QK_ROPE_DIM]
    k_r_sw = z[:, c1 + QK_ROPE_DIM:c1 + 2 * QK_ROPE_DIM]

    cos = cos_ref[...]
    sin = sin_ref[...]
    k_rope = (k_r * cos[:, :QK_ROPE_DIM] + k_r_sw * sin[:, :QK_ROPE_DIM]).astype(BF16)

    hq = _rms(c_q, qg_ref[...]).astype(BF16)
    qe_t = _dot_nt(wuqt_ref[...], hq)
    n_rope = N_HEADS * QK_ROPE_DIM
    q_nope_w = N_HEADS * QK_NOPE_DIM
    cos_t = jnp.concatenate([cost_ref[...]] * N_HEADS, axis=0)
    sin_t = jnp.concatenate([sint_ref[...]] * N_HEADS, axis=0)
    q_rope_t = qe_t[q_nope_w:q_nope_w + n_rope] * cos_t + qe_t[q_nope_w + n_rope:] * sin_t

    hkv = _rms(c_kv, kvg_ref[...]).astype(BF16)
    k_nope = _dot(hkv, wk_ref[...])
    v_t = _dot_nt(wvt_ref[...], hkv)

    scale = QK_DIM ** -0.5 * LOG2_E
    for hd in range(N_HEADS):
        qt_ref[hd, 0:QK_NOPE_DIM, :] = (qe_t[hd * QK_NOPE_DIM:(hd + 1) * QK_NOPE_DIM] * scale).astype(BF16)
        qt_ref[hd, QK_NOPE_DIM:QK_DIM, :] = (
            q_rope_t[hd * QK_ROPE_DIM:(hd + 1) * QK_ROPE_DIM] * scale).astype(BF16)
        k_ref[hd, :, 0:QK_NOPE_DIM] = k_nope[:, hd * QK_NOPE_DIM:(hd + 1) * QK_NOPE_DIM].astype(BF16)
        k_ref[hd, :, QK_NOPE_DIM:QK_DIM] = k_rope
        vt_ref[hd, 0:V_HEAD_DIM, :] = v_t[hd * V_HEAD_DIM:(hd + 1) * V_HEAD_DIM, :].astype(BF16)
        pad_row = lax.broadcasted_iota(jnp.int32, (V_EXT_DIM - V_HEAD_DIM, v_t.shape[1]), 0)
        vt_ref[hd, V_HEAD_DIM:, :] = jnp.where(pad_row == 0, 1.0, 0.0).astype(BF16)


def _in_proj(x, g_mix, w_in_ext, q_g, w_uqt_ext, kv_g, w_k, w_vt, dft_c, cos2, sin2, cos_t, sin_t, *, tm, tk):
    b, s, _ = x.shape
    grid = (b, s // tm)
    r = tk // tm
    return pl.pallas_call(
        _in_proj_kernel,
        grid=grid,
        in_specs=[
            pl.BlockSpec((None, tm, D_MODEL), lambda bi, i: (bi, i, 0)),
            _const_spec(g_mix.shape),
            _const_spec(w_in_ext.shape),
            _const_spec(q_g.shape),
            _const_spec(w_uqt_ext.shape),
            _const_spec(kv_g.shape),
            _const_spec(w_k.shape),
            _const_spec(w_vt.shape),
            _const_spec(dft_c.shape),
            pl.BlockSpec((tm, 128), lambda bi, i: (i, 0)),
            pl.BlockSpec((tm, 128), lambda bi, i: (i, 0)),
            pl.BlockSpec((QK_ROPE_DIM, tm), lambda bi, i: (0, i)),
            pl.BlockSpec((QK_ROPE_DIM, tm), lambda bi, i: (0, i)),
        ],
        out_specs=[
            pl.BlockSpec((None, N_FOURIER_GROUPS, 2, tm, FOURIER_GROUP_DIM), lambda bi, i: (bi, 0, 0, i, 0)),
            pl.BlockSpec((None, N_HEADS, QK_DIM, tm), lambda bi, i: (bi, 0, 0, i)),
            pl.BlockSpec((None, N_HEADS, tm, QK_DIM), lambda bi, i: (bi, 0, i, 0)),
            pl.BlockSpec((None, N_HEADS, None, V_EXT_DIM, tm), lambda bi, i: (bi, 0, i // r, 0, i % r)),
        ],
        out_shape=[
            jax.ShapeDtypeStruct((b, N_FOURIER_GROUPS, 2, s, FOURIER_GROUP_DIM), F32),
            jax.ShapeDtypeStruct((b, N_HEADS, QK_DIM, s), BF16),
            jax.ShapeDtypeStruct((b, N_HEADS, s, QK_DIM), BF16),
            jax.ShapeDtypeStruct((b, N_HEADS, s // tk, V_EXT_DIM, tk), BF16),
        ],
        compiler_params=_params("parallel", "parallel"),
        name="in_proj",
    )(x, g_mix, w_in_ext, q_g, w_uqt_ext, kv_g, w_k, w_vt, dft_c, cos2, sin2, cos_t, sin_t)


def _fourier_a_kernel(m_ref, x_ref, o_ref):
    two, n1, tn2, c = x_ref.shape
    for j in range(tn2):
        x = jnp.concatenate([x_ref[0, :, j, :], x_ref[1, :, j, :]], axis=0).astype(BF16)
        o_ref[:, :, j * c:(j + 1) * c] = _dot(m_ref[...], x).astype(BF16).reshape(two, n1, c)


def _fourier_a(ab, m1, *, tn2):
    b, g, two, n1, n2, c = ab.shape
    return pl.pallas_call(
        _fourier_a_kernel,
        grid=(b, g, n2 // tn2),
        in_specs=[
            _const_spec(m1.shape),
            pl.BlockSpec((None, None, two, n1, tn2, c), lambda bi, gi, ci: (bi, gi, 0, 0, ci, 0)),
        ],
        out_specs=pl.BlockSpec((None, None, two, n1, tn2 * c), lambda bi, gi, ci: (bi, gi, 0, 0, ci)),
        out_shape=jax.ShapeDtypeStruct((b, g, two, n1, n2 * c), BF16),
        compiler_params=_params("parallel", "parallel", "parallel"),
        name="fourier_a",
    )(m1, ab)


def _fourier_b_kernel(e_ref, g_ref, o_ref, *, scale):
    kb = e_ref.shape[0]
    for j in range(kb):
        x = jnp.concatenate([g_ref[0, j], g_ref[1, j]], axis=0)
        o_ref[:, j, :] = _dot(e_ref[j], x) * scale


def _fourier_b(gc, e, *, kb, scale):
    b, g, two, n1, n2, c = gc.shape
    return pl.pallas_call(
        functools.partial(_fourier_b_kernel, scale=scale),
        grid=(b, g, n1 // kb),
        in_specs=[
            pl.BlockSpec((kb, n2, 2 * n2), lambda bi, gi, ki: (ki, 0, 0)),
            pl.BlockSpec((None, None, two, kb, n2, c), lambda bi, gi, ki: (bi, gi, 0, ki, 0, 0)),
        ],
        out_specs=pl.BlockSpec((None, n2, kb, c), lambda bi, gi, ki: (bi, 0, ki, gi)),
        out_shape=jax.ShapeDtypeStruct((b, n2, n1, g * c), F32),
        compiler_params=_params("parallel", "parallel", "parallel"),
        name="fourier_b",
    )(e, gc)


def _attn_kernel(qt_ref, k_ref, vt_ref, o_ref, sa_ref, sb_ref, acc_ref, *, group_size):
    qt = qt_ref[...]
    tq = qt.shape[1]
    nk, _, tk = vt_ref.shape

    def scores(j, s_ref):
        start = pl.multiple_of(j * tk, tk)
        s = _dot(k_ref[pl.ds(start, tk), :], qt)
        s_ref[...] = s
        return jnp.max(s, axis=0, keepdims=True)

    def update(j, s_ref, cmax, m):
        m_new = jnp.maximum(m, cmax)
        p = jnp.exp2(s_ref[...] - m_new)
        acc_ref[...] = jnp.exp2(m - m_new) * acc_ref[...] + _dot(vt_ref[j], p.astype(BF16))
        return m_new

    bufs = (sa_ref, sb_ref)

    def group(j, cmax, m, more):
        for t in range(group_size):
            nxt = scores(j + t + 1, bufs[(t + 1) % 2]) if (more or t + 1 < group_size) else None
            m = update(j + t, bufs[t % 2], cmax, m)
            cmax = nxt
        return cmax, m

    acc_ref[...] = jnp.zeros_like(acc_ref)
    carry = (scores(0, sa_ref), jnp.full((1, tq), -jnp.inf, F32))
    carry = lax.fori_loop(0, nk // group_size - 1, lambda i, c: group(group_size * i, *c, True), carry)
    group(nk - group_size, *carry, False)
    acc = acc_ref[...]
    o_ref[...] = (acc[0:V_HEAD_DIM] / acc[V_HEAD_DIM:V_HEAD_DIM + 1]).T


def _attention(qt, k, vt, *, tq, group_size):
    b, h, s, _ = k.shape
    nk, _, tk = vt.shape[2:]
    assert group_size % 2 == 0 and nk % group_size == 0
    return pl.pallas_call(
        functools.partial(_attn_kernel, group_size=group_size),
        grid=(b, h, s // tq),
        in_specs=[
            pl.BlockSpec((None, None, QK_DIM, tq), lambda bi, hi, i: (bi, hi, 0, i)),
            pl.BlockSpec((None, None, s, QK_DIM), lambda bi, hi, i: (bi, hi, 0, 0)),
            pl.BlockSpec((None, None, nk, V_EXT_DIM, tk), lambda bi, hi, i: (bi, hi, 0, 0, 0)),
        ],
        out_specs=pl.BlockSpec((None, tq, V_HEAD_DIM), lambda bi, hi, i: (bi, i, hi)),
        out_shape=jax.ShapeDtypeStruct((b, s, h * V_HEAD_DIM), F32),
        scratch_shapes=[pltpu.VMEM((tk, tq), F32), pltpu.VMEM((tk, tq), F32), pltpu.VMEM((V_EXT_DIM, tq), F32)],
        compiler_params=_params("parallel", "parallel", "parallel"),
        name="attention",
    )(qt, k, vt)


def _out_proj_kernel(x_ref, f_ref, a_ref, gf_ref, ga_ref, w_ref, o_ref):
    hf = _rms(f_ref[...], gf_ref[...]).astype(BF16)
    ha = _rms(a_ref[...], ga_ref[...]).astype(BF16)
    o_ref[...] = x_ref[...] + _dot(hf, w_ref[0:FOURIER_DIM, :]) + _dot(ha, w_ref[FOURIER_DIM:, :])


def _out_proj(x, f, a, g_f, g_a, w_out, *, tm):
    t = x.shape[0]
    return pl.pallas_call(
        _out_proj_kernel,
        grid=(t // tm,),
        in_specs=[
            pl.BlockSpec((tm, D_MODEL), lambda i: (i, 0)),
            pl.BlockSpec((tm, FOURIER_DIM), lambda i: (i, 0)),
            pl.BlockSpec((tm, MLA_DIM), lambda i: (i, 0)),
            _const_spec(g_f.shape),
            _const_spec(g_a.shape),
            _const_spec(w_out.shape),
        ],
        out_specs=pl.BlockSpec((tm, D_MODEL), lambda i: (i, 0)),
        out_shape=jax.ShapeDtypeStruct((t, D_MODEL), F32),
        compiler_params=_params("parallel"),
        name="out_proj",
    )(x, f, a, g_f, g_a, w_out)


def _mem_kv_kernel(m_ref, g_ref, w_ref, o_ref):
    o_ref[...] = _dot(_rms(m_ref[...], g_ref[...]).astype(BF16), w_ref[...]).astype(BF16)


def _mem_kv(mem, g_mem, w_ckv, *, tn):
    b, m, _ = mem.shape
    n = w_ckv.shape[1]
    return pl.pallas_call(
        _mem_kv_kernel,
        grid=(b, n // tn),
        in_specs=[
            pl.BlockSpec((None, m, D_MODEL), lambda bi, j: (bi, 0, 0)),
            _const_spec(g_mem.shape),
            pl.BlockSpec((D_MODEL, tn), lambda bi, j: (0, j)),
        ],
        out_specs=pl.BlockSpec((None, m, tn), lambda bi, j: (bi, 0, j)),
        out_shape=jax.ShapeDtypeStruct((b, m, n), BF16),
        compiler_params=_params("parallel", "parallel"),
        name="mem_kv",
    )(mem, g_mem, w_ckv)


def _cross_kernel(x_ref, kv_ref, gc_ref, wq_ref, wo_ref, gn_ref, o_ref, h_ref):
    x = x_ref[...]
    h = _rms(x, gc_ref[...]).astype(BF16)
    q = _dot(h, wq_ref[...]) * (CROSS_HEAD_DIM ** -0.5)
    heads = []
    for hd in range(N_CROSS_HEADS):
        lo = hd * CROSS_HEAD_DIM
        s = _dot_nt(q[:, lo:lo + CROSS_HEAD_DIM].astype(BF16), kv_ref[:, lo:lo + CROSS_HEAD_DIM])
        p = jnp.exp(s - jnp.max(s, axis=-1, keepdims=True))
        p = p / jnp.sum(p, axis=-1, keepdims=True)
        heads.append(_dot(p.astype(BF16), kv_ref[:, D_MODEL + lo:D_MODEL + lo + CROSS_HEAD_DIM]).astype(BF16))
    y = x + _dot(jnp.concatenate(heads, axis=1), wo_ref[...])
    o_ref[...] = y
    h_ref[...] = _rms(y, gn_ref[...]).astype(BF16)


def _cross(x, kvm, g_cross, w_cq, w_co, g_ffn, *, tm):
    b, s, _ = x.shape
    m = kvm.shape[1]
    spec = pl.BlockSpec((None, tm, D_MODEL), lambda bi, i: (bi, i, 0))
    return pl.pallas_call(
        _cross_kernel,
        grid=(b, s // tm),
        in_specs=[
            spec,
            pl.BlockSpec((None, m, 2 * D_MODEL), lambda bi, i: (bi, 0, 0)),
            _const_spec(g_cross.shape),
            _const_spec(w_cq.shape),
            _const_spec(w_co.shape),
            _const_spec(g_ffn.shape),
        ],
        out_specs=[spec, spec],
        out_shape=[jax.ShapeDtypeStruct((b, s, D_MODEL), F32), jax.ShapeDtypeStruct((b, s, D_MODEL), BF16)],
        compiler_params=_params("parallel", "parallel"),
        name="cross",
    )(x, kvm, g_cross, w_cq, w_co, g_ffn)


def _ffn_up_kernel(h_ref, prev_ref, next_ref, wg_ref, wu_ref, cw_ref, cb_ref, o_ref, hext_ref, *, seq):
    i = pl.program_id(0)
    tm = h_ref.shape[0]

    @pl.when(pl.program_id(1) == 0)
    def _():
        first = (i * tm) % seq == 0
        last = ((i + 1) * tm) % seq == 0
        prev = prev_ref[...]
        nxt = next_ref[...]
        hext_ref[0:HALO, :] = jnp.where(first, jnp.zeros_like(prev), prev)
        hext_ref[HALO:HALO + tm, :] = h_ref[...]
        hext_ref[HALO + tm:, :] = jnp.where(last, jnp.zeros_like(nxt), nxt)

    g = _dot(hext_ref[...], wg_ref[...])
    u = _dot(h_ref[...], wu_ref[...])
    rows = g.shape[0]
    g_prev = pltpu.roll(g, 1, axis=0)[HALO:HALO + tm]
    g_next = pltpu.roll(g, rows - 1, axis=0)[HALO:HALO + tm]
    cw = cw_ref[...]
    c = g_prev * cw[0:1] + g[HALO:HALO + tm] * cw[1:2] + g_next * cw[2:3] + cb_ref[...]
    o_ref[...] = (c / (1.0 + jnp.exp(-c)) * u).astype(BF16)


def _ffn_up(h, w_gate, w_up, conv_w, conv_b, *, seq, tm, tf):
    t = h.shape[0]
    hb = tm // HALO
    n_halo = t // HALO
    return pl.pallas_call(
        functools.partial(_ffn_up_kernel, seq=seq),
        grid=(t // tm, D_FF // tf),
        in_specs=[
            pl.BlockSpec((tm, D_MODEL), lambda i, j: (i, 0)),
            pl.BlockSpec((HALO, D_MODEL), lambda i, j: (jnp.maximum(i * hb - 1, 0), 0)),
            pl.BlockSpec((HALO, D_MODEL), lambda i, j: (jnp.minimum((i + 1) * hb, n_halo - 1), 0)),
            pl.BlockSpec((D_MODEL, tf), lambda i, j: (0, j)),
            pl.BlockSpec((D_MODEL, tf), lambda i, j: (0, j)),
            pl.BlockSpec((3, tf), lambda i, j: (0, j)),
            pl.BlockSpec((1, tf), lambda i, j: (0, j)),
        ],
        out_specs=pl.BlockSpec((tm, tf), lambda i, j: (i, j)),
        out_shape=jax.ShapeDtypeStruct((t, D_FF), BF16),
        scratch_shapes=[pltpu.VMEM((tm + 2 * HALO, D_MODEL), BF16)],
        compiler_params=_params("parallel", "arbitrary"),
        name="ffn_up",
    )(h, h, h, w_gate, w_up, conv_w, conv_b)


def _ffn_down_kernel(a_ref, w_ref, x_ref, g_ref, o_ref):
    o_ref[...] = _rms(x_ref[...] + _dot(a_ref[...], w_ref[...]), g_ref[...])


def _ffn_down(act, w_down, x, g_final, *, tm):
    t = x.shape[0]
    return pl.pallas_call(
        _ffn_down_kernel,
        grid=(t // tm,),
        in_specs=[
            pl.BlockSpec((tm, D_FF), lambda i: (i, 0)),
            _const_spec(w_down.shape),
            pl.BlockSpec((tm, D_MODEL), lambda i: (i, 0)),
            _const_spec(g_final.shape),
        ],
        out_specs=pl.BlockSpec((tm, D_MODEL), lambda i: (i, 0)),
        out_shape=jax.ShapeDtypeStruct((t, D_MODEL), F32),
        compiler_params=_params("parallel"),
        name="ffn_down",
    )(act, w_down, x, g_final)


def _dft_tables(seq):
    n1 = DFT_N1
    n2 = seq // n1
    c = np.arange(FOURIER_GROUP_DIM)
    ang = 2.0 * np.pi * ((c[:, None] * c[None, :]) % FOURIER_GROUP_DIM) / FOURIER_GROUP_DIM
    dft_c = np.concatenate([np.cos(ang), np.sin(ang)], axis=1)

    j = np.arange(n1)
    a1 = 2.0 * np.pi * ((j[:, None] * j[None, :]) % n1) / n1
    c1, s1 = np.cos(a1), np.sin(a1)
    m1 = np.block([[c1, -s1], [-s1, -c1]])

    k1 = np.arange(n1)[:, None, None]
    k2 = np.arange(n2)[None, :, None]
    m2 = np.arange(n2)[None, None, :]
    a2 = 2.0 * np.pi * ((m2 * (k1 + n1 * k2)) % seq) / seq
    e = np.concatenate([np.cos(a2), np.sin(a2)], axis=2)
    return (jnp.asarray(dft_c, BF16), jnp.asarray(m1, BF16), jnp.asarray(e, BF16))


def _rope_tables(seq):
    inv = ROPE_THETA ** (-jnp.arange(0, QK_ROPE_DIM, 2, dtype=F32) / QK_ROPE_DIM)
    ang = jnp.arange(seq, dtype=F32)[:, None] * inv[None, :]
    cos, sin = jnp.cos(ang), jnp.sin(ang)
    return (jnp.concatenate([cos, cos, cos, cos], axis=1), jnp.concatenate([-sin, sin, -sin, sin], axis=1),
            jnp.concatenate([cos.T, cos.T], axis=0), jnp.concatenate([-sin.T, sin.T], axis=0))


def _prep_weights(w_in, w_uq, w_ukv):
    half = QK_ROPE_DIM // 2
    c1 = FOURIER_DIM + Q_LORA_RANK + KV_LORA_RANK
    w_in_ext = jnp.concatenate(
        [w_in, w_in[:, c1 + half:c1 + QK_ROPE_DIM], w_in[:, c1:c1 + half]], axis=1).astype(BF16)
    q3 = w_uq.reshape(Q_LORA_RANK, N_HEADS, QK_DIM)
    rope = q3[:, :, QK_NOPE_DIM:]
    rope_sw = jnp.concatenate([rope[:, :, half:], rope[:, :, :half]], axis=2)
    w_uqt_ext = jnp.concatenate(
        [q3[:, :, :QK_NOPE_DIM].reshape(Q_LORA_RANK, -1), rope.reshape(Q_LORA_RANK, -1),
         rope_sw.reshape(Q_LORA_RANK, -1)], axis=1).T.astype(BF16)
    kv3 = w_ukv.reshape(KV_LORA_RANK, N_HEADS, QK_NOPE_DIM + V_HEAD_DIM)
    w_k = kv3[:, :, :QK_NOPE_DIM].reshape(KV_LORA_RANK, -1).astype(BF16)
    w_vt = kv3[:, :, QK_NOPE_DIM:].reshape(KV_LORA_RANK, -1).T.astype(BF16)
    return w_in_ext, w_uqt_ext, w_k, w_vt


def _tile(n, pref):
    return pref if n % pref == 0 else n


def _tiles(s):
    attn_k = _tile(s, 512)
    return dict(
        in_proj=_tile(s, 256),
        attn_q=_tile(s, 1024), attn_k=attn_k, attn_group=4 if (s // attn_k) % 4 == 0 else 2,
        fourier_tn2=8, fourier_kb=32,
        out_proj=_tile(s, 512), mem_kv_cols=1024, cross=_tile(s, 512),
        ffn_up=_tile(s, 1024), ffn_cols=512, ffn_down=_tile(s, 256),
    )


def _trunk(x, mem, p, tables):
    b, s, _ = x.shape
    t = b * s
    n1 = DFT_N1
    n2 = s // n1
    ts = _tiles(s)
    dft_c, m1, e, cos2, sin2, cos_t, sin_t = tables

    ab, qt, k, vt = _in_proj(x, p["g_mix"], p["w_in_ext"], p["g_q"], p["w_uqt_ext"], p["g_kv"], p["w_k"],
                             p["w_vt"], dft_c, cos2, sin2, cos_t, sin_t, tm=ts["in_proj"], tk=ts["attn_k"])

    gc = _fourier_a(ab.reshape(b, N_FOURIER_GROUPS, 2, n1, n2, FOURIER_GROUP_DIM), m1, tn2=ts["fourier_tn2"])
    f = _fourier_b(gc.reshape(b, N_FOURIER_GROUPS, 2, n1, n2, FOURIER_GROUP_DIM), e, kb=ts["fourier_kb"],
                   scale=float((s * FOURIER_GROUP_DIM) ** -0.5))

    a = _attention(qt, k, vt, tq=ts["attn_q"], group_size=ts["attn_group"])

    x1 = _out_proj(x.reshape(t, D_MODEL), f.reshape(t, FOURIER_DIM), a.reshape(t, MLA_DIM),
                   p["g_f"], p["g_a"], p["w_out"], tm=ts["out_proj"])

    kvm = _mem_kv(mem, p["g_mem"], p["w_ckv"], tn=ts["mem_kv_cols"])
    x2, hf = _cross(x1.reshape(b, s, D_MODEL), kvm, p["g_cross"], p["w_cq"], p["w_co"], p["g_ffn"],
                    tm=ts["cross"])

    act = _ffn_up(hf.reshape(t, D_MODEL), p["w_gate"], p["w_up"], p["conv_w"], p["conv_b"], seq=s,
                  tm=ts["ffn_up"], tf=ts["ffn_cols"])
    y = _ffn_down(act, p["w_down"], x2.reshape(t, D_MODEL), p["g_final"], tm=ts["ffn_down"])
    return y.reshape(b, s, D_MODEL)


@jax.jit
def kernel(x_prompt, x_sample, mem_prompt, mem_sample, norm_mix_g, w_in, q_norm_g, w_uq, kv_norm_g, w_ukv,
           fourier_out_g, mla_out_g, w_out, norm_cross_g, norm_mem_g, w_cq, w_ckv, w_co, norm_ffn_g, w_gate,
           w_up, conv_w, conv_b, w_down, final_norm_g):
    assert norm_mix_g.shape[0] == 1, "single-layer trunk"
    assert x_prompt.shape[1] % DFT_N1 == 0 and x_sample.shape[1] % DFT_N1 == 0
    w_in_ext, w_uqt_ext, w_k, w_vt = _prep_weights(w_in[0], w_uq[0], w_ukv[0])
    row = lambda g: g.reshape(1, -1)
    p = dict(
        g_mix=row(norm_mix_g[0]), w_in_ext=w_in_ext, g_q=row(q_norm_g[0]), w_uqt_ext=w_uqt_ext,
        g_kv=row(kv_norm_g[0]), w_k=w_k, w_vt=w_vt, g_f=row(fourier_out_g[0]), g_a=row(mla_out_g[0]),
        w_out=w_out[0].astype(BF16), g_cross=row(norm_cross_g[0]), g_mem=row(norm_mem_g[0]),
        w_cq=w_cq[0].astype(BF16), w_ckv=w_ckv[0].astype(BF16), w_co=w_co[0].astype(BF16),
        g_ffn=row(norm_ffn_g[0]), w_gate=w_gate[0].astype(BF16), w_up=w_up[0].astype(BF16),
        conv_w=conv_w[0], conv_b=row(conv_b[0]), w_down=w_down[0].astype(BF16), g_final=row(final_norm_g),
    )
    outs = []
    for x, mem in ((x_prompt, mem_prompt), (x_sample, mem_sample)):
        s = x.shape[1]
        tables = _dft_tables(s) + _rope_tables(s)
        outs.append(_trunk(x, mem, p, tables))
    return tuple(outs)
```

```python
import functools

import numpy as np
import jax
import jax.numpy as jnp
from jax import lax
from jax.experimental import pallas as pl
from jax.experimental.pallas import tpu as pltpu

F32 = jnp.float32
BF16 = jnp.bfloat16

D_MODEL = 2048
N_FOURIER_GROUPS = 4
FOURIER_GROUP_DIM = 256
FOURIER_DIM = N_FOURIER_GROUPS * FOURIER_GROUP_DIM
N_HEADS = 8
QK_NOPE_DIM = 128
QK_ROPE_DIM = 64
QK_DIM = QK_NOPE_DIM + QK_ROPE_DIM
V_HEAD_DIM = 128
V_EXT_DIM = V_HEAD_DIM + 16
Q_LORA_RANK = 512
KV_LORA_RANK = 512
MLA_DIM = N_HEADS * V_HEAD_DIM
ROPE_THETA = 10000.0
N_CROSS_HEADS = 4
CROSS_HEAD_DIM = D_MODEL // N_CROSS_HEADS
D_FF = 5632
EPS = 1e-6
LOG2_E = 1.4426950408889634

V7X_VMEM_LIMIT_BYTES = 56 * 1024 * 1024
MXU_DIM = 256
DFT_N1 = MXU_DIM // 2
HALO = 16


def _params(*sem):
    return pltpu.CompilerParams(dimension_semantics=sem, vmem_limit_bytes=V7X_VMEM_LIMIT_BYTES)


def _const_spec(shape):
    zeros = (0,) * len(shape)
    return pl.BlockSpec(shape, lambda *_: zeros, pipeline_mode=pl.Buffered(1))


def _rms(x, g):
    return x * lax.rsqrt(jnp.mean(x * x, axis=-1, keepdims=True) + EPS) * g


def _dot(a, b):
    return jnp.dot(a, b, preferred_element_type=F32)


def _dot_nt(a, b):
    return lax.dot_general(a, b, (((1,), (1,)), ((), ())), preferred_element_type=F32)


def _in_proj_kernel(x_ref, g_ref, win_ref, qg_ref, wuqt_ref, kvg_ref, wk_ref, wvt_ref, dft_ref, cos_ref,
                    sin_ref, cost_ref, sint_ref, ab_ref, qt_ref, k_ref, vt_ref):
    h = _rms(x_ref[...], g_ref[...]).astype(BF16)
    z = _dot(h, win_ref[...])

    for g in range(N_FOURIER_GROUPS):
        u = z[:, g * FOURIER_GROUP_DIM:(g + 1) * FOURIER_GROUP_DIM].astype(BF16)
        ab = _dot(u, dft_ref[...])
        ab_ref[g, 0] = ab[:, :FOURIER_GROUP_DIM]
        ab_ref[g, 1] = ab[:, FOURIER_GROUP_DIM:]

    c0 = FOURIER_DIM
    c_q = z[:, c0:c0 + Q_LORA_RANK]
    c_kv = z[:, c0 + Q_LORA_RANK:c0 + Q_LORA_RANK + KV_LORA_RANK]
    c1 = c0 + Q_LORA_RANK + KV_LORA_RANK
    k_r = z[:, c1:c1 + QK_ROPE_DIM]
    k_r_sw = z[:, c1 + QK_ROPE_DIM:c1 + 2 * QK_ROPE_DIM]

    cos = cos_ref[...]
    sin = sin_ref[...]
    k_rope = (k_r * cos[:, :QK_ROPE_DIM] + k_r_sw * sin[:, :QK_ROPE_DIM]).astype(BF16)

    hq = _rms(c_q, qg_ref[...]).astype(BF16)
    qe_t = _dot_nt(wuqt_ref[...], hq)
    n_rope = N_HEADS * QK_ROPE_DIM
    q_nope_w = N_HEADS * QK_NOPE_DIM
    cos_t = jnp.concatenate([cost_ref[...]] * N_HEADS, axis=0)
    sin_t = jnp.concatenate([sint_ref[...]] * N_HEADS, axis=0)
    q_rope_t = qe_t[q_nope_w:q_nope_w + n_rope] * cos_t + qe_t[q_nope_w + n_rope:] * sin_t

    hkv = _rms(c_kv, kvg_ref[...]).astype(BF16)
    k_nope = _dot(hkv, wk_ref[...])
    v_t = _dot_nt(wvt_ref[...], hkv)

    scale = QK_DIM ** -0.5 * LOG2_E
    for hd in range(N_HEADS):
        qt_ref[hd, 0:QK_NOPE_DIM, :] = (qe_t[hd * QK_NOPE_DIM:(hd + 1) * QK_NOPE_DIM] * scale).astype(BF16)
        qt_ref[hd, QK_NOPE_DIM:QK_DIM, :] = (
            q_rope_t[hd * QK_ROPE_DIM:(hd + 1) * QK_ROPE_DIM] * scale).astype(BF16)
        k_ref[hd, :, 0:QK_NOPE_DIM] = k_nope[:, hd * QK_NOPE_DIM:(hd + 1) * QK_NOPE_DIM].astype(BF16)
        k_ref[hd, :, QK_NOPE_DIM:QK_DIM] = k_rope
        vt_ref[hd, 0:V_HEAD_DIM, :] = v_t[hd * V_HEAD_DIM:(hd + 1) * V_HEAD_DIM, :].astype(BF16)
        pad_row = lax.broadcasted_iota(jnp.int32, (V_EXT_DIM - V_HEAD_DIM, v_t.shape[1]), 0)
        vt_ref[hd, V_HEAD_DIM:, :] = jnp.where(pad_row == 0, 1.0, 0.0).astype(BF16)


def _in_proj(x, g_mix, w_in_ext, q_g, w_uqt_ext, kv_g, w_k, w_vt, dft_c, cos2, sin2, cos_t, sin_t, *, tm, tk):
    b, s, _ = x.shape
    grid = (b, s // tm)
    r = tk // tm
    return pl.pallas_call(
        _in_proj_kernel,
        grid=grid,
        in_specs=[
            pl.BlockSpec((None, tm, D_MODEL), lambda bi, i: (bi, i, 0)),
            _const_spec(g_mix.shape),
            _const_spec(w_in_ext.shape),
            _const_spec(q_g.shape),
            _const_spec(w_uqt_ext.shape),
            _const_spec(kv_g.shape),
            _const_spec(w_k.shape),
            _const_spec(w_vt.shape),
            _const_spec(dft_c.shape),
            pl.BlockSpec((tm, 128), lambda bi, i: (i, 0)),
            pl.BlockSpec((tm, 128), lambda bi, i: (i, 0)),
            pl.BlockSpec((QK_ROPE_DIM, tm), lambda bi, i: (0, i)),
            pl.BlockSpec((QK_ROPE_DIM, tm), lambda bi, i: (0, i)),
        ],
        out_specs=[
            pl.BlockSpec((None, N_FOURIER_GROUPS, 2, tm, FOURIER_GROUP_DIM), lambda bi, i: (bi, 0, 0, i, 0)),
            pl.BlockSpec((None, N_HEADS, QK_DIM, tm), lambda bi, i: (bi, 0, 0, i)),
            pl.BlockSpec((None, N_HEADS, tm, QK_DIM), lambda bi, i: (bi, 0, i, 0)),
            pl.BlockSpec((None, N_HEADS, None, V_EXT_DIM, tm), lambda bi, i: (bi, 0, i // r, 0, i % r)),
        ],
        out_shape=[
            jax.ShapeDtypeStruct((b, N_FOURIER_GROUPS, 2, s, FOURIER_GROUP_DIM), F32),
            jax.ShapeDtypeStruct((b, N_HEADS, QK_DIM, s), BF16),
            jax.ShapeDtypeStruct((b, N_HEADS, s, QK_DIM), BF16),
            jax.ShapeDtypeStruct((b, N_HEADS, s // tk, V_EXT_DIM, tk), BF16),
        ],
        compiler_params=_params("parallel", "parallel"),
        name="in_proj",
    )(x, g_mix, w_in_ext, q_g, w_uqt_ext, kv_g, w_k, w_vt, dft_c, cos2, sin2, cos_t, sin_t)


def _fourier_a_kernel(m_ref, x_ref, o_ref):
    two, n1, tn2, c = x_ref.shape
    for j in range(tn2):
        x = jnp.concatenate([x_ref[0, :, j, :], x_ref[1, :, j, :]], axis=0).astype(BF16)
        o_ref[:, :, j * c:(j + 1) * c] = _dot(m_ref[...], x).astype(BF16).reshape(two, n1, c)


def _fourier_a(ab, m1, *, tn2):
    b, g, two, n1, n2, c = ab.shape
    return pl.pallas_call(
        _fourier_a_kernel,
        grid=(b, g, n2 // tn2),
        in_specs=[
            _const_spec(m1.shape),
            pl.BlockSpec((None, None, two, n1, tn2, c), lambda bi, gi, ci: (bi, gi, 0, 0, ci, 0)),
        ],
        out_specs=pl.BlockSpec((None, None, two, n1, tn2 * c), lambda bi, gi, ci: (bi, gi, 0, 0, ci)),
        out_shape=jax.ShapeDtypeStruct((b, g, two, n1, n2 * c), BF16),
        compiler_params=_params("parallel", "parallel", "parallel"),
        name="fourier_a",
    )(m1, ab)


def _fourier_b_kernel(e_ref, g_ref, o_ref, *, scale):
    kb = e_ref.shape[0]
    for j in range(kb):
        x = jnp.concatenate([g_ref[0, j], g_ref[1, j]], axis=0)
        o_ref[:, j, :] = _dot(e_ref[j], x) * scale


def _fourier_b(gc, e, *, kb, scale):
    b, g, two, n1, n2, c = gc.shape
    return pl.pallas_call(
        functools.partial(_fourier_b_kernel, scale=scale),
        grid=(b, g, n1 // kb),
        in_specs=[
            pl.BlockSpec((kb, n2, 2 * n2), lambda bi, gi, ki: (ki, 0, 0)),
            pl.BlockSpec((None, None, two, kb, n2, c), lambda bi, gi, ki: (bi, gi, 0, ki, 0, 0)),
        ],
        out_specs=pl.BlockSpec((None, n2, kb, c), lambda bi, gi, ki: (bi, 0, ki, gi)),
        out_shape=jax.ShapeDtypeStruct((b, n2, n1, g * c), F32),
        compiler_params=_params("parallel", "parallel", "parallel"),
        name="fourier_b",
    )(e, gc)


def _attn_kernel(qt_ref, k_ref, vt_ref, o_ref, sa_ref, sb_ref, acc_ref, *, group_size):
    tk, tq = sa_ref.shape
    nk = vt_ref.shape[0]
    n_tiles = qt_ref.shape[1] // tq
    bufs = (sa_ref, sb_ref)

    def scores(tile, j, s_ref):
        start = pl.multiple_of(j * tk, tk)
        qt = qt_ref[:, tile * tq:(tile + 1) * tq]
        s = _dot(k_ref[pl.ds(start, tk), :], qt)
        s_ref[...] = s
        return jnp.max(s, axis=0, keepdims=True)

    def update(j, s_ref, cmax, m):
        m_new = jnp.maximum(m, cmax)
        p = jnp.exp2(s_ref[...] - m_new)
        acc_ref[...] = jnp.exp2(m - m_new) * acc_ref[...] + _dot(vt_ref[j], p.astype(BF16))
        return m_new

    def group(tile, j, cmax, m, last):
        for t in range(group_size):
            if not last or t + 1 < group_size:
                nxt = scores(tile, j + t + 1, bufs[(t + 1) % 2])
            elif tile + 1 < n_tiles:
                nxt = scores(tile + 1, 0, bufs[0])
            else:
                nxt = None
            m = update(j + t, bufs[t % 2], cmax, m)
            cmax = nxt
        return cmax, m

    acc_ref[...] = jnp.zeros_like(acc_ref)
    cmax = scores(0, 0, sa_ref)
    for tile in range(n_tiles):
        carry = (cmax, jnp.full((1, tq), -jnp.inf, F32))
        carry = lax.fori_loop(0, nk // group_size - 1,
                              lambda i, c, tile=tile: group(tile, group_size * i, *c, False), carry)
        cmax, _ = group(tile, nk - group_size, *carry, True)
        acc = acc_ref[...]
        o_ref[tile * tq:(tile + 1) * tq, :] = (acc[0:V_HEAD_DIM] / acc[V_HEAD_DIM:V_HEAD_DIM + 1]).T


def _attention(qt, k, vt, *, tq, tiles_per_step, group_size):
    b, h, s, _ = k.shape
    nk, _, tk = vt.shape[2:]
    assert group_size % 2 == 0 and nk % group_size == 0
    rows = tq * tiles_per_step
    return pl.pallas_call(
        functools.partial(_attn_kernel, group_size=group_size),
        grid=(b, h, s // rows),
        in_specs=[
            pl.BlockSpec((None, None, QK_DIM, rows), lambda bi, hi, i: (bi, hi, 0, i)),
            pl.BlockSpec((None, None, s, QK_DIM), lambda bi, hi, i: (bi, hi, 0, 0)),
            pl.BlockSpec((None, None, nk, V_EXT_DIM, tk), lambda bi, hi, i: (bi, hi, 0, 0, 0)),
        ],
        out_specs=pl.BlockSpec((None, rows, V_HEAD_DIM), lambda bi, hi, i: (bi, i, hi)),
        out_shape=jax.ShapeDtypeStruct((b, s, h * V_HEAD_DIM), F32),
        scratch_shapes=[pltpu.VMEM((tk, tq), F32), pltpu.VMEM((tk, tq), F32), pltpu.VMEM((V_EXT_DIM, tq), F32)],
        compiler_params=_params("parallel", "parallel", "parallel"),
        name="attention",
    )(qt, k, vt)


def _out_proj_kernel(x_ref, f_ref, a_ref, gf_ref, ga_ref, w_ref, o_ref):
    hf = _rms(f_ref[...], gf_ref[...]).astype(BF16)
    ha = _rms(a_ref[...], ga_ref[...]).astype(BF16)
    o_ref[...] = x_ref[...] + _dot(hf, w_ref[0:FOURIER_DIM, :]) + _dot(ha, w_ref[FOURIER_DIM:, :])


def _out_proj(x, f, a, g_f, g_a, w_out, *, tm):
    t = x.shape[0]
    return pl.pallas_call(
        _out_proj_kernel,
        grid=(t // tm,),
        in_specs=[
            pl.BlockSpec((tm, D_MODEL), lambda i: (i, 0)),
            pl.BlockSpec((tm, FOURIER_DIM), lambda i: (i, 0)),
            pl.BlockSpec((tm, MLA_DIM), lambda i: (i, 0)),
            _const_spec(g_f.shape),
            _const_spec(g_a.shape),
            _const_spec(w_out.shape),
        ],
        out_specs=pl.BlockSpec((tm, D_MODEL), lambda i: (i, 0)),
        out_shape=jax.ShapeDtypeStruct((t, D_MODEL), F32),
        compiler_params=_params("parallel"),
        name="out_proj",
    )(x, f, a, g_f, g_a, w_out)


def _mem_kv_kernel(m_ref, g_ref, w_ref, o_ref):
    o_ref[...] = _dot(_rms(m_ref[...], g_ref[...]).astype(BF16), w_ref[...]).astype(BF16)


def _mem_kv(mem, g_mem, w_ckv, *, tn):
    b, m, _ = mem.shape
    n = w_ckv.shape[1]
    return pl.pallas_call(
        _mem_kv_kernel,
        grid=(b, n // tn),
        in_specs=[
            pl.BlockSpec((None, m, D_MODEL), lambda bi, j: (bi, 0, 0)),
            _const_spec(g_mem.shape),
            pl.BlockSpec((D_MODEL, tn), lambda bi, j: (0, j)),
        ],
        out_specs=pl.BlockSpec((None, m, tn), lambda bi, j: (bi, 0, j)),
        out_shape=jax.ShapeDtypeStruct((b, m, n), BF16),
        compiler_params=_params("parallel", "parallel"),
        name="mem_kv",
    )(mem, g_mem, w_ckv)


def _cross_kernel(x_ref, kv_ref, gc_ref, wq_ref, wo_ref, gn_ref, o_ref, h_ref):
    x = x_ref[...]
    h = _rms(x, gc_ref[...]).astype(BF16)
    q = _dot(h, wq_ref[...]) * (CROSS_HEAD_DIM ** -0.5)
    heads = []
    for hd in range(N_CROSS_HEADS):
        lo = hd * CROSS_HEAD_DIM
        s = _dot_nt(q[:, lo:lo + CROSS_HEAD_DIM].astype(BF16), kv_ref[:, lo:lo + CROSS_HEAD_DIM])
        p = jnp.exp(s - jnp.max(s, axis=-1, keepdims=True))
        p = p / jnp.sum(p, axis=-1, keepdims=True)
        heads.append(_dot(p.astype(BF16), kv_ref[:, D_MODEL + lo:D_MODEL + lo + CROSS_HEAD_DIM]).astype(BF16))
    y = x + _dot(jnp.concatenate(heads, axis=1), wo_ref[...])
    o_ref[...] = y
    h_ref[...] = _rms(y, gn_ref[...]).astype(BF16)


def _cross(x, kvm, g_cross, w_cq, w_co, g_ffn, *, tm):
    b, s, _ = x.shape
    m = kvm.shape[1]
    spec = pl.BlockSpec((None, tm, D_MODEL), lambda bi, i: (bi, i, 0))
    return pl.pallas_call(
        _cross_kernel,
        grid=(b, s // tm),
        in_specs=[
            spec,
            pl.BlockSpec((None, m, 2 * D_MODEL), lambda bi, i: (bi, 0, 0)),
            _const_spec(g_cross.shape),
            _const_spec(w_cq.shape),
            _const_spec(w_co.shape),
            _const_spec(g_ffn.shape),
        ],
        out_specs=[spec, spec],
        out_shape=[jax.ShapeDtypeStruct((b, s, D_MODEL), F32), jax.ShapeDtypeStruct((b, s, D_MODEL), BF16)],
        compiler_params=_params("parallel", "parallel"),
        name="cross",
    )(x, kvm, g_cross, w_cq, w_co, g_ffn)


def _ffn_up_kernel(h_ref, prev_ref, next_ref, wg_ref, wu_ref, cw_ref, cb_ref, o_ref, hext_ref, *, seq):
    i = pl.program_id(0)
    tm = h_ref.shape[0]

    @pl.when(pl.program_id(1) == 0)
    def _():
        first = (i * tm) % seq == 0
        last = ((i + 1) * tm) % seq == 0
        prev = prev_ref[...]
        nxt = next_ref[...]
        hext_ref[0:HALO, :] = jnp.where(first, jnp.zeros_like(prev), prev)
        hext_ref[HALO:HALO + tm, :] = h_ref[...]
        hext_ref[HALO + tm:, :] = jnp.where(last, jnp.zeros_like(nxt), nxt)

    g = _dot(hext_ref[...], wg_ref[...])
    u = _dot(h_ref[...], wu_ref[...])
    rows = g.shape[0]
    g_prev = pltpu.roll(g, 1, axis=0)[HALO:HALO + tm]
    g_next = pltpu.roll(g, rows - 1, axis=0)[HALO:HALO + tm]
    cw = cw_ref[...]
    c = g_prev * cw[0:1] + g[HALO:HALO + tm] * cw[1:2] + g_next * cw[2:3] + cb_ref[...]
    o_ref[...] = (c / (1.0 + jnp.exp(-c)) * u).astype(BF16)


def _ffn_up(h, w_gate, w_up, conv_w, conv_b, *, seq, tm, tf):
    t = h.shape[0]
    hb = tm // HALO
    n_halo = t // HALO
    return pl.pallas_call(
        functools.partial(_ffn_up_kernel, seq=seq),
        grid=(t // tm, D_FF // tf),
        in_specs=[
            pl.BlockSpec((tm, D_MODEL), lambda i, j: (i, 0)),
            pl.BlockSpec((HALO, D_MODEL), lambda i, j: (jnp.maximum(i * hb - 1, 0), 0)),
            pl.BlockSpec((HALO, D_MODEL), lambda i, j: (jnp.minimum((i + 1) * hb, n_halo - 1), 0)),
            pl.BlockSpec((D_MODEL, tf), lambda i, j: (0, j)),
            pl.BlockSpec((D_MODEL, tf), lambda i, j: (0, j)),
            pl.BlockSpec((3, tf), lambda i, j: (0, j)),
            pl.BlockSpec((1, tf), lambda i, j: (0, j)),
        ],
        out_specs=pl.BlockSpec((tm, tf), lambda i, j: (i, j)),
        out_shape=jax.ShapeDtypeStruct((t, D_FF), BF16),
        scratch_shapes=[pltpu.VMEM((tm + 2 * HALO, D_MODEL), BF16)],
        compiler_params=_params("parallel", "arbitrary"),
        name="ffn_up",
    )(h, h, h, w_gate, w_up, conv_w, conv_b)


def _ffn_down_kernel(a_ref, w_ref, x_ref, g_ref, o_ref):
    o_ref[...] = _rms(x_ref[...] + _dot(a_ref[...], w_ref[...]), g_ref[...])


def _ffn_down(act, w_down, x, g_final, *, tm):
    t = x.shape[0]
    return pl.pallas_call(
        _ffn_down_kernel,
        grid=(t // tm,),
        in_specs=[
            pl.BlockSpec((tm, D_FF), lambda i: (i, 0)),
            _const_spec(w_down.shape),
            pl.BlockSpec((tm, D_MODEL), lambda i: (i, 0)),
            _const_spec(g_final.shape),
        ],
        out_specs=pl.BlockSpec((tm, D_MODEL), lambda i: (i, 0)),
        out_shape=jax.ShapeDtypeStruct((t, D_MODEL), F32),
        compiler_params=_params("parallel"),
        name="ffn_down",
    )(act, w_down, x, g_final)


def _dft_tables(seq):
    n1 = DFT_N1
    n2 = seq // n1
    c = np.arange(FOURIER_GROUP_DIM)
    ang = 2.0 * np.pi * ((c[:, None] * c[None, :]) % FOURIER_GROUP_DIM) / FOURIER_GROUP_DIM
    dft_c = np.concatenate([np.cos(ang), np.sin(ang)], axis=1)

    j = np.arange(n1)
    a1 = 2.0 * np.pi * ((j[:, None] * j[None, :]) % n1) / n1
    c1, s1 = np.cos(a1), np.sin(a1)
    m1 = np.block([[c1, -s1], [-s1, -c1]])

    k1 = np.arange(n1)[:, None, None]
    k2 = np.arange(n2)[None, :, None]
    m2 = np.arange(n2)[None, None, :]
    a2 = 2.0 * np.pi * ((m2 * (k1 + n1 * k2)) % seq) / seq
    e = np.concatenate([np.cos(a2), np.sin(a2)], axis=2)
    return (jnp.asarray(dft_c, BF16), jnp.asarray(m1, BF16), jnp.asarray(e, BF16))


def _rope_tables(seq):
    inv = ROPE_THETA ** (-jnp.arange(0, QK_ROPE_DIM, 2, dtype=F32) / QK_ROPE_DIM)
    ang = jnp.arange(seq, dtype=F32)[:, None] * inv[None, :]
    cos, sin = jnp.cos(ang), jnp.sin(ang)
    return (jnp.concatenate([cos, cos, cos, cos], axis=1), jnp.concatenate([-sin, sin, -sin, sin], axis=1),
            jnp.concatenate([cos.T, cos.T], axis=0), jnp.concatenate([-sin.T, sin.T], axis=0))


def _prep_weights(w_in, w_uq, w_ukv):
    half = QK_ROPE_DIM // 2
    c1 = FOURIER_DIM + Q_LORA_RANK + KV_LORA_RANK
    w_in_ext = jnp.concatenate(
        [w_in, w_in[:, c1 + half:c1 + QK_ROPE_DIM], w_in[:, c1:c1 + half]], axis=1).astype(BF16)
    q3 = w_uq.reshape(Q_LORA_RANK, N_HEADS, QK_DIM)
    rope = q3[:, :, QK_NOPE_DIM:]
    rope_sw = jnp.concatenate([rope[:, :, half:], rope[:, :, :half]], axis=2)
    w_uqt_ext = jnp.concatenate(
        [q3[:, :, :QK_NOPE_DIM].reshape(Q_LORA_RANK, -1), rope.reshape(Q_LORA_RANK, -1),
         rope_sw.reshape(Q_LORA_RANK, -1)], axis=1).T.astype(BF16)
    kv3 = w_ukv.reshape(KV_LORA_RANK, N_HEADS, QK_NOPE_DIM + V_HEAD_DIM)
    w_k = kv3[:, :, :QK_NOPE_DIM].reshape(KV_LORA_RANK, -1).astype(BF16)
    w_vt = kv3[:, :, QK_NOPE_DIM:].reshape(KV_LORA_RANK, -1).T.astype(BF16)
    return w_in_ext, w_uqt_ext, w_k, w_vt


def _tile(n, pref):
    return pref if n % pref == 0 else n


def _tiles(s):
    attn_k = _tile(s, 512)
    return dict(
        in_proj=_tile(s, 512),
        attn_q=_tile(s, 1024), attn_tiles=4 if s % 4096 == 0 else 1,
        attn_k=attn_k, attn_group=4 if (s // attn_k) % 4 == 0 else 2,
        fourier_tn2=min(16, s // DFT_N1), fourier_kb=32,
        out_proj=_tile(s, 512), mem_kv_cols=1024, cross=_tile(s, 512),
        ffn_up=_tile(s, 1024), ffn_cols=512, ffn_down=_tile(s, 256),
    )


def _trunk(x, mem, p, tables):
    b, s, _ = x.shape
    t = b * s
    n1 = DFT_N1
    n2 = s // n1
    ts = _tiles(s)
    dft_c, m1, e, cos2, sin2, cos_t, sin_t = tables

    ab, qt, k, vt = _in_proj(x, p["g_mix"], p["w_in_ext"], p["g_q"], p["w_uqt_ext"], p["g_kv"], p["w_k"],
                             p["w_vt"], dft_c, cos2, sin2, cos_t, sin_t, tm=ts["in_proj"], tk=ts["attn_k"])

    gc = _fourier_a(ab.reshape(b, N_FOURIER_GROUPS, 2, n1, n2, FOURIER_GROUP_DIM), m1, tn2=ts["fourier_tn2"])
    f = _fourier_b(gc.reshape(b, N_FOURIER_GROUPS, 2, n1, n2, FOURIER_GROUP_DIM), e, kb=ts["fourier_kb"],
                   scale=float((s * FOURIER_GROUP_DIM) ** -0.5))

    a = _attention(qt, k, vt, tq=ts["attn_q"], tiles_per_step=ts["attn_tiles"], group_size=ts["attn_group"])

    x1 = _out_proj(x.reshape(t, D_MODEL), f.reshape(t, FOURIER_DIM), a.reshape(t, MLA_DIM),
                   p["g_f"], p["g_a"], p["w_out"], tm=ts["out_proj"])

    kvm = _mem_kv(mem, p["g_mem"], p["w_ckv"], tn=ts["mem_kv_cols"])
    x2, hf = _cross(x1.reshape(b, s, D_MODEL), kvm, p["g_cross"], p["w_cq"], p["w_co"], p["g_ffn"],
                    tm=ts["cross"])

    act = _ffn_up(hf.reshape(t, D_MODEL), p["w_gate"], p["w_up"], p["conv_w"], p["conv_b"], seq=s,
                  tm=ts["ffn_up"], tf=ts["ffn_cols"])
    y = _ffn_down(act, p["w_down"], x2.reshape(t, D_MODEL), p["g_final"], tm=ts["ffn_down"])
    return y.reshape(b, s, D_MODEL)


@jax.jit
def kernel(x_prompt, x_sample, mem_prompt, mem_sample, norm_mix_g, w_in, q_norm_g, w_uq, kv_norm_g, w_ukv,
           fourier_out_g, mla_out_g, w_out, norm_cross_g, norm_mem_g, w_cq, w_ckv, w_co, norm_ffn_g, w_gate,
           w_up, conv_w, conv_b, w_down, final_norm_g):
    assert norm_mix_g.shape[0] == 1, "single-layer trunk"
    assert x_prompt.shape[1] % DFT_N1 == 0 and x_sample.shape[1] % DFT_N1 == 0
    w_in_ext, w_uqt_ext, w_k, w_vt = _prep_weights(w_in[0], w_uq[0], w_ukv[0])
    row = lambda g: g.reshape(1, -1)
    p = dict(
        g_mix=row(norm_mix_g[0]), w_in_ext=w_in_ext, g_q=row(q_norm_g[0]), w_uqt_ext=w_uqt_ext,
        g_kv=row(kv_norm_g[0]), w_k=w_k, w_vt=w_vt, g_f=row(fourier_out_g[0]), g_a=row(mla_out_g[0]),
        w_out=w_out[0].astype(BF16), g_cross=row(norm_cross_g[0]), g_mem=row(norm_mem_g[0]),
        w_cq=w_cq[0].astype(BF16), w_ckv=w_ckv[0].astype(BF16), w_co=w_co[0].astype(BF16),
        g_ffn=row(norm_ffn_g[0]), w_gate=w_gate[0].astype(BF16), w_up=w_up[0].astype(BF16),
        conv_w=conv_w[0], conv_b=row(conv_b[0]), w_down=w_down[0].astype(BF16), g_final=row(final_norm_g),
    )
    outs = []
    for x, mem in ((x_prompt, mem_prompt), (x_sample, mem_sample)):
        s = x.shape[1]
        tables = _dft_tables(s) + _rope_tables(s)
        outs.append(_trunk(x, mem, p, tables))
    return tuple(outs)
```

```python
import functools

import numpy as np
import jax
import jax.numpy as jnp
from jax import lax
from jax.experimental import pallas as pl
from jax.experimental.pallas import tpu as pltpu

F32 = jnp.float32
BF16 = jnp.bfloat16

D_MODEL = 2048
N_FOURIER_GROUPS = 4
FOURIER_GROUP_DIM = 256
FOURIER_DIM = N_FOURIER_GROUPS * FOURIER_GROUP_DIM
N_HEADS = 8
QK_NOPE_DIM = 128
QK_ROPE_DIM = 64
QK_DIM = QK_NOPE_DIM + QK_ROPE_DIM
V_HEAD_DIM = 128
V_EXT_DIM = V_HEAD_DIM + 16
Q_LORA_RANK = 512
KV_LORA_RANK = 512
MLA_DIM = N_HEADS * V_HEAD_DIM
ROPE_THETA = 10000.0
N_CROSS_HEADS = 4
CROSS_HEAD_DIM = D_MODEL // N_CROSS_HEADS
D_FF = 5632
EPS = 1e-6
LOG2_E = 1.4426950408889634

V7X_VMEM_LIMIT_BYTES = 56 * 1024 * 1024
MXU_DIM = 256
DFT_N1 = MXU_DIM // 2
HALO = 16


def _params(*sem):
    return pltpu.CompilerParams(dimension_semantics=sem, vmem_limit_bytes=V7X_VMEM_LIMIT_BYTES)


def _const_spec(shape):
    zeros = (0,) * len(shape)
    return pl.BlockSpec(shape, lambda *_: zeros, pipeline_mode=pl.Buffered(1))


def _rms(x, g):
    return x * lax.rsqrt(jnp.mean(x * x, axis=-1, keepdims=True) + EPS) * g


def _dot(a, b):
    return jnp.dot(a, b, preferred_element_type=F32)


def _dot_nt(a, b):
    return lax.dot_general(a, b, (((1,), (1,)), ((), ())), preferred_element_type=F32)


def _in_proj_kernel(x_ref, g_ref, win_ref, qg_ref, wuqt_ref, kvg_ref, wk_ref, wvt_ref, dft_ref, cos_ref,
                    sin_ref, cost_ref, sint_ref, ab_ref, qt_ref, k_ref, vt_ref):
    h = _rms(x_ref[...], g_ref[...]).astype(BF16)
    z = _dot(h, win_ref[...])

    for g in range(N_FOURIER_GROUPS):
        u = z[:, g * FOURIER_GROUP_DIM:(g + 1) * FOURIER_GROUP_DIM].astype(BF16)
        ab = _dot(u, dft_ref[...])
        ab_ref[g, 0] = ab[:, :FOURIER_GROUP_DIM]
        ab_ref[g, 1] = ab[:, FOURIER_GROUP_DIM:]

    c0 = FOURIER_DIM
    c_q = z[:, c0:c0 + Q_LORA_RANK]
    c_kv = z[:, c0 + Q_LORA_RANK:c0 + Q_LORA_RANK + KV_LORA_RANK]
    c1 = c0 + Q_LORA_RANK + KV_LORA_RANK
    k_r = z[:, c1:c1 + QK_ROPE_DIM]
    k_r_sw = z[:, c1 + QK_ROPE_DIM:c1 + 2 * QK_ROPE_DIM]

    cos = cos_ref[...]
    sin = sin_ref[...]
    k_rope = (k_r * cos[:, :QK_ROPE_DIM] + k_r_sw * sin[:, :QK_ROPE_DIM]).astype(BF16)

    hq = _rms(c_q, qg_ref[...]).astype(BF16)
    qe_t = _dot_nt(wuqt_ref[...], hq)
    q_nope_w = N_HEADS * QK_NOPE_DIM
    cos_t = jnp.concatenate([cost_ref[...]] * N_HEADS, axis=0)
    sin_t = jnp.concatenate([sint_ref[...]] * N_HEADS, axis=0)
    q_r = qe_t[q_nope_w:]
    half = QK_ROPE_DIM // 2
    q_sw = jnp.concatenate(
        [q_r[lo:lo + half] for hd in range(N_HEADS) for lo in (hd * QK_ROPE_DIM + half, hd * QK_ROPE_DIM)], axis=0)
    q_rope_t = q_r * cos_t + q_sw * sin_t

    hkv = _rms(c_kv, kvg_ref[...]).astype(BF16)
    k_nope = _dot(hkv, wk_ref[...])
    v_t = _dot_nt(wvt_ref[...], hkv)

    scale = QK_DIM ** -0.5 * LOG2_E
    for hd in range(N_HEADS):
        qt_ref[hd, 0:QK_NOPE_DIM, :] = (qe_t[hd * QK_NOPE_DIM:(hd + 1) * QK_NOPE_DIM] * scale).astype(BF16)
        qt_ref[hd, QK_NOPE_DIM:QK_DIM, :] = (
            q_rope_t[hd * QK_ROPE_DIM:(hd + 1) * QK_ROPE_DIM] * scale).astype(BF16)
        k_ref[hd, :, 0:QK_NOPE_DIM] = k_nope[:, hd * QK_NOPE_DIM:(hd + 1) * QK_NOPE_DIM].astype(BF16)
        k_ref[hd, :, QK_NOPE_DIM:QK_DIM] = k_rope
        vt_ref[hd, 0:V_HEAD_DIM, :] = v_t[hd * V_HEAD_DIM:(hd + 1) * V_HEAD_DIM, :].astype(BF16)
        pad_row = lax.broadcasted_iota(jnp.int32, (V_EXT_DIM - V_HEAD_DIM, v_t.shape[1]), 0)
        vt_ref[hd, V_HEAD_DIM:, :] = jnp.where(pad_row == 0, 1.0, 0.0).astype(BF16)


def _in_proj(x, g_mix, w_in_ext, q_g, w_uqt_ext, kv_g, w_k, w_vt, dft_c, cos2, sin2, cos_t, sin_t, *, tm, tk):
    b, s, _ = x.shape
    grid = (b, s // tm)
    r = tk // tm
    return pl.pallas_call(
        _in_proj_kernel,
        grid=grid,
        in_specs=[
            pl.BlockSpec((None, tm, D_MODEL), lambda bi, i: (bi, i, 0)),
            _const_spec(g_mix.shape),
            _const_spec(w_in_ext.shape),
            _const_spec(q_g.shape),
            _const_spec(w_uqt_ext.shape),
            _const_spec(kv_g.shape),
            _const_spec(w_k.shape),
            _const_spec(w_vt.shape),
            _const_spec(dft_c.shape),
            pl.BlockSpec((tm, 128), lambda bi, i: (i, 0)),
            pl.BlockSpec((tm, 128), lambda bi, i: (i, 0)),
            pl.BlockSpec((QK_ROPE_DIM, tm), lambda bi, i: (0, i)),
            pl.BlockSpec((QK_ROPE_DIM, tm), lambda bi, i: (0, i)),
        ],
        out_specs=[
            pl.BlockSpec((None, N_FOURIER_GROUPS, 2, tm, FOURIER_GROUP_DIM), lambda bi, i: (bi, 0, 0, i, 0)),
            pl.BlockSpec((None, N_HEADS, QK_DIM, tm), lambda bi, i: (bi, 0, 0, i)),
            pl.BlockSpec((None, N_HEADS, tm, QK_DIM), lambda bi, i: (bi, 0, i, 0)),
            pl.BlockSpec((None, N_HEADS, None, V_EXT_DIM, tm), lambda bi, i: (bi, 0, i // r, 0, i % r)),
        ],
        out_shape=[
            jax.ShapeDtypeStruct((b, N_FOURIER_GROUPS, 2, s, FOURIER_GROUP_DIM), F32),
            jax.ShapeDtypeStruct((b, N_HEADS, QK_DIM, s), BF16),
            jax.ShapeDtypeStruct((b, N_HEADS, s, QK_DIM), BF16),
            jax.ShapeDtypeStruct((b, N_HEADS, s // tk, V_EXT_DIM, tk), BF16),
        ],
        compiler_params=_params("parallel", "parallel"),
        name="in_proj",
    )(x, g_mix, w_in_ext, q_g, w_uqt_ext, kv_g, w_k, w_vt, dft_c, cos2, sin2, cos_t, sin_t)


def _fourier_a_kernel(m_ref, x_ref, o_ref):
    two, n1, tn2, c = x_ref.shape
    for j in range(tn2):
        x = jnp.concatenate([x_ref[0, :, j, :], x_ref[1, :, j, :]], axis=0).astype(BF16)
        o_ref[:, :, j * c:(j + 1) * c] = _dot(m_ref[...], x).astype(BF16).reshape(two, n1, c)


def _fourier_a(ab, m1, *, tn2):
    b, g, two, n1, n2, c = ab.shape
    return pl.pallas_call(
        _fourier_a_kernel,
        grid=(b, g, n2 // tn2),
        in_specs=[
            _const_spec(m1.shape),
            pl.BlockSpec((None, None, two, n1, tn2, c), lambda bi, gi, ci: (bi, gi, 0, 0, ci, 0)),
        ],
        out_specs=pl.BlockSpec((None, None, two, n1, tn2 * c), lambda bi, gi, ci: (bi, gi, 0, 0, ci)),
        out_shape=jax.ShapeDtypeStruct((b, g, two, n1, n2 * c), BF16),
        compiler_params=_params("parallel", "parallel", "parallel"),
        name="fourier_a",
    )(m1, ab)


def _fourier_b_kernel(e_ref, g_ref, o_ref, *, scale):
    kb = e_ref.shape[0]
    for j in range(kb):
        x = jnp.concatenate([g_ref[0, j], g_ref[1, j]], axis=0)
        o_ref[:, j, :] = _dot(e_ref[j], x) * scale


def _fourier_b(gc, e, *, kb, scale):
    b, g, two, n1, n2, c = gc.shape
    return pl.pallas_call(
        functools.partial(_fourier_b_kernel, scale=scale),
        grid=(b, g, n1 // kb),
        in_specs=[
            pl.BlockSpec((kb, n2, 2 * n2), lambda bi, gi, ki: (ki, 0, 0)),
            pl.BlockSpec((None, None, two, kb, n2, c), lambda bi, gi, ki: (bi, gi, 0, ki, 0, 0)),
        ],
        out_specs=pl.BlockSpec((None, n2, kb, c), lambda bi, gi, ki: (bi, 0, ki, gi)),
        out_shape=jax.ShapeDtypeStruct((b, n2, n1, g * c), F32),
        compiler_params=_params("parallel", "parallel", "parallel"),
        name="fourier_b",
    )(e, gc)


def _attn_kernel(qt_ref, k_ref, vt_ref, o_ref, sa_ref, sb_ref, acc_ref, *, group_size):
    tk, tq = sa_ref.shape
    nk = vt_ref.shape[0]
    n_tiles = qt_ref.shape[1] // tq
    bufs = (sa_ref, sb_ref)

    def scores(tile, j, s_ref):
        start = pl.multiple_of(j * tk, tk)
        qt = qt_ref[:, tile * tq:(tile + 1) * tq]
        s = _dot(k_ref[pl.ds(start, tk), :], qt)
        s_ref[...] = s
        return jnp.max(s, axis=0, keepdims=True)

    def update(j, s_ref, cmax, m):
        m_new = jnp.maximum(m, cmax)
        p = jnp.exp2(s_ref[...] - m_new)
        acc_ref[...] = jnp.exp2(m - m_new) * acc_ref[...] + _dot(vt_ref[j], p.astype(BF16))
        return m_new

    def group(tile, j, cmax, m, last):
        for t in range(group_size):
            if not last or t + 1 < group_size:
                nxt = scores(tile, j + t + 1, bufs[(t + 1) % 2])
            elif tile + 1 < n_tiles:
                nxt = scores(tile + 1, 0, bufs[0])
            else:
                nxt = None
            m = update(j + t, bufs[t % 2], cmax, m)
            cmax = nxt
        return cmax, m

    acc_ref[...] = jnp.zeros_like(acc_ref)
    cmax = scores(0, 0, sa_ref)
    for tile in range(n_tiles):
        carry = (cmax, jnp.full((1, tq), -jnp.inf, F32))
        carry = lax.fori_loop(0, nk // group_size - 1,
                              lambda i, c, tile=tile: group(tile, group_size * i, *c, False), carry)
        cmax, _ = group(tile, nk - group_size, *carry, True)
        acc = acc_ref[...]
        o_ref[tile * tq:(tile + 1) * tq, :] = (acc[0:V_HEAD_DIM] / acc[V_HEAD_DIM:V_HEAD_DIM + 1]).T


def _attention(qt, k, vt, *, tq, tiles_per_step, group_size):
    b, h, s, _ = k.shape
    nk, _, tk = vt.shape[2:]
    assert group_size % 2 == 0 and nk % group_size == 0
    rows = tq * tiles_per_step
    return pl.pallas_call(
        functools.partial(_attn_kernel, group_size=group_size),
        grid=(b, h, s // rows),
        in_specs=[
            pl.BlockSpec((None, None, QK_DIM, rows), lambda bi, hi, i: (bi, hi, 0, i)),
            pl.BlockSpec((None, None, s, QK_DIM), lambda bi, hi, i: (bi, hi, 0, 0)),
            pl.BlockSpec((None, None, nk, V_EXT_DIM, tk), lambda bi, hi, i: (bi, hi, 0, 0, 0)),
        ],
        out_specs=pl.BlockSpec((None, rows, V_HEAD_DIM), lambda bi, hi, i: (bi, i, hi)),
        out_shape=jax.ShapeDtypeStruct((b, s, h * V_HEAD_DIM), F32),
        scratch_shapes=[pltpu.VMEM((tk, tq), F32), pltpu.VMEM((tk, tq), F32), pltpu.VMEM((V_EXT_DIM, tq), F32)],
        compiler_params=_params("parallel", "parallel", "parallel"),
        name="attention",
    )(qt, k, vt)


def _out_proj_kernel(x_ref, f_ref, a_ref, gf_ref, ga_ref, w_ref, o_ref):
    hf = _rms(f_ref[...], gf_ref[...]).astype(BF16)
    ha = _rms(a_ref[...], ga_ref[...]).astype(BF16)
    o_ref[...] = x_ref[...] + _dot(hf, w_ref[0:FOURIER_DIM, :]) + _dot(ha, w_ref[FOURIER_DIM:, :])


def _out_proj(x, f, a, g_f, g_a, w_out, *, tm):
    t = x.shape[0]
    return pl.pallas_call(
        _out_proj_kernel,
        grid=(t // tm,),
        in_specs=[
            pl.BlockSpec((tm, D_MODEL), lambda i: (i, 0)),
            pl.BlockSpec((tm, FOURIER_DIM), lambda i: (i, 0)),
            pl.BlockSpec((tm, MLA_DIM), lambda i: (i, 0)),
            _const_spec(g_f.shape),
            _const_spec(g_a.shape),
            _const_spec(w_out.shape),
        ],
        out_specs=pl.BlockSpec((tm, D_MODEL), lambda i: (i, 0)),
        out_shape=jax.ShapeDtypeStruct((t, D_MODEL), F32),
        compiler_params=_params("parallel"),
        name="out_proj",
    )(x, f, a, g_f, g_a, w_out)


def _mem_kv_kernel(m_ref, g_ref, w_ref, o_ref):
    o_ref[...] = _dot(_rms(m_ref[...], g_ref[...]).astype(BF16), w_ref[...].astype(BF16)).astype(BF16)


def _mem_kv(mem, g_mem, w_ckv, *, tn):
    b, m, _ = mem.shape
    n = w_ckv.shape[1]
    return pl.pallas_call(
        _mem_kv_kernel,
        grid=(n // tn, b),
        in_specs=[
            pl.BlockSpec((None, m, D_MODEL), lambda j, bi: (bi, 0, 0)),
            _const_spec(g_mem.shape),
            pl.BlockSpec((D_MODEL, tn), lambda j, bi: (0, j)),
        ],
        out_specs=pl.BlockSpec((None, m, tn), lambda j, bi: (bi, 0, j)),
        out_shape=jax.ShapeDtypeStruct((b, m, n), BF16),
        compiler_params=_params("parallel", "parallel"),
        name="mem_kv",
    )(mem, g_mem, w_ckv)


def _cross_kernel(x_ref, kv_ref, gc_ref, wq_ref, wo_ref, gn_ref, o_ref, h_ref):
    x = x_ref[...]
    h = _rms(x, gc_ref[...]).astype(BF16)
    q = _dot(h, wq_ref[...]) * (CROSS_HEAD_DIM ** -0.5)
    heads = []
    for hd in range(N_CROSS_HEADS):
        lo = hd * CROSS_HEAD_DIM
        s = _dot_nt(q[:, lo:lo + CROSS_HEAD_DIM].astype(BF16), kv_ref[:, lo:lo + CROSS_HEAD_DIM])
        p = jnp.exp(s - jnp.max(s, axis=-1, keepdims=True))
        p = p / jnp.sum(p, axis=-1, keepdims=True)
        heads.append(_dot(p.astype(BF16), kv_ref[:, D_MODEL + lo:D_MODEL + lo + CROSS_HEAD_DIM]).astype(BF16))
    y = x + _dot(jnp.concatenate(heads, axis=1), wo_ref[...])
    o_ref[...] = y
    h_ref[...] = _rms(y, gn_ref[...]).astype(BF16)


def _cross(x, kvm, g_cross, w_cq, w_co, g_ffn, *, tm):
    b, s, _ = x.shape
    m = kvm.shape[1]
    spec = pl.BlockSpec((None, tm, D_MODEL), lambda bi, i: (bi, i, 0))
    return pl.pallas_call(
        _cross_kernel,
        grid=(b, s // tm),
        in_specs=[
            spec,
            pl.BlockSpec((None, m, 2 * D_MODEL), lambda bi, i: (bi, 0, 0)),
            _const_spec(g_cross.shape),
            _const_spec(w_cq.shape),
            _const_spec(w_co.shape),
            _const_spec(g_ffn.shape),
        ],
        out_specs=[spec, spec],
        out_shape=[jax.ShapeDtypeStruct((b, s, D_MODEL), F32), jax.ShapeDtypeStruct((b, s, D_MODEL), BF16)],
        compiler_params=_params("parallel", "parallel"),
        name="cross",
    )(x, kvm, g_cross, w_cq, w_co, g_ffn)


def _ffn_up_kernel(h_ref, prev_ref, next_ref, wg_ref, wu_ref, cw_ref, cb_ref, o_ref, hext_ref, *, seq):
    i = pl.program_id(0)
    tm = h_ref.shape[0]

    @pl.when(pl.program_id(1) == 0)
    def _():
        first = (i * tm) % seq == 0
        last = ((i + 1) * tm) % seq == 0
        prev = prev_ref[...]
        nxt = next_ref[...]
        hext_ref[0:HALO, :] = jnp.where(first, jnp.zeros_like(prev), prev)
        hext_ref[HALO:HALO + tm, :] = h_ref[...]
        hext_ref[HALO + tm:, :] = jnp.where(last, jnp.zeros_like(nxt), nxt)

    g = _dot(hext_ref[...], wg_ref[...].astype(BF16))
    u = _dot(h_ref[...], wu_ref[...].astype(BF16))
    rows = g.shape[0]
    g_prev = pltpu.roll(g, 1, axis=0)[HALO:HALO + tm]
    g_next = pltpu.roll(g, rows - 1, axis=0)[HALO:HALO + tm]
    cw = cw_ref[...]
    c = g_prev * cw[0:1] + g[HALO:HALO + tm] * cw[1:2] + g_next * cw[2:3] + cb_ref[...]
    o_ref[...] = (c / (1.0 + jnp.exp(-c)) * u).astype(BF16)


def _ffn_up(h, w_gate, w_up, conv_w, conv_b, *, seq, tm, tf):
    t = h.shape[0]
    hb = tm // HALO
    n_halo = t // HALO
    return pl.pallas_call(
        functools.partial(_ffn_up_kernel, seq=seq),
        grid=(t // tm, D_FF // tf),
        in_specs=[
            pl.BlockSpec((tm, D_MODEL), lambda i, j: (i, 0)),
            pl.BlockSpec((HALO, D_MODEL), lambda i, j: (jnp.maximum(i * hb - 1, 0), 0)),
            pl.BlockSpec((HALO, D_MODEL), lambda i, j: (jnp.minimum((i + 1) * hb, n_halo - 1), 0)),
            pl.BlockSpec((D_MODEL, tf), lambda i, j: (0, j)),
            pl.BlockSpec((D_MODEL, tf), lambda i, j: (0, j)),
            pl.BlockSpec((3, tf), lambda i, j: (0, j)),
            pl.BlockSpec((1, tf), lambda i, j: (0, j)),
        ],
        out_specs=pl.BlockSpec((tm, tf), lambda i, j: (i, j)),
        out_shape=jax.ShapeDtypeStruct((t, D_FF), BF16),
        scratch_shapes=[pltpu.VMEM((tm + 2 * HALO, D_MODEL), BF16)],
        compiler_params=_params("parallel", "arbitrary"),
        name="ffn_up",
    )(h, h, h, w_gate, w_up, conv_w, conv_b)


def _ffn_down_kernel(a_ref, w_ref, x_ref, g_ref, o_ref):
    o_ref[...] = _rms(x_ref[...] + _dot(a_ref[...], w_ref[...]), g_ref[...])


def _ffn_down(act, w_down, x, g_final, *, tm):
    t = x.shape[0]
    return pl.pallas_call(
        _ffn_down_kernel,
        grid=(t // tm,),
        in_specs=[
            pl.BlockSpec((tm, D_FF), lambda i: (i, 0)),
            _const_spec(w_down.shape),
            pl.BlockSpec((tm, D_MODEL), lambda i: (i, 0)),
            _const_spec(g_final.shape),
        ],
        out_specs=pl.BlockSpec((tm, D_MODEL), lambda i: (i, 0)),
        out_shape=jax.ShapeDtypeStruct((t, D_MODEL), F32),
        compiler_params=_params("parallel"),
        name="ffn_down",
    )(act, w_down, x, g_final)


def _dft_tables(seq):
    n1 = DFT_N1
    n2 = seq // n1
    c = np.arange(FOURIER_GROUP_DIM)
    ang = 2.0 * np.pi * ((c[:, None] * c[None, :]) % FOURIER_GROUP_DIM) / FOURIER_GROUP_DIM
    dft_c = np.concatenate([np.cos(ang), np.sin(ang)], axis=1)

    j = np.arange(n1)
    a1 = 2.0 * np.pi * ((j[:, None] * j[None, :]) % n1) / n1
    c1, s1 = np.cos(a1), np.sin(a1)
    m1 = np.block([[c1, -s1], [-s1, -c1]])

    k1 = np.arange(n1)[:, None, None]
    k2 = np.arange(n2)[None, :, None]
    m2 = np.arange(n2)[None, None, :]
    a2 = 2.0 * np.pi * ((m2 * (k1 + n1 * k2)) % seq) / seq
    e = np.concatenate([np.cos(a2), np.sin(a2)], axis=2)
    return (jnp.asarray(dft_c, BF16), jnp.asarray(m1, BF16), jnp.asarray(e, BF16))


def _rope_tables(seq):
    inv = ROPE_THETA ** (-jnp.arange(0, QK_ROPE_DIM, 2, dtype=F32) / QK_ROPE_DIM)
    ang = jnp.arange(seq, dtype=F32)[:, None] * inv[None, :]
    cos, sin = jnp.cos(ang), jnp.sin(ang)
    return (jnp.concatenate([cos, cos, cos, cos], axis=1), jnp.concatenate([-sin, sin, -sin, sin], axis=1),
            jnp.concatenate([cos.T, cos.T], axis=0), jnp.concatenate([-sin.T, sin.T], axis=0))


def _prep_weights(w_in, w_uq, w_ukv):
    half = QK_ROPE_DIM // 2
    c1 = FOURIER_DIM + Q_LORA_RANK + KV_LORA_RANK
    w_in_ext = jnp.concatenate(
        [w_in, w_in[:, c1 + half:c1 + QK_ROPE_DIM], w_in[:, c1:c1 + half]], axis=1).astype(BF16)
    q3 = w_uq.reshape(Q_LORA_RANK, N_HEADS, QK_DIM)
    w_uqt_ext = jnp.concatenate(
        [q3[:, :, :QK_NOPE_DIM].reshape(Q_LORA_RANK, -1), q3[:, :, QK_NOPE_DIM:].reshape(Q_LORA_RANK, -1)],
        axis=1).T.astype(BF16)
    kv3 = w_ukv.reshape(KV_LORA_RANK, N_HEADS, QK_NOPE_DIM + V_HEAD_DIM)
    w_k = kv3[:, :, :QK_NOPE_DIM].reshape(KV_LORA_RANK, -1).astype(BF16)
    w_vt = kv3[:, :, QK_NOPE_DIM:].reshape(KV_LORA_RANK, -1).T.astype(BF16)
    return w_in_ext, w_uqt_ext, w_k, w_vt


def _tile(n, pref):
    return pref if n % pref == 0 else n


def _tiles(s):
    attn_k = _tile(s, 512)
    return dict(
        in_proj=_tile(s, 512),
        attn_q=_tile(s, 1024), attn_tiles=4 if s % 4096 == 0 else 1,
        attn_k=attn_k, attn_group=4 if (s // attn_k) % 4 == 0 else 2,
        fourier_tn2=min(16, s // DFT_N1), fourier_kb=32,
        out_proj=_tile(s, 512), mem_kv_cols=1024, cross=_tile(s, 512),
        ffn_up=_tile(s, 1024), ffn_cols=512, ffn_down=_tile(s, 256),
    )


def _trunk(x, mem, p, tables):
    b, s, _ = x.shape
    t = b * s
    n1 = DFT_N1
    n2 = s // n1
    ts = _tiles(s)
    dft_c, m1, e, cos2, sin2, cos_t, sin_t = tables

    ab, qt, k, vt = _in_proj(x, p["g_mix"], p["w_in_ext"], p["g_q"], p["w_uqt_ext"], p["g_kv"], p["w_k"],
                             p["w_vt"], dft_c, cos2, sin2, cos_t, sin_t, tm=ts["in_proj"], tk=ts["attn_k"])

    gc = _fourier_a(ab.reshape(b, N_FOURIER_GROUPS, 2, n1, n2, FOURIER_GROUP_DIM), m1, tn2=ts["fourier_tn2"])
    f = _fourier_b(gc.reshape(b, N_FOURIER_GROUPS, 2, n1, n2, FOURIER_GROUP_DIM), e, kb=ts["fourier_kb"],
                   scale=float((s * FOURIER_GROUP_DIM) ** -0.5))

    a = _attention(qt, k, vt, tq=ts["attn_q"], tiles_per_step=ts["attn_tiles"], group_size=ts["attn_group"])

    x1 = _out_proj(x.reshape(t, D_MODEL), f.reshape(t, FOURIER_DIM), a.reshape(t, MLA_DIM),
                   p["g_f"], p["g_a"], p["w_out"], tm=ts["out_proj"])

    kvm = _mem_kv(mem, p["g_mem"], p["w_ckv"], tn=ts["mem_kv_cols"])
    x2, hf = _cross(x1.reshape(b, s, D_MODEL), kvm, p["g_cross"], p["w_cq"], p["w_co"], p["g_ffn"],
                    tm=ts["cross"])

    act = _ffn_up(hf.reshape(t, D_MODEL), p["w_gate"], p["w_up"], p["conv_w"], p["conv_b"], seq=s,
                  tm=ts["ffn_up"], tf=ts["ffn_cols"])
    y = _ffn_down(act, p["w_down"], x2.reshape(t, D_MODEL), p["g_final"], tm=ts["ffn_down"])
    return y.reshape(b, s, D_MODEL)


@jax.jit
def kernel(x_prompt, x_sample, mem_prompt, mem_sample, norm_mix_g, w_in, q_norm_g, w_uq, kv_norm_g, w_ukv,
           fourier_out_g, mla_out_g, w_out, norm_cross_g, norm_mem_g, w_cq, w_ckv, w_co, norm_ffn_g, w_gate,
           w_up, conv_w, conv_b, w_down, final_norm_g):
    assert norm_mix_g.shape[0] == 1, "single-layer trunk"
    assert x_prompt.shape[1] % DFT_N1 == 0 and x_sample.shape[1] % DFT_N1 == 0
    w_in_ext, w_uqt_ext, w_k, w_vt = _prep_weights(w_in[0], w_uq[0], w_ukv[0])
    row = lambda g: g.reshape(1, -1)
    p = dict(
        g_mix=row(norm_mix_g[0]), w_in_ext=w_in_ext, g_q=row(q_norm_g[0]), w_uqt_ext=w_uqt_ext,
        g_kv=row(kv_norm_g[0]), w_k=w_k, w_vt=w_vt, g_f=row(fourier_out_g[0]), g_a=row(mla_out_g[0]),
        w_out=w_out[0].astype(BF16), g_cross=row(norm_cross_g[0]), g_mem=row(norm_mem_g[0]),
        w_cq=w_cq[0].astype(BF16), w_ckv=w_ckv.reshape(w_ckv.shape[1:]), w_co=w_co[0].astype(BF16),
        g_ffn=row(norm_ffn_g[0]), w_gate=w_gate.reshape(w_gate.shape[1:]), w_up=w_up.reshape(w_up.shape[1:]),
        conv_w=conv_w[0], conv_b=row(conv_b[0]), w_down=w_down[0].astype(BF16), g_final=row(final_norm_g),
    )
    outs = []
    for x, mem in ((x_prompt, mem_prompt), (x_sample, mem_sample)):
        s = x.shape[1]
        tables = _dft_tables(s) + _rope_tables(s)
        outs.append(_trunk(x, mem, p, tables))
    return tuple(outs)
```

```python
import functools

import numpy as np
import jax
import jax.numpy as jnp
from jax import lax
from jax.experimental import pallas as pl
from jax.experimental.pallas import tpu as pltpu

F32 = jnp.float32
BF16 = jnp.bfloat16

D_MODEL = 2048
N_FOURIER_GROUPS = 4
FOURIER_GROUP_DIM = 256
FOURIER_DIM = N_FOURIER_GROUPS * FOURIER_GROUP_DIM
N_HEADS = 8
QK_NOPE_DIM = 128
QK_ROPE_DIM = 64
QK_DIM = QK_NOPE_DIM + QK_ROPE_DIM
V_HEAD_DIM = 128
V_EXT_DIM = V_HEAD_DIM + 16
Q_LORA_RANK = 512
KV_LORA_RANK = 512
MLA_DIM = N_HEADS * V_HEAD_DIM
ROPE_THETA = 10000.0
N_CROSS_HEADS = 4
CROSS_HEAD_DIM = D_MODEL // N_CROSS_HEADS
D_FF = 5632
EPS = 1e-6
LOG2_E = 1.4426950408889634

V7X_VMEM_LIMIT_BYTES = 56 * 1024 * 1024
MXU_DIM = 256
DFT_N1 = MXU_DIM // 2
HALO = 16


def _params(*sem):
    return pltpu.CompilerParams(dimension_semantics=sem, vmem_limit_bytes=V7X_VMEM_LIMIT_BYTES)


def _const_spec(shape):
    zeros = (0,) * len(shape)
    return pl.BlockSpec(shape, lambda *_: zeros, pipeline_mode=pl.Buffered(1))


def _rms(x, g):
    return x * lax.rsqrt(jnp.mean(x * x, axis=-1, keepdims=True) + EPS) * g


def _dot(a, b):
    return jnp.dot(a, b, preferred_element_type=F32)


def _dot_nt(a, b):
    return lax.dot_general(a, b, (((1,), (1,)), ((), ())), preferred_element_type=F32)


def _in_proj_kernel(x_ref, g_ref, win_ref, qg_ref, wuqt_ref, kvg_ref, wk_ref, wvt_ref, dft_ref, cos_ref,
                    sin_ref, cost_ref, sint_ref, ab_ref, qt_ref, k_ref, vt_ref):
    h = _rms(x_ref[...], g_ref[...]).astype(BF16)
    z = _dot(h, win_ref[...])

    for g in range(N_FOURIER_GROUPS):
        u = z[:, g * FOURIER_GROUP_DIM:(g + 1) * FOURIER_GROUP_DIM].astype(BF16)
        ab = _dot(u, dft_ref[...])
        ab_ref[g, 0] = ab[:, :FOURIER_GROUP_DIM]
        ab_ref[g, 1] = ab[:, FOURIER_GROUP_DIM:]

    c0 = FOURIER_DIM
    c_q = z[:, c0:c0 + Q_LORA_RANK]
    c_kv = z[:, c0 + Q_LORA_RANK:c0 + Q_LORA_RANK + KV_LORA_RANK]
    c1 = c0 + Q_LORA_RANK + KV_LORA_RANK
    k_r = z[:, c1:c1 + QK_ROPE_DIM]
    k_r_sw = z[:, c1 + QK_ROPE_DIM:c1 + 2 * QK_ROPE_DIM]

    cos = cos_ref[...]
    sin = sin_ref[...]
    k_rope = (k_r * cos[:, :QK_ROPE_DIM] + k_r_sw * sin[:, :QK_ROPE_DIM]).astype(BF16)

    hq = _rms(c_q, qg_ref[...]).astype(BF16)
    qe_t = _dot_nt(wuqt_ref[...], hq)
    q_nope_w = N_HEADS * QK_NOPE_DIM
    cos_t = jnp.concatenate([cost_ref[...]] * N_HEADS, axis=0)
    sin_t = jnp.concatenate([sint_ref[...]] * N_HEADS, axis=0)
    q_r = qe_t[q_nope_w:]
    half = QK_ROPE_DIM // 2
    q_sw = jnp.concatenate(
        [q_r[lo:lo + half] for hd in range(N_HEADS) for lo in (hd * QK_ROPE_DIM + half, hd * QK_ROPE_DIM)], axis=0)
    q_rope_t = q_r * cos_t + q_sw * sin_t

    hkv = _rms(c_kv, kvg_ref[...]).astype(BF16)
    k_nope = _dot(hkv, wk_ref[...])
    v_t = _dot_nt(wvt_ref[...], hkv)

    scale = QK_DIM ** -0.5 * LOG2_E
    for hd in range(N_HEADS):
        qt_ref[hd, 0:QK_NOPE_DIM, :] = (qe_t[hd * QK_NOPE_DIM:(hd + 1) * QK_NOPE_DIM] * scale).astype(BF16)
        qt_ref[hd, QK_NOPE_DIM:QK_DIM, :] = (
            q_rope_t[hd * QK_ROPE_DIM:(hd + 1) * QK_ROPE_DIM] * scale).astype(BF16)
        k_ref[hd, :, 0:QK_NOPE_DIM] = k_nope[:, hd * QK_NOPE_DIM:(hd + 1) * QK_NOPE_DIM].astype(BF16)
        k_ref[hd, :, QK_NOPE_DIM:QK_DIM] = k_rope
        vt_ref[hd, 0:V_HEAD_DIM, :] = v_t[hd * V_HEAD_DIM:(hd + 1) * V_HEAD_DIM, :].astype(BF16)
        pad_row = lax.broadcasted_iota(jnp.int32, (V_EXT_DIM - V_HEAD_DIM, v_t.shape[1]), 0)
        vt_ref[hd, V_HEAD_DIM:, :] = jnp.where(pad_row == 0, 1.0, 0.0).astype(BF16)


def _in_proj(x, g_mix, w_in_ext, q_g, w_uqt_ext, kv_g, w_k, w_vt, dft_c, cos2, sin2, cos_t, sin_t, *, tm, tk):
    b, s, _ = x.shape
    grid = (b, s // tm)
    r = tk // tm
    return pl.pallas_call(
        _in_proj_kernel,
        grid=grid,
        in_specs=[
            pl.BlockSpec((None, tm, D_MODEL), lambda bi, i: (bi, i, 0)),
            _const_spec(g_mix.shape),
            _const_spec(w_in_ext.shape),
            _const_spec(q_g.shape),
            _const_spec(w_uqt_ext.shape),
            _const_spec(kv_g.shape),
            _const_spec(w_k.shape),
            _const_spec(w_vt.shape),
            _const_spec(dft_c.shape),
            pl.BlockSpec((tm, 128), lambda bi, i: (i, 0)),
            pl.BlockSpec((tm, 128), lambda bi, i: (i, 0)),
            pl.BlockSpec((QK_ROPE_DIM, tm), lambda bi, i: (0, i)),
            pl.BlockSpec((QK_ROPE_DIM, tm), lambda bi, i: (0, i)),
        ],
        out_specs=[
            pl.BlockSpec((None, N_FOURIER_GROUPS, 2, tm, FOURIER_GROUP_DIM), lambda bi, i: (bi, 0, 0, i, 0)),
            pl.BlockSpec((None, N_HEADS, QK_DIM, tm), lambda bi, i: (bi, 0, 0, i)),
            pl.BlockSpec((None, N_HEADS, tm, QK_DIM), lambda bi, i: (bi, 0, i, 0)),
            pl.BlockSpec((None, N_HEADS, None, V_EXT_DIM, tm), lambda bi, i: (bi, 0, i // r, 0, i % r)),
        ],
        out_shape=[
            jax.ShapeDtypeStruct((b, N_FOURIER_GROUPS, 2, s, FOURIER_GROUP_DIM), F32),
            jax.ShapeDtypeStruct((b, N_HEADS, QK_DIM, s), BF16),
            jax.ShapeDtypeStruct((b, N_HEADS, s, QK_DIM), BF16),
            jax.ShapeDtypeStruct((b, N_HEADS, s // tk, V_EXT_DIM, tk), BF16),
        ],
        compiler_params=_params("parallel", "parallel"),
        name="in_proj",
    )(x, g_mix, w_in_ext, q_g, w_uqt_ext, kv_g, w_k, w_vt, dft_c, cos2, sin2, cos_t, sin_t)


def _fourier_a_kernel(m_ref, x_ref, o_ref):
    two, n1, tn2, c = x_ref.shape
    for j in range(tn2):
        x = jnp.concatenate([x_ref[0, :, j, :], x_ref[1, :, j, :]], axis=0).astype(BF16)
        o_ref[:, :, j * c:(j + 1) * c] = _dot(m_ref[...], x).astype(BF16).reshape(two, n1, c)


def _fourier_a(ab, m1, *, tn2):
    b, g, two, n1, n2, c = ab.shape
    return pl.pallas_call(
        _fourier_a_kernel,
        grid=(b, g, n2 // tn2),
        in_specs=[
            _const_spec(m1.shape),
            pl.BlockSpec((None, None, two, n1, tn2, c), lambda bi, gi, ci: (bi, gi, 0, 0, ci, 0)),
        ],
        out_specs=pl.BlockSpec((None, None, two, n1, tn2 * c), lambda bi, gi, ci: (bi, gi, 0, 0, ci)),
        out_shape=jax.ShapeDtypeStruct((b, g, two, n1, n2 * c), BF16),
        compiler_params=_params("parallel", "parallel", "parallel"),
        name="fourier_a",
    )(m1, ab)


def _fourier_b_kernel(e_ref, g_ref, o_ref, *, scale):
    kb = e_ref.shape[0]
    for j in range(kb):
        x = jnp.concatenate([g_ref[0, j], g_ref[1, j]], axis=0)
        o_ref[:, j, :] = _dot(e_ref[j], x) * scale


def _fourier_b(gc, e, *, kb, scale):
    b, g, two, n1, n2, c = gc.shape
    return pl.pallas_call(
        functools.partial(_fourier_b_kernel, scale=scale),
        grid=(b, g, n1 // kb),
        in_specs=[
            pl.BlockSpec((kb, n2, 2 * n2), lambda bi, gi, ki: (ki, 0, 0)),
            pl.BlockSpec((None, None, two, kb, n2, c), lambda bi, gi, ki: (bi, gi, 0, ki, 0, 0)),
        ],
        out_specs=pl.BlockSpec((None, n2, kb, c), lambda bi, gi, ki: (bi, 0, ki, gi)),
        out_shape=jax.ShapeDtypeStruct((b, n2, n1, g * c), F32),
        compiler_params=_params("parallel", "parallel", "parallel"),
        name="fourier_b",
    )(e, gc)


def _attn_kernel(qt_ref, k_ref, vt_ref, o_ref, sa_ref, sb_ref, acc_ref, *, group_size):
    tk, tq = sa_ref.shape
    nk = vt_ref.shape[0]
    n_tiles = qt_ref.shape[1] // tq
    bufs = (sa_ref, sb_ref)

    def scores(tile, j, s_ref):
        start = pl.multiple_of(j * tk, tk)
        qt = qt_ref[:, tile * tq:(tile + 1) * tq]
        s = _dot(k_ref[pl.ds(start, tk), :], qt)
        s_ref[...] = s
        return jnp.max(s, axis=0, keepdims=True)

    def update(j, s_ref, cmax, m):
        m_new = jnp.maximum(m, cmax)
        p = jnp.exp2(s_ref[...] - m_new)
        acc_ref[...] = jnp.exp2(m - m_new) * acc_ref[...] + _dot(vt_ref[j], p.astype(BF16))
        return m_new

    def group(tile, j, cmax, m, last):
        for t in range(group_size):
            if not last or t + 1 < group_size:
                nxt = scores(tile, j + t + 1, bufs[(t + 1) % 2])
            elif tile + 1 < n_tiles:
                nxt = scores(tile + 1, 0, bufs[0])
            else:
                nxt = None
            m = update(j + t, bufs[t % 2], cmax, m)
            cmax = nxt
        return cmax, m

    acc_ref[...] = jnp.zeros_like(acc_ref)
    cmax = scores(0, 0, sa_ref)
    for tile in range(n_tiles):
        carry = (cmax, jnp.full((1, tq), -jnp.inf, F32))
        carry = lax.fori_loop(0, nk // group_size - 1,
                              lambda i, c, tile=tile: group(tile, group_size * i, *c, False), carry)
        cmax, _ = group(tile, nk - group_size, *carry, True)
        acc = acc_ref[...]
        o_ref[tile * tq:(tile + 1) * tq, :] = (acc[0:V_HEAD_DIM] / acc[V_HEAD_DIM:V_HEAD_DIM + 1]).T


def _attention(qt, k, vt, *, tq, tiles_per_step, group_size):
    b, h, s, _ = k.shape
    nk, _, tk = vt.shape[2:]
    assert group_size % 2 == 0 and nk % group_size == 0
    rows = tq * tiles_per_step
    return pl.pallas_call(
        functools.partial(_attn_kernel, group_size=group_size),
        grid=(b, h, s // rows),
        in_specs=[
            pl.BlockSpec((None, None, QK_DIM, rows), lambda bi, hi, i: (bi, hi, 0, i)),
            pl.BlockSpec((None, None, s, QK_DIM), lambda bi, hi, i: (bi, hi, 0, 0)),
            pl.BlockSpec((None, None, nk, V_EXT_DIM, tk), lambda bi, hi, i: (bi, hi, 0, 0, 0)),
        ],
        out_specs=pl.BlockSpec((None, rows, V_HEAD_DIM), lambda bi, hi, i: (bi, i, hi)),
        out_shape=jax.ShapeDtypeStruct((b, s, h * V_HEAD_DIM), F32),
        scratch_shapes=[pltpu.VMEM((tk, tq), F32), pltpu.VMEM((tk, tq), F32), pltpu.VMEM((V_EXT_DIM, tq), F32)],
        compiler_params=_params("parallel", "parallel", "parallel"),
        name="attention",
    )(qt, k, vt)


def _out_proj_kernel(x_ref, f_ref, a_ref, gf_ref, ga_ref, w_ref, o_ref):
    hf = _rms(f_ref[...], gf_ref[...]).astype(BF16)
    ha = _rms(a_ref[...], ga_ref[...]).astype(BF16)
    o_ref[...] = x_ref[...] + _dot(hf, w_ref[0:FOURIER_DIM, :]) + _dot(ha, w_ref[FOURIER_DIM:, :])


def _out_proj(x, f, a, g_f, g_a, w_out, *, tm):
    t = x.shape[0]
    return pl.pallas_call(
        _out_proj_kernel,
        grid=(t // tm,),
        in_specs=[
            pl.BlockSpec((tm, D_MODEL), lambda i: (i, 0)),
            pl.BlockSpec((tm, FOURIER_DIM), lambda i: (i, 0)),
            pl.BlockSpec((tm, MLA_DIM), lambda i: (i, 0)),
            _const_spec(g_f.shape),
            _const_spec(g_a.shape),
            _const_spec(w_out.shape),
        ],
        out_specs=pl.BlockSpec((tm, D_MODEL), lambda i: (i, 0)),
        out_shape=jax.ShapeDtypeStruct((t, D_MODEL), F32),
        compiler_params=_params("parallel"),
        name="out_proj",
    )(x, f, a, g_f, g_a, w_out)


def _mem_kv_kernel(m_ref, g_ref, w_ref, o_ref):
    o_ref[...] = _dot(_rms(m_ref[...], g_ref[...]).astype(BF16), w_ref[...].astype(BF16)).astype(BF16)


def _mem_kv(mem, g_mem, w_ckv, *, tn):
    b, m, _ = mem.shape
    n = w_ckv.shape[1]
    return pl.pallas_call(
        _mem_kv_kernel,
        grid=(n // tn, b),
        in_specs=[
            pl.BlockSpec((None, m, D_MODEL), lambda j, bi: (bi, 0, 0)),
            _const_spec(g_mem.shape),
            pl.BlockSpec((D_MODEL, tn), lambda j, bi: (0, j)),
        ],
        out_specs=pl.BlockSpec((None, m, tn), lambda j, bi: (bi, 0, j)),
        out_shape=jax.ShapeDtypeStruct((b, m, n), BF16),
        compiler_params=_params("arbitrary", "arbitrary"),
        name="mem_kv",
    )(mem, g_mem, w_ckv)


def _cross_kernel(x_ref, kv_ref, gc_ref, wq_ref, wo_ref, gn_ref, o_ref, h_ref):
    x = x_ref[...]
    h = _rms(x, gc_ref[...]).astype(BF16)
    q = _dot(h, wq_ref[...]) * (CROSS_HEAD_DIM ** -0.5)
    heads = []
    for hd in range(N_CROSS_HEADS):
        lo = hd * CROSS_HEAD_DIM
        s = _dot_nt(q[:, lo:lo + CROSS_HEAD_DIM].astype(BF16), kv_ref[:, lo:lo + CROSS_HEAD_DIM])
        p = jnp.exp(s - jnp.max(s, axis=-1, keepdims=True))
        p = p / jnp.sum(p, axis=-1, keepdims=True)
        heads.append(_dot(p.astype(BF16), kv_ref[:, D_MODEL + lo:D_MODEL + lo + CROSS_HEAD_DIM]).astype(BF16))
    y = x + _dot(jnp.concatenate(heads, axis=1), wo_ref[...])
    o_ref[...] = y
    h_ref[...] = _rms(y, gn_ref[...]).astype(BF16)


def _cross(x, kvm, g_cross, w_cq, w_co, g_ffn, *, tm):
    b, s, _ = x.shape
    m = kvm.shape[1]
    spec = pl.BlockSpec((None, tm, D_MODEL), lambda bi, i: (bi, i, 0))
    return pl.pallas_call(
        _cross_kernel,
        grid=(b, s // tm),
        in_specs=[
            spec,
            pl.BlockSpec((None, m, 2 * D_MODEL), lambda bi, i: (bi, 0, 0)),
            _const_spec(g_cross.shape),
            _const_spec(w_cq.shape),
            _const_spec(w_co.shape),
            _const_spec(g_ffn.shape),
        ],
        out_specs=[spec, spec],
        out_shape=[jax.ShapeDtypeStruct((b, s, D_MODEL), F32), jax.ShapeDtypeStruct((b, s, D_MODEL), BF16)],
        compiler_params=_params("parallel", "parallel"),
        name="cross",
    )(x, kvm, g_cross, w_cq, w_co, g_ffn)


def _ffn_up_kernel(h_ref, prev_ref, next_ref, wg_ref, wu_ref, cwb_ref, o_ref, hext_ref, *, seq):
    i = pl.program_id(0)
    j = pl.program_id(1)
    tm = h_ref.shape[0]

    @pl.when(j == 0)
    def _():
        first = (i * tm) % seq == 0
        last = ((i + 1) * tm) % seq == 0
        prev = prev_ref[...]
        nxt = next_ref[...]
        hext_ref[0:HALO, :] = jnp.where(first, jnp.zeros_like(prev), prev)
        hext_ref[HALO:HALO + tm, :] = h_ref[...]
        hext_ref[HALO + tm:, :] = jnp.where(last, jnp.zeros_like(nxt), nxt)

    g = _dot(hext_ref[...], wg_ref[...].astype(BF16))
    u = _dot(h_ref[...], wu_ref[...].astype(BF16))
    rows = g.shape[0]
    g_prev = pltpu.roll(g, 1, axis=0)[HALO:HALO + tm]
    g_next = pltpu.roll(g, rows - 1, axis=0)[HALO:HALO + tm]
    cwb = cwb_ref[j]
    c = g_prev * cwb[0:1] + g[HALO:HALO + tm] * cwb[1:2] + g_next * cwb[2:3] + cwb[3:4]
    o_ref[...] = (c / (1.0 + jnp.exp(-c)) * u).astype(BF16)


def _ffn_up(h, w_gate, w_up, conv_w, conv_b, *, seq, tm, tf):
    t = h.shape[0]
    hb = tm // HALO
    n_halo = t // HALO
    cwb = jnp.concatenate([conv_w, conv_b], axis=0).reshape(4, D_FF // tf, tf).transpose(1, 0, 2)
    return pl.pallas_call(
        functools.partial(_ffn_up_kernel, seq=seq),
        grid=(t // tm, D_FF // tf),
        in_specs=[
            pl.BlockSpec((tm, D_MODEL), lambda i, j: (i, 0)),
            pl.BlockSpec((HALO, D_MODEL), lambda i, j: (jnp.maximum(i * hb - 1, 0), 0)),
            pl.BlockSpec((HALO, D_MODEL), lambda i, j: (jnp.minimum((i + 1) * hb, n_halo - 1), 0)),
            pl.BlockSpec((D_MODEL, tf), lambda i, j: (0, j)),
            pl.BlockSpec((D_MODEL, tf), lambda i, j: (0, j)),
            _const_spec(cwb.shape),
        ],
        out_specs=pl.BlockSpec((tm, tf), lambda i, j: (i, j)),
        out_shape=jax.ShapeDtypeStruct((t, D_FF), BF16),
        scratch_shapes=[pltpu.VMEM((tm + 2 * HALO, D_MODEL), BF16)],
        compiler_params=_params("parallel", "arbitrary"),
        name="ffn_up",
    )(h, h, h, w_gate, w_up, cwb)


def _ffn_down_kernel(a_ref, w_ref, x_ref, g_ref, o_ref):
    o_ref[...] = _rms(x_ref[...] + _dot(a_ref[...], w_ref[...]), g_ref[...])


def _ffn_down(act, w_down, x, g_final, *, tm):
    t = x.shape[0]
    return pl.pallas_call(
        _ffn_down_kernel,
        grid=(t // tm,),
        in_specs=[
            pl.BlockSpec((tm, D_FF), lambda i: (i, 0)),
            _const_spec(w_down.shape),
            pl.BlockSpec((tm, D_MODEL), lambda i: (i, 0)),
            _const_spec(g_final.shape),
        ],
        out_specs=pl.BlockSpec((tm, D_MODEL), lambda i: (i, 0)),
        out_shape=jax.ShapeDtypeStruct((t, D_MODEL), F32),
        compiler_params=_params("parallel"),
        name="ffn_down",
    )(act, w_down, x, g_final)


def _dft_tables(seq):
    n1 = DFT_N1
    n2 = seq // n1
    c = np.arange(FOURIER_GROUP_DIM)
    ang = 2.0 * np.pi * ((c[:, None] * c[None, :]) % FOURIER_GROUP_DIM) / FOURIER_GROUP_DIM
    dft_c = np.concatenate([np.cos(ang), np.sin(ang)], axis=1)

    j = np.arange(n1)
    a1 = 2.0 * np.pi * ((j[:, None] * j[None, :]) % n1) / n1
    c1, s1 = np.cos(a1), np.sin(a1)
    m1 = np.block([[c1, -s1], [-s1, -c1]])

    k1 = np.arange(n1)[:, None, None]
    k2 = np.arange(n2)[None, :, None]
    m2 = np.arange(n2)[None, None, :]
    a2 = 2.0 * np.pi * ((m2 * (k1 + n1 * k2)) % seq) / seq
    e = np.concatenate([np.cos(a2), np.sin(a2)], axis=2)
    return (jnp.asarray(dft_c, BF16), jnp.asarray(m1, BF16), jnp.asarray(e, BF16))


def _rope_tables(seq):
    inv = ROPE_THETA ** (-jnp.arange(0, QK_ROPE_DIM, 2, dtype=F32) / QK_ROPE_DIM)
    ang = jnp.arange(seq, dtype=F32)[:, None] * inv[None, :]
    cos, sin = jnp.cos(ang), jnp.sin(ang)
    return (jnp.concatenate([cos, cos, cos, cos], axis=1), jnp.concatenate([-sin, sin, -sin, sin], axis=1),
            jnp.concatenate([cos.T, cos.T], axis=0), jnp.concatenate([-sin.T, sin.T], axis=0))


def _prep_weights(w_in, w_uq, w_ukv):
    half = QK_ROPE_DIM // 2
    c1 = FOURIER_DIM + Q_LORA_RANK + KV_LORA_RANK
    w_in_ext = jnp.concatenate(
        [w_in, w_in[:, c1 + half:c1 + QK_ROPE_DIM], w_in[:, c1:c1 + half]], axis=1).astype(BF16)
    q3 = w_uq.reshape(Q_LORA_RANK, N_HEADS, QK_DIM)
    w_uqt_ext = jnp.concatenate(
        [q3[:, :, :QK_NOPE_DIM].reshape(Q_LORA_RANK, -1), q3[:, :, QK_NOPE_DIM:].reshape(Q_LORA_RANK, -1)],
        axis=1).T.astype(BF16)
    kv3 = w_ukv.reshape(KV_LORA_RANK, N_HEADS, QK_NOPE_DIM + V_HEAD_DIM)
    w_k = kv3[:, :, :QK_NOPE_DIM].reshape(KV_LORA_RANK, -1).astype(BF16)
    w_vt = kv3[:, :, QK_NOPE_DIM:].reshape(KV_LORA_RANK, -1).T.astype(BF16)
    return w_in_ext, w_uqt_ext, w_k, w_vt


def _tile(n, pref):
    return pref if n % pref == 0 else n


def _tiles(s):
    attn_k = _tile(s, 512)
    return dict(
        in_proj=_tile(s, 512),
        attn_q=_tile(s, 1024), attn_tiles=4 if s % 4096 == 0 else 1,
        attn_k=attn_k, attn_group=4 if (s // attn_k) % 4 == 0 else 2,
        fourier_tn2=min(16, s // DFT_N1), fourier_kb=32,
        out_proj=_tile(s, 512), mem_kv_cols=1024, cross=_tile(s, 512),
        ffn_up=_tile(s, 1024), ffn_cols=512, ffn_down=_tile(s, 256),
    )


def _trunk(x, mem, p, tables):
    b, s, _ = x.shape
    t = b * s
    n1 = DFT_N1
    n2 = s // n1
    ts = _tiles(s)
    dft_c, m1, e, cos2, sin2, cos_t, sin_t = tables

    ab, qt, k, vt = _in_proj(x, p["g_mix"], p["w_in_ext"], p["g_q"], p["w_uqt_ext"], p["g_kv"], p["w_k"],
                             p["w_vt"], dft_c, cos2, sin2, cos_t, sin_t, tm=ts["in_proj"], tk=ts["attn_k"])

    gc = _fourier_a(ab.reshape(b, N_FOURIER_GROUPS, 2, n1, n2, FOURIER_GROUP_DIM), m1, tn2=ts["fourier_tn2"])
    f = _fourier_b(gc.reshape(b, N_FOURIER_GROUPS, 2, n1, n2, FOURIER_GROUP_DIM), e, kb=ts["fourier_kb"],
                   scale=float((s * FOURIER_GROUP_DIM) ** -0.5))

    a = _attention(qt, k, vt, tq=ts["attn_q"], tiles_per_step=ts["attn_tiles"], group_size=ts["attn_group"])

    x1 = _out_proj(x.reshape(t, D_MODEL), f.reshape(t, FOURIER_DIM), a.reshape(t, MLA_DIM),
                   p["g_f"], p["g_a"], p["w_out"], tm=ts["out_proj"])

    kvm = _mem_kv(mem, p["g_mem"], p["w_ckv"], tn=ts["mem_kv_cols"])
    x2, hf = _cross(x1.reshape(b, s, D_MODEL), kvm, p["g_cross"], p["w_cq"], p["w_co"], p["g_ffn"],
                    tm=ts["cross"])

    act = _ffn_up(hf.reshape(t, D_MODEL), p["w_gate"], p["w_up"], p["conv_w"], p["conv_b"], seq=s,
                  tm=ts["ffn_up"], tf=ts["ffn_cols"])
    y = _ffn_down(act, p["w_down"], x2.reshape(t, D_MODEL), p["g_final"], tm=ts["ffn_down"])
    return y.reshape(b, s, D_MODEL)


@jax.jit
def kernel(x_prompt, x_sample, mem_prompt, mem_sample, norm_mix_g, w_in, q_norm_g, w_uq, kv_norm_g, w_ukv,
           fourier_out_g, mla_out_g, w_out, norm_cross_g, norm_mem_g, w_cq, w_ckv, w_co, norm_ffn_g, w_gate,
           w_up, conv_w, conv_b, w_down, final_norm_g):
    assert norm_mix_g.shape[0] == 1, "single-layer trunk"
    assert x_prompt.shape[1] % DFT_N1 == 0 and x_sample.shape[1] % DFT_N1 == 0
    w_in_ext, w_uqt_ext, w_k, w_vt = _prep_weights(w_in[0], w_uq[0], w_ukv[0])
    row = lambda g: g.reshape(1, -1)
    p = dict(
        g_mix=row(norm_mix_g[0]), w_in_ext=w_in_ext, g_q=row(q_norm_g[0]), w_uqt_ext=w_uqt_ext,
        g_kv=row(kv_norm_g[0]), w_k=w_k, w_vt=w_vt, g_f=row(fourier_out_g[0]), g_a=row(mla_out_g[0]),
        w_out=w_out[0].astype(BF16), g_cross=row(norm_cross_g[0]), g_mem=row(norm_mem_g[0]),
        w_cq=w_cq[0].astype(BF16), w_ckv=w_ckv.reshape(w_ckv.shape[1:]), w_co=w_co[0].astype(BF16),
        g_ffn=row(norm_ffn_g[0]), w_gate=w_gate.reshape(w_gate.shape[1:]), w_up=w_up.reshape(w_up.shape[1:]),
        conv_w=conv_w[0], conv_b=row(conv_b[0]), w_down=w_down[0].astype(BF16), g_final=row(final_norm_g),
    )
    outs = []
    for x, mem in ((x_prompt, mem_prompt), (x_sample, mem_sample)):
        s = x.shape[1]
        tables = _dft_tables(s) + _rope_tables(s)
        outs.append(_trunk(x, mem, p, tables))
    return tuple(outs)
```

```python
import functools

import numpy as np
import jax
import jax.numpy as jnp
from jax import lax
from jax.experimental import pallas as pl
from jax.experimental.pallas import tpu as pltpu

F32 = jnp.float32
BF16 = jnp.bfloat16

D_MODEL = 2048
N_FOURIER_GROUPS = 4
FOURIER_GROUP_DIM = 256
FOURIER_DIM = N_FOURIER_GROUPS * FOURIER_GROUP_DIM
N_HEADS = 8
QK_NOPE_DIM = 128
QK_ROPE_DIM = 64
QK_DIM = QK_NOPE_DIM + QK_ROPE_DIM
V_HEAD_DIM = 128
V_EXT_DIM = V_HEAD_DIM + 16
Q_LORA_RANK = 512
KV_LORA_RANK = 512
MLA_DIM = N_HEADS * V_HEAD_DIM
ROPE_THETA = 10000.0
N_CROSS_HEADS = 4
CROSS_HEAD_DIM = D_MODEL // N_CROSS_HEADS
D_FF = 5632
EPS = 1e-6
LOG2_E = 1.4426950408889634

V7X_VMEM_LIMIT_BYTES = 56 * 1024 * 1024
MXU_DIM = 256
DFT_N1 = MXU_DIM // 2
HALO = 16


def _params(*sem):
    return pltpu.CompilerParams(dimension_semantics=sem, vmem_limit_bytes=V7X_VMEM_LIMIT_BYTES)


def _const_spec(shape):
    zeros = (0,) * len(shape)
    return pl.BlockSpec(shape, lambda *_: zeros, pipeline_mode=pl.Buffered(1))


def _rms(x, g):
    return x * lax.rsqrt(jnp.mean(x * x, axis=-1, keepdims=True) + EPS) * g


def _dot(a, b):
    return jnp.dot(a, b, preferred_element_type=F32)


def _dot_nt(a, b):
    return lax.dot_general(a, b, (((1,), (1,)), ((), ())), preferred_element_type=F32)


def _in_proj_kernel(x_ref, g_ref, win_ref, qg_ref, wuqt_ref, kvg_ref, wk_ref, wvt_ref, dft_ref, cos_ref,
                    sin_ref, cost_ref, sint_ref, ab_ref, qt_ref, k_ref, vt_ref):
    h = _rms(x_ref[...], g_ref[...]).astype(BF16)
    z = _dot(h, win_ref[...])

    for g in range(N_FOURIER_GROUPS):
        u = z[:, g * FOURIER_GROUP_DIM:(g + 1) * FOURIER_GROUP_DIM].astype(BF16)
        ab = _dot(u, dft_ref[...])
        ab_ref[g, 0] = ab[:, :FOURIER_GROUP_DIM]
        ab_ref[g, 1] = ab[:, FOURIER_GROUP_DIM:]

    c0 = FOURIER_DIM
    c_q = z[:, c0:c0 + Q_LORA_RANK]
    c_kv = z[:, c0 + Q_LORA_RANK:c0 + Q_LORA_RANK + KV_LORA_RANK]
    c1 = c0 + Q_LORA_RANK + KV_LORA_RANK
    k_r = z[:, c1:c1 + QK_ROPE_DIM]
    k_r_sw = z[:, c1 + QK_ROPE_DIM:c1 + 2 * QK_ROPE_DIM]

    cos = cos_ref[...]
    sin = sin_ref[...]
    k_rope = (k_r * cos[:, :QK_ROPE_DIM] + k_r_sw * sin[:, :QK_ROPE_DIM]).astype(BF16)

    hq = _rms(c_q, qg_ref[...]).astype(BF16)
    qe_t = _dot_nt(wuqt_ref[...], hq)
    q_nope_w = N_HEADS * QK_NOPE_DIM
    cos_t = jnp.concatenate([cost_ref[...]] * N_HEADS, axis=0)
    sin_t = jnp.concatenate([sint_ref[...]] * N_HEADS, axis=0)
    q_r = qe_t[q_nope_w:]
    half = QK_ROPE_DIM // 2
    q_sw = jnp.concatenate(
        [q_r[lo:lo + half] for hd in range(N_HEADS) for lo in (hd * QK_ROPE_DIM + half, hd * QK_ROPE_DIM)], axis=0)
    q_rope_t = q_r * cos_t + q_sw * sin_t

    hkv = _rms(c_kv, kvg_ref[...]).astype(BF16)
    k_nope = _dot(hkv, wk_ref[...])
    v_t = _dot_nt(wvt_ref[...], hkv)

    scale = QK_DIM ** -0.5 * LOG2_E
    for hd in range(N_HEADS):
        qt_ref[hd, 0:QK_NOPE_DIM, :] = (qe_t[hd * QK_NOPE_DIM:(hd + 1) * QK_NOPE_DIM] * scale).astype(BF16)
        qt_ref[hd, QK_NOPE_DIM:QK_DIM, :] = (
            q_rope_t[hd * QK_ROPE_DIM:(hd + 1) * QK_ROPE_DIM] * scale).astype(BF16)
        k_ref[hd, :, 0:QK_NOPE_DIM] = k_nope[:, hd * QK_NOPE_DIM:(hd + 1) * QK_NOPE_DIM].astype(BF16)
        k_ref[hd, :, QK_NOPE_DIM:QK_DIM] = k_rope
        vt_ref[hd, 0:V_HEAD_DIM, :] = v_t[hd * V_HEAD_DIM:(hd + 1) * V_HEAD_DIM, :].astype(BF16)
        pad_row = lax.broadcasted_iota(jnp.int32, (V_EXT_DIM - V_HEAD_DIM, v_t.shape[1]), 0)
        vt_ref[hd, V_HEAD_DIM:, :] = jnp.where(pad_row == 0, 1.0, 0.0).astype(BF16)


def _in_proj(x, g_mix, w_in_ext, q_g, w_uqt_ext, kv_g, w_k, w_vt, dft_c, cos2, sin2, cos_t, sin_t, *, tm, tk):
    b, s, _ = x.shape
    grid = (b, s // tm)
    r = tk // tm
    return pl.pallas_call(
        _in_proj_kernel,
        grid=grid,
        in_specs=[
            pl.BlockSpec((None, tm, D_MODEL), lambda bi, i: (bi, i, 0)),
            _const_spec(g_mix.shape),
            _const_spec(w_in_ext.shape),
            _const_spec(q_g.shape),
            _const_spec(w_uqt_ext.shape),
            _const_spec(kv_g.shape),
            _const_spec(w_k.shape),
            _const_spec(w_vt.shape),
            _const_spec(dft_c.shape),
            pl.BlockSpec((tm, 128), lambda bi, i: (i, 0)),
            pl.BlockSpec((tm, 128), lambda bi, i: (i, 0)),
            pl.BlockSpec((QK_ROPE_DIM, tm), lambda bi, i: (0, i)),
            pl.BlockSpec((QK_ROPE_DIM, tm), lambda bi, i: (0, i)),
        ],
        out_specs=[
            pl.BlockSpec((None, N_FOURIER_GROUPS, 2, tm, FOURIER_GROUP_DIM), lambda bi, i: (bi, 0, 0, i, 0)),
            pl.BlockSpec((None, N_HEADS, QK_DIM, tm), lambda bi, i: (bi, 0, 0, i)),
            pl.BlockSpec((None, N_HEADS, tm, QK_DIM), lambda bi, i: (bi, 0, i, 0)),
            pl.BlockSpec((None, N_HEADS, None, V_EXT_DIM, tm), lambda bi, i: (bi, 0, i // r, 0, i % r)),
        ],
        out_shape=[
            jax.ShapeDtypeStruct((b, N_FOURIER_GROUPS, 2, s, FOURIER_GROUP_DIM), F32),
            jax.ShapeDtypeStruct((b, N_HEADS, QK_DIM, s), BF16),
            jax.ShapeDtypeStruct((b, N_HEADS, s, QK_DIM), BF16),
            jax.ShapeDtypeStruct((b, N_HEADS, s // tk, V_EXT_DIM, tk), BF16),
        ],
        compiler_params=_params("parallel", "parallel"),
        name="in_proj",
    )(x, g_mix, w_in_ext, q_g, w_uqt_ext, kv_g, w_k, w_vt, dft_c, cos2, sin2, cos_t, sin_t)


def _fourier_a_kernel(m_ref, x_ref, o_ref):
    two, n1, tn2, c = x_ref.shape
    for j in range(tn2):
        x = jnp.concatenate([x_ref[0, :, j, :], x_ref[1, :, j, :]], axis=0).astype(BF16)
        o_ref[:, :, j * c:(j + 1) * c] = _dot(m_ref[...], x).astype(BF16).reshape(two, n1, c)


def _fourier_a(ab, m1, *, tn2):
    b, g, two, n1, n2, c = ab.shape
    return pl.pallas_call(
        _fourier_a_kernel,
        grid=(b, g, n2 // tn2),
        in_specs=[
            _const_spec(m1.shape),
            pl.BlockSpec((None, None, two, n1, tn2, c), lambda bi, gi, ci: (bi, gi, 0, 0, ci, 0)),
        ],
        out_specs=pl.BlockSpec((None, None, two, n1, tn2 * c), lambda bi, gi, ci: (bi, gi, 0, 0, ci)),
        out_shape=jax.ShapeDtypeStruct((b, g, two, n1, n2 * c), BF16),
        compiler_params=_params("parallel", "parallel", "parallel"),
        name="fourier_a",
    )(m1, ab)


def _fourier_b_kernel(e_ref, g_ref, o_ref, *, scale):
    kb = e_ref.shape[0]
    for j in range(kb):
        x = jnp.concatenate([g_ref[0, j], g_ref[1, j]], axis=0)
        o_ref[:, j, :] = _dot(e_ref[j], x) * scale


def _fourier_b(gc, e, *, kb, scale):
    b, g, two, n1, n2, c = gc.shape
    return pl.pallas_call(
        functools.partial(_fourier_b_kernel, scale=scale),
        grid=(b, g, n1 // kb),
        in_specs=[
            pl.BlockSpec((kb, n2, 2 * n2), lambda bi, gi, ki: (ki, 0, 0)),
            pl.BlockSpec((None, None, two, kb, n2, c), lambda bi, gi, ki: (bi, gi, 0, ki, 0, 0)),
        ],
        out_specs=pl.BlockSpec((None, n2, kb, c), lambda bi, gi, ki: (bi, 0, ki, gi)),
        out_shape=jax.ShapeDtypeStruct((b, n2, n1, g * c), F32),
        compiler_params=_params("parallel", "parallel", "parallel"),
        name="fourier_b",
    )(e, gc)


def _attn_kernel(qt_ref, k_ref, vt_ref, o_ref, sa_ref, sb_ref, acc_ref, *, group_size):
    tk, tq = sa_ref.shape
    nk = vt_ref.shape[0]
    n_tiles = qt_ref.shape[1] // tq
    bufs = (sa_ref, sb_ref)

    def scores(tile, j, s_ref):
        start = pl.multiple_of(j * tk, tk)
        qt = qt_ref[:, tile * tq:(tile + 1) * tq]
        s = _dot(k_ref[pl.ds(start, tk), :], qt)
        s_ref[...] = s
        return jnp.max(s, axis=0, keepdims=True)

    def update(j, s_ref, cmax, m):
        m_new = jnp.maximum(m, cmax)
        p = jnp.exp2(s_ref[...] - m_new)
        acc_ref[...] = jnp.exp2(m - m_new) * acc_ref[...] + _dot(vt_ref[j], p.astype(BF16))
        return m_new

    def group(tile, j, cmax, m, last):
        for t in range(group_size):
            if not last or t + 1 < group_size:
                nxt = scores(tile, j + t + 1, bufs[(t + 1) % 2])
            elif tile + 1 < n_tiles:
                nxt = scores(tile + 1, 0, bufs[0])
            else:
                nxt = None
            m = update(j + t, bufs[t % 2], cmax, m)
            cmax = nxt
        return cmax, m

    acc_ref[...] = jnp.zeros_like(acc_ref)
    cmax = scores(0, 0, sa_ref)
    for tile in range(n_tiles):
        carry = (cmax, jnp.full((1, tq), -jnp.inf, F32))
        carry = lax.fori_loop(0, nk // group_size - 1,
                              lambda i, c, tile=tile: group(tile, group_size * i, *c, False), carry)
        cmax, _ = group(tile, nk - group_size, *carry, True)
        acc = acc_ref[...]
        o_ref[tile * tq:(tile + 1) * tq, :] = (acc[0:V_HEAD_DIM] / acc[V_HEAD_DIM:V_HEAD_DIM + 1]).T


def _attention(qt, k, vt, *, tq, tiles_per_step, group_size):
    b, h, s, _ = k.shape
    nk, _, tk = vt.shape[2:]
    assert group_size % 2 == 0 and nk % group_size == 0
    rows = tq * tiles_per_step
    return pl.pallas_call(
        functools.partial(_attn_kernel, group_size=group_size),
        grid=(b, h, s // rows),
        in_specs=[
            pl.BlockSpec((None, None, QK_DIM, rows), lambda bi, hi, i: (bi, hi, 0, i)),
            pl.BlockSpec((None, None, s, QK_DIM), lambda bi, hi, i: (bi, hi, 0, 0)),
            pl.BlockSpec((None, None, nk, V_EXT_DIM, tk), lambda bi, hi, i: (bi, hi, 0, 0, 0)),
        ],
        out_specs=pl.BlockSpec((None, rows, V_HEAD_DIM), lambda bi, hi, i: (bi, i, hi)),
        out_shape=jax.ShapeDtypeStruct((b, s, h * V_HEAD_DIM), F32),
        scratch_shapes=[pltpu.VMEM((tk, tq), F32), pltpu.VMEM((tk, tq), F32), pltpu.VMEM((V_EXT_DIM, tq), F32)],
        compiler_params=_params("parallel", "parallel", "parallel"),
        name="attention",
    )(qt, k, vt)


def _out_proj_kernel(x_ref, f_ref, a_ref, gf_ref, ga_ref, w_ref, o_ref):
    hf = _rms(f_ref[...], gf_ref[...]).astype(BF16)
    ha = _rms(a_ref[...], ga_ref[...]).astype(BF16)
    o_ref[...] = x_ref[...] + _dot(hf, w_ref[0:FOURIER_DIM, :]) + _dot(ha, w_ref[FOURIER_DIM:, :])


def _out_proj(x, f, a, g_f, g_a, w_out, *, tm):
    t = x.shape[0]
    return pl.pallas_call(
        _out_proj_kernel,
        grid=(t // tm,),
        in_specs=[
            pl.BlockSpec((tm, D_MODEL), lambda i: (i, 0)),
            pl.BlockSpec((tm, FOURIER_DIM), lambda i: (i, 0)),
            pl.BlockSpec((tm, MLA_DIM), lambda i: (i, 0)),
            _const_spec(g_f.shape),
            _const_spec(g_a.shape),
            _const_spec(w_out.shape),
        ],
        out_specs=pl.BlockSpec((tm, D_MODEL), lambda i: (i, 0)),
        out_shape=jax.ShapeDtypeStruct((t, D_MODEL), F32),
        compiler_params=_params("parallel"),
        name="out_proj",
    )(x, f, a, g_f, g_a, w_out)


def _mem_kv_kernel(m_ref, g_ref, w_ref, o_ref):
    o_ref[...] = _dot(_rms(m_ref[...], g_ref[...]).astype(BF16), w_ref[...].astype(BF16)).astype(BF16)


def _mem_kv(mem, g_mem, w_ckv, *, tn):
    b, m, _ = mem.shape
    n = w_ckv.shape[1]
    return pl.pallas_call(
        _mem_kv_kernel,
        grid=(n // tn, b),
        in_specs=[
            pl.BlockSpec((None, m, D_MODEL), lambda j, bi: (bi, 0, 0)),
            _const_spec(g_mem.shape),
            pl.BlockSpec((D_MODEL, tn), lambda j, bi: (0, j)),
        ],
        out_specs=pl.BlockSpec((None, m, tn), lambda j, bi: (bi, 0, j)),
        out_shape=jax.ShapeDtypeStruct((b, m, n), BF16),
        compiler_params=_params("arbitrary", "arbitrary"),
        name="mem_kv",
    )(mem, g_mem, w_ckv)


def _cross_fold_kernel(k_ref, v_ref, wq_ref, wo_ref, m_ref, n_ref):
    m_ref[...] = (_dot_nt(wq_ref[...], k_ref[...]) * (CROSS_HEAD_DIM ** -0.5)).astype(BF16)
    n_ref[...] = _dot(v_ref[...], wo_ref[...]).astype(BF16)


def _cross_fold(kvm, w_cq, w_co):
    b, m, _ = kvm.shape
    return pl.pallas_call(
        _cross_fold_kernel,
        grid=(N_CROSS_HEADS, b),
        in_specs=[
            pl.BlockSpec((None, m, CROSS_HEAD_DIM), lambda hd, bi: (bi, 0, hd)),
            pl.BlockSpec((None, m, CROSS_HEAD_DIM), lambda hd, bi: (bi, 0, N_CROSS_HEADS + hd)),
            pl.BlockSpec((D_MODEL, CROSS_HEAD_DIM), lambda hd, bi: (0, hd)),
            pl.BlockSpec((CROSS_HEAD_DIM, D_MODEL), lambda hd, bi: (hd, 0)),
        ],
        out_specs=[
            pl.BlockSpec((None, D_MODEL, m), lambda hd, bi: (bi, 0, hd)),
            pl.BlockSpec((None, m, D_MODEL), lambda hd, bi: (bi, hd, 0)),
        ],
        out_shape=[jax.ShapeDtypeStruct((b, D_MODEL, N_CROSS_HEADS * m), BF16),
                   jax.ShapeDtypeStruct((b, N_CROSS_HEADS * m, D_MODEL), BF16)],
        compiler_params=_params("arbitrary", "arbitrary"),
        name="cross_fold",
    )(kvm, kvm, w_cq, w_co)


def _cross_kernel(x_ref, m_ref, n_ref, gc_ref, gn_ref, o_ref, h_ref):
    x = x_ref[...]
    h = _rms(x, gc_ref[...]).astype(BF16)
    s = _dot(h, m_ref[...])
    n_mem = m_ref.shape[1] // N_CROSS_HEADS
    probs = []
    for hd in range(N_CROSS_HEADS):
        sh = s[:, hd * n_mem:(hd + 1) * n_mem]
        p = jnp.exp(sh - jnp.max(sh, axis=-1, keepdims=True))
        probs.append((p / jnp.sum(p, axis=-1, keepdims=True)).astype(BF16))
    y = x + _dot(jnp.concatenate(probs, axis=1), n_ref[...])
    o_ref[...] = y
    h_ref[...] = _rms(y, gn_ref[...]).astype(BF16)


def _cross(x, m_all, n_all, g_cross, g_ffn, *, tm):
    b, s, _ = x.shape
    spec = pl.BlockSpec((None, tm, D_MODEL), lambda bi, i: (bi, i, 0))
    return pl.pallas_call(
        _cross_kernel,
        grid=(b, s // tm),
        in_specs=[
            spec,
            pl.BlockSpec((None,) + m_all.shape[1:], lambda bi, i: (bi, 0, 0)),
            pl.BlockSpec((None,) + n_all.shape[1:], lambda bi, i: (bi, 0, 0)),
            _const_spec(g_cross.shape),
            _const_spec(g_ffn.shape),
        ],
        out_specs=[spec, spec],
        out_shape=[jax.ShapeDtypeStruct((b, s, D_MODEL), F32), jax.ShapeDtypeStruct((b, s, D_MODEL), BF16)],
        compiler_params=_params("parallel", "parallel"),
        name="cross",
    )(x, m_all, n_all, g_cross, g_ffn)


def _ffn_up_kernel(h_ref, prev_ref, next_ref, wg_ref, wu_ref, cwb_ref, o_ref, hext_ref, *, seq):
    i = pl.program_id(0)
    j = pl.program_id(1)
    tm = h_ref.shape[0]

    @pl.when(j == 0)
    def _():
        first = (i * tm) % seq == 0
        last = ((i + 1) * tm) % seq == 0
        prev = prev_ref[...]
        nxt = next_ref[...]
        hext_ref[0:HALO, :] = jnp.where(first, jnp.zeros_like(prev), prev)
        hext_ref[HALO:HALO + tm, :] = h_ref[...]
        hext_ref[HALO + tm:, :] = jnp.where(last, jnp.zeros_like(nxt), nxt)

    g = _dot(hext_ref[...], wg_ref[...].astype(BF16))
    u = _dot(h_ref[...], wu_ref[...].astype(BF16))
    rows = g.shape[0]
    g_prev = pltpu.roll(g, 1, axis=0)[HALO:HALO + tm]
    g_next = pltpu.roll(g, rows - 1, axis=0)[HALO:HALO + tm]
    cwb = cwb_ref[j]
    c = g_prev * cwb[0:1] + g[HALO:HALO + tm] * cwb[1:2] + g_next * cwb[2:3] + cwb[3:4]
    o_ref[...] = (c / (1.0 + jnp.exp(-c)) * u).astype(BF16)


def _ffn_up(h, w_gate, w_up, conv_w, conv_b, *, seq, tm, tf):
    t = h.shape[0]
    hb = tm // HALO
    n_halo = t // HALO
    cwb = jnp.concatenate([conv_w, conv_b], axis=0).reshape(4, D_FF // tf, tf).transpose(1, 0, 2)
    return pl.pallas_call(
        functools.partial(_ffn_up_kernel, seq=seq),
        grid=(t // tm, D_FF // tf),
        in_specs=[
            pl.BlockSpec((tm, D_MODEL), lambda i, j: (i, 0)),
            pl.BlockSpec((HALO, D_MODEL), lambda i, j: (jnp.maximum(i * hb - 1, 0), 0)),
            pl.BlockSpec((HALO, D_MODEL), lambda i, j: (jnp.minimum((i + 1) * hb, n_halo - 1), 0)),
            pl.BlockSpec((D_MODEL, tf), lambda i, j: (0, j)),
            pl.BlockSpec((D_MODEL, tf), lambda i, j: (0, j)),
            _const_spec(cwb.shape),
        ],
        out_specs=pl.BlockSpec((tm, tf), lambda i, j: (i, j)),
        out_shape=jax.ShapeDtypeStruct((t, D_FF), BF16),
        scratch_shapes=[pltpu.VMEM((tm + 2 * HALO, D_MODEL), BF16)],
        compiler_params=_params("parallel", "arbitrary"),
        name="ffn_up",
    )(h, h, h, w_gate, w_up, cwb)


def _ffn_down_kernel(a_ref, w_ref, x_ref, g_ref, o_ref):
    o_ref[...] = _rms(x_ref[...] + _dot(a_ref[...], w_ref[...]), g_ref[...])


def _ffn_down(act, w_down, x, g_final, *, tm):
    t = x.shape[0]
    return pl.pallas_call(
        _ffn_down_kernel,
        grid=(t // tm,),
        in_specs=[
            pl.BlockSpec((tm, D_FF), lambda i: (i, 0)),
            _const_spec(w_down.shape),
            pl.BlockSpec((tm, D_MODEL), lambda i: (i, 0)),
            _const_spec(g_final.shape),
        ],
        out_specs=pl.BlockSpec((tm, D_MODEL), lambda i: (i, 0)),
        out_shape=jax.ShapeDtypeStruct((t, D_MODEL), F32),
        compiler_params=_params("parallel"),
        name="ffn_down",
    )(act, w_down, x, g_final)


def _dft_tables(seq):
    n1 = DFT_N1
    n2 = seq // n1
    c = np.arange(FOURIER_GROUP_DIM)
    ang = 2.0 * np.pi * ((c[:, None] * c[None, :]) % FOURIER_GROUP_DIM) / FOURIER_GROUP_DIM
    dft_c = np.concatenate([np.cos(ang), np.sin(ang)], axis=1)

    j = np.arange(n1)
    a1 = 2.0 * np.pi * ((j[:, None] * j[None, :]) % n1) / n1
    c1, s1 = np.cos(a1), np.sin(a1)
    m1 = np.block([[c1, -s1], [-s1, -c1]])

    k1 = np.arange(n1)[:, None, None]
    k2 = np.arange(n2)[None, :, None]
    m2 = np.arange(n2)[None, None, :]
    a2 = 2.0 * np.pi * ((m2 * (k1 + n1 * k2)) % seq) / seq
    e = np.concatenate([np.cos(a2), np.sin(a2)], axis=2)
    return (jnp.asarray(dft_c, BF16), jnp.asarray(m1, BF16), jnp.asarray(e, BF16))


def _rope_tables(seq):
    inv = ROPE_THETA ** (-jnp.arange(0, QK_ROPE_DIM, 2, dtype=F32) / QK_ROPE_DIM)
    ang = jnp.arange(seq, dtype=F32)[:, None] * inv[None, :]
    cos, sin = jnp.cos(ang), jnp.sin(ang)
    return (jnp.concatenate([cos, cos, cos, cos], axis=1), jnp.concatenate([-sin, sin, -sin, sin], axis=1),
            jnp.concatenate([cos.T, cos.T], axis=0), jnp.concatenate([-sin.T, sin.T], axis=0))


def _prep_weights(w_in, w_uq, w_ukv):
    half = QK_ROPE_DIM // 2
    c1 = FOURIER_DIM + Q_LORA_RANK + KV_LORA_RANK
    w_in_ext = jnp.concatenate(
        [w_in, w_in[:, c1 + half:c1 + QK_ROPE_DIM], w_in[:, c1:c1 + half]], axis=1).astype(BF16)
    q3 = w_uq.reshape(Q_LORA_RANK, N_HEADS, QK_DIM)
    w_uqt_ext = jnp.concatenate(
        [q3[:, :, :QK_NOPE_DIM].reshape(Q_LORA_RANK, -1), q3[:, :, QK_NOPE_DIM:].reshape(Q_LORA_RANK, -1)],
        axis=1).T.astype(BF16)
    kv3 = w_ukv.reshape(KV_LORA_RANK, N_HEADS, QK_NOPE_DIM + V_HEAD_DIM)
    w_k = kv3[:, :, :QK_NOPE_DIM].reshape(KV_LORA_RANK, -1).astype(BF16)
    w_vt = kv3[:, :, QK_NOPE_DIM:].reshape(KV_LORA_RANK, -1).T.astype(BF16)
    return w_in_ext, w_uqt_ext, w_k, w_vt


def _tile(n, pref):
    return pref if n % pref == 0 else n


def _tiles(s):
    attn_k = _tile(s, 512)
    return dict(
        in_proj=_tile(s, 512),
        attn_q=_tile(s, 1024), attn_tiles=4 if s % 4096 == 0 else 1,
        attn_k=attn_k, attn_group=4 if (s // attn_k) % 4 == 0 else 2,
        fourier_tn2=min(16, s // DFT_N1), fourier_kb=32,
        out_proj=_tile(s, 512), mem_kv_cols=1024, cross=_tile(s, 512),
        ffn_up=_tile(s, 1024), ffn_cols=512, ffn_down=_tile(s, 256),
    )


def _trunk(x, mem, p, tables):
    b, s, _ = x.shape
    t = b * s
    n1 = DFT_N1
    n2 = s // n1
    ts = _tiles(s)
    dft_c, m1, e, cos2, sin2, cos_t, sin_t = tables

    ab, qt, k, vt = _in_proj(x, p["g_mix"], p["w_in_ext"], p["g_q"], p["w_uqt_ext"], p["g_kv"], p["w_k"],
                             p["w_vt"], dft_c, cos2, sin2, cos_t, sin_t, tm=ts["in_proj"], tk=ts["attn_k"])

    gc = _fourier_a(ab.reshape(b, N_FOURIER_GROUPS, 2, n1, n2, FOURIER_GROUP_DIM), m1, tn2=ts["fourier_tn2"])
    f = _fourier_b(gc.reshape(b, N_FOURIER_GROUPS, 2, n1, n2, FOURIER_GROUP_DIM), e, kb=ts["fourier_kb"],
                   scale=float((s * FOURIER_GROUP_DIM) ** -0.5))

    a = _attention(qt, k, vt, tq=ts["attn_q"], tiles_per_step=ts["attn_tiles"], group_size=ts["attn_group"])

    x1 = _out_proj(x.reshape(t, D_MODEL), f.reshape(t, FOURIER_DIM), a.reshape(t, MLA_DIM),
                   p["g_f"], p["g_a"], p["w_out"], tm=ts["out_proj"])

    kvm = _mem_kv(mem, p["g_mem"], p["w_ckv"], tn=ts["mem_kv_cols"])
    m_all, n_all = _cross_fold(kvm, p["w_cq"], p["w_co"])
    x2, hf = _cross(x1.reshape(b, s, D_MODEL), m_all, n_all, p["g_cross"], p["g_ffn"], tm=ts["cross"])

    act = _ffn_up(hf.reshape(t, D_MODEL), p["w_gate"], p["w_up"], p["conv_w"], p["conv_b"], seq=s,
                  tm=ts["ffn_up"], tf=ts["ffn_cols"])
    y = _ffn_down(act, p["w_down"], x2.reshape(t, D_MODEL), p["g_final"], tm=ts["ffn_down"])
    return y.reshape(b, s, D_MODEL)


@jax.jit
def kernel(x_prompt, x_sample, mem_prompt, mem_sample, norm_mix_g, w_in, q_norm_g, w_uq, kv_norm_g, w_ukv,
           fourier_out_g, mla_out_g, w_out, norm_cross_g, norm_mem_g, w_cq, w_ckv, w_co, norm_ffn_g, w_gate,
           w_up, conv_w, conv_b, w_down, final_norm_g):
    assert norm_mix_g.shape[0] == 1, "single-layer trunk"
    assert x_prompt.shape[1] % DFT_N1 == 0 and x_sample.shape[1] % DFT_N1 == 0
    w_in_ext, w_uqt_ext, w_k, w_vt = _prep_weights(w_in[0], w_uq[0], w_ukv[0])
    row = lambda g: g.reshape(1, -1)
    p = dict(
        g_mix=row(norm_mix_g[0]), w_in_ext=w_in_ext, g_q=row(q_norm_g[0]), w_uqt_ext=w_uqt_ext,
        g_kv=row(kv_norm_g[0]), w_k=w_k, w_vt=w_vt, g_f=row(fourier_out_g[0]), g_a=row(mla_out_g[0]),
        w_out=w_out[0].astype(BF16), g_cross=row(norm_cross_g[0]), g_mem=row(norm_mem_g[0]),
        w_cq=w_cq[0].astype(BF16), w_ckv=w_ckv.reshape(w_ckv.shape[1:]), w_co=w_co[0].astype(BF16),
        g_ffn=row(norm_ffn_g[0]), w_gate=w_gate.reshape(w_gate.shape[1:]), w_up=w_up.reshape(w_up.shape[1:]),
        conv_w=conv_w[0], conv_b=row(conv_b[0]), w_down=w_down[0].astype(BF16), g_final=row(final_norm_g),
    )
    outs = []
    for x, mem in ((x_prompt, mem_prompt), (x_sample, mem_sample)):
        s = x.shape[1]
        tables = _dft_tables(s) + _rope_tables(s)
        outs.append(_trunk(x, mem, p, tables))
    return tuple(outs)
```

```python
import functools

import numpy as np
import jax
import jax.numpy as jnp
from jax import lax
from jax.experimental import pallas as pl
from jax.experimental.pallas import tpu as pltpu

F32 = jnp.float32
BF16 = jnp.bfloat16

D_MODEL = 2048
N_FOURIER_GROUPS = 4
FOURIER_GROUP_DIM = 256
FOURIER_DIM = N_FOURIER_GROUPS * FOURIER_GROUP_DIM
N_HEADS = 8
QK_NOPE_DIM = 128
QK_ROPE_DIM = 64
QK_DIM = QK_NOPE_DIM + QK_ROPE_DIM
V_HEAD_DIM = 128
V_EXT_DIM = V_HEAD_DIM + 16
Q_LORA_RANK = 512
KV_LORA_RANK = 512
MLA_DIM = N_HEADS * V_HEAD_DIM
ROPE_THETA = 10000.0
N_CROSS_HEADS = 4
CROSS_HEAD_DIM = D_MODEL // N_CROSS_HEADS
D_FF = 5632
EPS = 1e-6
LOG2_E = 1.4426950408889634

V7X_VMEM_LIMIT_BYTES = 56 * 1024 * 1024
MXU_DIM = 256
DFT_N1 = MXU_DIM // 2
HALO = 16


def _params(*sem):
    return pltpu.CompilerParams(dimension_semantics=sem, vmem_limit_bytes=V7X_VMEM_LIMIT_BYTES)


def _const_spec(shape):
    zeros = (0,) * len(shape)
    return pl.BlockSpec(shape, lambda *_: zeros, pipeline_mode=pl.Buffered(1))


def _rms(x, g):
    return x * lax.rsqrt(jnp.mean(x * x, axis=-1, keepdims=True) + EPS) * g


def _dot(a, b):
    return jnp.dot(a, b, preferred_element_type=F32)


def _dot_nt(a, b):
    return lax.dot_general(a, b, (((1,), (1,)), ((), ())), preferred_element_type=F32)


def _in_proj_kernel(x_ref, g_ref, win_ref, qg_ref, wuqt_ref, kvg_ref, wk_ref, wvt_ref, dft_ref, cos_ref,
                    sin_ref, cost_ref, sint_ref, ab_ref, qt_ref, k_ref, vt_ref):
    h = _rms(x_ref[...], g_ref[...]).astype(BF16)
    z = _dot(h, win_ref[...])

    for g in range(N_FOURIER_GROUPS):
        u = z[:, g * FOURIER_GROUP_DIM:(g + 1) * FOURIER_GROUP_DIM].astype(BF16)
        ab = _dot(u, dft_ref[...])
        ab_ref[g, 0] = ab[:, :FOURIER_GROUP_DIM]
        ab_ref[g, 1] = ab[:, FOURIER_GROUP_DIM:]

    c0 = FOURIER_DIM
    c_q = z[:, c0:c0 + Q_LORA_RANK]
    c_kv = z[:, c0 + Q_LORA_RANK:c0 + Q_LORA_RANK + KV_LORA_RANK]
    c1 = c0 + Q_LORA_RANK + KV_LORA_RANK
    k_r = z[:, c1:c1 + QK_ROPE_DIM]
    k_r_sw = z[:, c1 + QK_ROPE_DIM:c1 + 2 * QK_ROPE_DIM]

    cos = cos_ref[...]
    sin = sin_ref[...]
    k_rope = (k_r * cos[:, :QK_ROPE_DIM] + k_r_sw * sin[:, :QK_ROPE_DIM]).astype(BF16)

    hq = _rms(c_q, qg_ref[...]).astype(BF16)
    qe_t = _dot_nt(wuqt_ref[...], hq)
    q_nope_w = N_HEADS * QK_NOPE_DIM
    cos_t = jnp.concatenate([cost_ref[...]] * N_HEADS, axis=0)
    sin_t = jnp.concatenate([sint_ref[...]] * N_HEADS, axis=0)
    q_r = qe_t[q_nope_w:]
    half = QK_ROPE_DIM // 2
    q_sw = jnp.concatenate(
        [q_r[lo:lo + half] for hd in range(N_HEADS) for lo in (hd * QK_ROPE_DIM + half, hd * QK_ROPE_DIM)], axis=0)
    q_rope_t = q_r * cos_t + q_sw * sin_t

    hkv = _rms(c_kv, kvg_ref[...]).astype(BF16)
    k_nope = _dot(hkv, wk_ref[...])
    v_t = _dot_nt(wvt_ref[...], hkv)

    scale = QK_DIM ** -0.5 * LOG2_E
    for hd in range(N_HEADS):
        qt_ref[hd, 0:QK_NOPE_DIM, :] = (qe_t[hd * QK_NOPE_DIM:(hd + 1) * QK_NOPE_DIM] * scale).astype(BF16)
        qt_ref[hd, QK_NOPE_DIM:QK_DIM, :] = (
            q_rope_t[hd * QK_ROPE_DIM:(hd + 1) * QK_ROPE_DIM] * scale).astype(BF16)
        k_ref[hd, :, 0:QK_NOPE_DIM] = k_nope[:, hd * QK_NOPE_DIM:(hd + 1) * QK_NOPE_DIM].astype(BF16)
        k_ref[hd, :, QK_NOPE_DIM:QK_DIM] = k_rope
        vt_ref[hd, 0:V_HEAD_DIM, :] = v_t[hd * V_HEAD_DIM:(hd + 1) * V_HEAD_DIM, :].astype(BF16)
        pad_row = lax.broadcasted_iota(jnp.int32, (V_EXT_DIM - V_HEAD_DIM, v_t.shape[1]), 0)
        vt_ref[hd, V_HEAD_DIM:, :] = jnp.where(pad_row == 0, 1.0, 0.0).astype(BF16)


def _in_proj(x, g_mix, w_in_ext, q_g, w_uqt_ext, kv_g, w_k, w_vt, dft_c, cos2, sin2, cos_t, sin_t, *, tm, tk):
    b, s, _ = x.shape
    grid = (b, s // tm)
    r = tk // tm
    return pl.pallas_call(
        _in_proj_kernel,
        grid=grid,
        in_specs=[
            pl.BlockSpec((None, tm, D_MODEL), lambda bi, i: (bi, i, 0)),
            _const_spec(g_mix.shape),
            _const_spec(w_in_ext.shape),
            _const_spec(q_g.shape),
            _const_spec(w_uqt_ext.shape),
            _const_spec(kv_g.shape),
            _const_spec(w_k.shape),
            _const_spec(w_vt.shape),
            _const_spec(dft_c.shape),
            pl.BlockSpec((tm, 128), lambda bi, i: (i, 0)),
            pl.BlockSpec((tm, 128), lambda bi, i: (i, 0)),
            pl.BlockSpec((QK_ROPE_DIM, tm), lambda bi, i: (0, i)),
            pl.BlockSpec((QK_ROPE_DIM, tm), lambda bi, i: (0, i)),
        ],
        out_specs=[
            pl.BlockSpec((None, N_FOURIER_GROUPS, 2, tm, FOURIER_GROUP_DIM), lambda bi, i: (bi, 0, 0, i, 0)),
            pl.BlockSpec((None, N_HEADS, QK_DIM, tm), lambda bi, i: (bi, 0, 0, i)),
            pl.BlockSpec((None, N_HEADS, tm, QK_DIM), lambda bi, i: (bi, 0, i, 0)),
            pl.BlockSpec((None, N_HEADS, None, V_EXT_DIM, tm), lambda bi, i: (bi, 0, i // r, 0, i % r)),
        ],
        out_shape=[
            jax.ShapeDtypeStruct((b, N_FOURIER_GROUPS, 2, s, FOURIER_GROUP_DIM), F32),
            jax.ShapeDtypeStruct((b, N_HEADS, QK_DIM, s), BF16),
            jax.ShapeDtypeStruct((b, N_HEADS, s, QK_DIM), BF16),
            jax.ShapeDtypeStruct((b, N_HEADS, s // tk, V_EXT_DIM, tk), BF16),
        ],
        compiler_params=_params("parallel", "parallel"),
        name="in_proj",
    )(x, g_mix, w_in_ext, q_g, w_uqt_ext, kv_g, w_k, w_vt, dft_c, cos2, sin2, cos_t, sin_t)


def _fourier_a_kernel(m_ref, x_ref, o_ref):
    two, n1, tn2, c = x_ref.shape
    for j in range(tn2):
        x = jnp.concatenate([x_ref[0, :, j, :], x_ref[1, :, j, :]], axis=0).astype(BF16)
        o_ref[:, :, j * c:(j + 1) * c] = _dot(m_ref[...], x).astype(BF16).reshape(two, n1, c)


def _fourier_a(ab, m1, *, tn2):
    b, g, two, n1, n2, c = ab.shape
    return pl.pallas_call(
        _fourier_a_kernel,
        grid=(b, g, n2 // tn2),
        in_specs=[
            _const_spec(m1.shape),
            pl.BlockSpec((None, None, two, n1, tn2, c), lambda bi, gi, ci: (bi, gi, 0, 0, ci, 0)),
        ],
        out_specs=pl.BlockSpec((None, None, two, n1, tn2 * c), lambda bi, gi, ci: (bi, gi, 0, 0, ci)),
        out_shape=jax.ShapeDtypeStruct((b, g, two, n1, n2 * c), BF16),
        compiler_params=_params("parallel", "parallel", "parallel"),
        name="fourier_a",
    )(m1, ab)


def _fourier_b_kernel(e_ref, g_ref, o_ref, *, scale):
    kb = e_ref.shape[0]
    for j in range(kb):
        x = jnp.concatenate([g_ref[0, j], g_ref[1, j]], axis=0)
        o_ref[:, j, :] = _dot(e_ref[j], x) * scale


def _fourier_b(gc, e, *, kb, scale):
    b, g, two, n1, n2, c = gc.shape
    return pl.pallas_call(
        functools.partial(_fourier_b_kernel, scale=scale),
        grid=(b, g, n1 // kb),
        in_specs=[
            pl.BlockSpec((kb, n2, 2 * n2), lambda bi, gi, ki: (ki, 0, 0)),
            pl.BlockSpec((None, None, two, kb, n2, c), lambda bi, gi, ki: (bi, gi, 0, ki, 0, 0)),
        ],
        out_specs=pl.BlockSpec((None, n2, kb, c), lambda bi, gi, ki: (bi, 0, ki, gi)),
        out_shape=jax.ShapeDtypeStruct((b, n2, n1, g * c), F32),
        compiler_params=_params("parallel", "parallel", "parallel"),
        name="fourier_b",
    )(e, gc)


def _attn_kernel(qt_ref, k_ref, vt_ref, o_ref, sa_ref, sb_ref, acc_ref, *, group_size):
    tk, tq = sa_ref.shape
    nk = vt_ref.shape[0]
    n_tiles = qt_ref.shape[1] // tq
    bufs = (sa_ref, sb_ref)

    def scores(tile, j, s_ref):
        start = pl.multiple_of(j * tk, tk)
        qt = qt_ref[:, tile * tq:(tile + 1) * tq]
        s = _dot(k_ref[pl.ds(start, tk), :], qt)
        s_ref[...] = s
        return jnp.max(s, axis=0, keepdims=True)

    def update(j, s_ref, cmax, m):
        m_new = jnp.maximum(m, cmax)
        p = jnp.exp2(s_ref[...] - m_new)
        acc_ref[...] = jnp.exp2(m - m_new) * acc_ref[...] + _dot(vt_ref[j], p.astype(BF16))
        return m_new

    def group(tile, j, cmax, m, last):
        for t in range(group_size):
            if not last or t + 1 < group_size:
                nxt = scores(tile, j + t + 1, bufs[(t + 1) % 2])
            elif tile + 1 < n_tiles:
                nxt = scores(tile + 1, 0, bufs[0])
            else:
                nxt = None
            m = update(j + t, bufs[t % 2], cmax, m)
            cmax = nxt
        return cmax, m

    acc_ref[...] = jnp.zeros_like(acc_ref)
    cmax = scores(0, 0, sa_ref)
    for tile in range(n_tiles):
        carry = (cmax, jnp.full((1, tq), -jnp.inf, F32))
        carry = lax.fori_loop(0, nk // group_size - 1,
                              lambda i, c, tile=tile: group(tile, group_size * i, *c, False), carry)
        cmax, _ = group(tile, nk - group_size, *carry, True)
        acc = acc_ref[...]
        o_ref[tile * tq:(tile + 1) * tq, :] = (acc[0:V_HEAD_DIM] / acc[V_HEAD_DIM:V_HEAD_DIM + 1]).T


def _attention(qt, k, vt, *, tq, tiles_per_step, group_size):
    b, h, s, _ = k.shape
    nk, _, tk = vt.shape[2:]
    assert group_size % 2 == 0 and nk % group_size == 0
    rows = tq * tiles_per_step
    return pl.pallas_call(
        functools.partial(_attn_kernel, group_size=group_size),
        grid=(b, h, s // rows),
        in_specs=[
            pl.BlockSpec((None, None, QK_DIM, rows), lambda bi, hi, i: (bi, hi, 0, i)),
            pl.BlockSpec((None, None, s, QK_DIM), lambda bi, hi, i: (bi, hi, 0, 0)),
            pl.BlockSpec((None, None, nk, V_EXT_DIM, tk), lambda bi, hi, i: (bi, hi, 0, 0, 0)),
        ],
        out_specs=pl.BlockSpec((None, rows, V_HEAD_DIM), lambda bi, hi, i: (bi, i, hi)),
        out_shape=jax.ShapeDtypeStruct((b, s, h * V_HEAD_DIM), F32),
        scratch_shapes=[pltpu.VMEM((tk, tq), F32), pltpu.VMEM((tk, tq), F32), pltpu.VMEM((V_EXT_DIM, tq), F32)],
        compiler_params=_params("parallel", "parallel", "parallel"),
        name="attention",
    )(qt, k, vt)


def _out_proj_kernel(x_ref, f_ref, a_ref, gf_ref, ga_ref, w_ref, o_ref):
    hf = _rms(f_ref[...], gf_ref[...]).astype(BF16)
    ha = _rms(a_ref[...], ga_ref[...]).astype(BF16)
    o_ref[...] = x_ref[...] + _dot(hf, w_ref[0:FOURIER_DIM, :]) + _dot(ha, w_ref[FOURIER_DIM:, :])


def _out_proj(x, f, a, g_f, g_a, w_out, *, tm):
    t = x.shape[0]
    return pl.pallas_call(
        _out_proj_kernel,
        grid=(t // tm,),
        in_specs=[
            pl.BlockSpec((tm, D_MODEL), lambda i: (i, 0)),
            pl.BlockSpec((tm, FOURIER_DIM), lambda i: (i, 0)),
            pl.BlockSpec((tm, MLA_DIM), lambda i: (i, 0)),
            _const_spec(g_f.shape),
            _const_spec(g_a.shape),
            _const_spec(w_out.shape),
        ],
        out_specs=pl.BlockSpec((tm, D_MODEL), lambda i: (i, 0)),
        out_shape=jax.ShapeDtypeStruct((t, D_MODEL), F32),
        compiler_params=_params("parallel"),
        name="out_proj",
    )(x, f, a, g_f, g_a, w_out)


def _mem_kv_kernel(m_ref, g_ref, w_ref, o_ref, wb_ref):
    @pl.when(pl.program_id(1) == 0)
    def _():
        wb_ref[...] = w_ref[...].astype(BF16)

    o_ref[...] = _dot(_rms(m_ref[...], g_ref[...]).astype(BF16), wb_ref[...]).astype(BF16)


def _mem_kv(mem, g_mem, w_ckv, *, tn):
    b, m, _ = mem.shape
    n = w_ckv.shape[1]
    return pl.pallas_call(
        _mem_kv_kernel,
        grid=(n // tn, b),
        in_specs=[
            pl.BlockSpec((None, m, D_MODEL), lambda j, bi: (bi, 0, 0)),
            _const_spec(g_mem.shape),
            pl.BlockSpec((D_MODEL, tn), lambda j, bi: (0, j)),
        ],
        out_specs=pl.BlockSpec((None, m, tn), lambda j, bi: (bi, 0, j)),
        out_shape=jax.ShapeDtypeStruct((b, m, n), BF16),
        scratch_shapes=[pltpu.VMEM((D_MODEL, tn), BF16)],
        compiler_params=_params("arbitrary", "arbitrary"),
        name="mem_kv",
    )(mem, g_mem, w_ckv)


def _cross_fold_kernel(k_ref, v_ref, wq_ref, wo_ref, m_ref, n_ref):
    m_ref[...] = (_dot_nt(wq_ref[...], k_ref[...]) * (CROSS_HEAD_DIM ** -0.5)).astype(BF16)
    n_ref[...] = _dot(v_ref[...], wo_ref[...]).astype(BF16)


def _cross_fold(kvm, w_cq, w_co):
    b, m, _ = kvm.shape
    return pl.pallas_call(
        _cross_fold_kernel,
        grid=(N_CROSS_HEADS, b),
        in_specs=[
            pl.BlockSpec((None, m, CROSS_HEAD_DIM), lambda hd, bi: (bi, 0, hd)),
            pl.BlockSpec((None, m, CROSS_HEAD_DIM), lambda hd, bi: (bi, 0, N_CROSS_HEADS + hd)),
            pl.BlockSpec((D_MODEL, CROSS_HEAD_DIM), lambda hd, bi: (0, hd)),
            pl.BlockSpec((CROSS_HEAD_DIM, D_MODEL), lambda hd, bi: (hd, 0)),
        ],
        out_specs=[
            pl.BlockSpec((None, D_MODEL, m), lambda hd, bi: (bi, 0, hd)),
            pl.BlockSpec((None, m, D_MODEL), lambda hd, bi: (bi, hd, 0)),
        ],
        out_shape=[jax.ShapeDtypeStruct((b, D_MODEL, N_CROSS_HEADS * m), BF16),
                   jax.ShapeDtypeStruct((b, N_CROSS_HEADS * m, D_MODEL), BF16)],
        compiler_params=_params("arbitrary", "arbitrary"),
        name="cross_fold",
    )(kvm, kvm, w_cq, w_co)


def _cross_kernel(x_ref, m_ref, n_ref, gc_ref, gn_ref, o_ref, h_ref):
    x = x_ref[...]
    h = _rms(x, gc_ref[...]).astype(BF16)
    s = _dot(h, m_ref[...])
    n_mem = m_ref.shape[1] // N_CROSS_HEADS
    probs = []
    for hd in range(N_CROSS_HEADS):
        sh = s[:, hd * n_mem:(hd + 1) * n_mem]
        p = jnp.exp(sh - jnp.max(sh, axis=-1, keepdims=True))
        probs.append((p / jnp.sum(p, axis=-1, keepdims=True)).astype(BF16))
    y = x + _dot(jnp.concatenate(probs, axis=1), n_ref[...])
    o_ref[...] = y
    h_ref[...] = _rms(y, gn_ref[...]).astype(BF16)


def _cross(x, m_all, n_all, g_cross, g_ffn, *, tm):
    b, s, _ = x.shape
    spec = pl.BlockSpec((None, tm, D_MODEL), lambda bi, i: (bi, i, 0))
    return pl.pallas_call(
        _cross_kernel,
        grid=(b, s // tm),
        in_specs=[
            spec,
            pl.BlockSpec((None,) + m_all.shape[1:], lambda bi, i: (bi, 0, 0)),
            pl.BlockSpec((None,) + n_all.shape[1:], lambda bi, i: (bi, 0, 0)),
            _const_spec(g_cross.shape),
            _const_spec(g_ffn.shape),
        ],
        out_specs=[spec, spec],
        out_shape=[jax.ShapeDtypeStruct((b, s, D_MODEL), F32), jax.ShapeDtypeStruct((b, s, D_MODEL), BF16)],
        compiler_params=_params("parallel", "parallel"),
        name="cross",
    )(x, m_all, n_all, g_cross, g_ffn)


def _ffn_up_kernel(h_ref, prev_ref, next_ref, wg_ref, wu_ref, cwb_ref, o_ref, hext_ref, *, seq):
    i = pl.program_id(0)
    j = pl.program_id(1)
    tm = h_ref.shape[0]

    @pl.when(j == 0)
    def _():
        first = (i * tm) % seq == 0
        last = ((i + 1) * tm) % seq == 0
        prev = prev_ref[...]
        nxt = next_ref[...]
        hext_ref[0:HALO, :] = jnp.where(first, jnp.zeros_like(prev), prev)
        hext_ref[HALO:HALO + tm, :] = h_ref[...]
        hext_ref[HALO + tm:, :] = jnp.where(last, jnp.zeros_like(nxt), nxt)

    g = _dot(hext_ref[...], wg_ref[...].astype(BF16))
    u = _dot(h_ref[...], wu_ref[...].astype(BF16))
    rows = g.shape[0]
    g_prev = pltpu.roll(g, 1, axis=0)[HALO:HALO + tm]
    g_next = pltpu.roll(g, rows - 1, axis=0)[HALO:HALO + tm]
    cwb = cwb_ref[j]
    c = g_prev * cwb[0:1] + g[HALO:HALO + tm] * cwb[1:2] + g_next * cwb[2:3] + cwb[3:4]
    o_ref[...] = (c / (1.0 + jnp.exp(-c)) * u).astype(BF16)


def _ffn_up(h, w_gate, w_up, conv_w, conv_b, *, seq, tm, tf):
    t = h.shape[0]
    hb = tm // HALO
    n_halo = t // HALO
    cwb = jnp.concatenate([conv_w, conv_b], axis=0).reshape(4, D_FF // tf, tf).transpose(1, 0, 2)
    return pl.pallas_call(
        functools.partial(_ffn_up_kernel, seq=seq),
        grid=(t // tm, D_FF // tf),
        in_specs=[
            pl.BlockSpec((tm, D_MODEL), lambda i, j: (i, 0)),
            pl.BlockSpec((HALO, D_MODEL), lambda i, j: (jnp.maximum(i * hb - 1, 0), 0)),
            pl.BlockSpec((HALO, D_MODEL), lambda i, j: (jnp.minimum((i + 1) * hb, n_halo - 1), 0)),
            pl.BlockSpec((D_MODEL, tf), lambda i, j: (0, j)),
            pl.BlockSpec((D_MODEL, tf), lambda i, j: (0, j)),
            _const_spec(cwb.shape),
        ],
        out_specs=pl.BlockSpec((tm, tf), lambda i, j: (i, j)),
        out_shape=jax.ShapeDtypeStruct((t, D_FF), BF16),
        scratch_shapes=[pltpu.VMEM((tm + 2 * HALO, D_MODEL), BF16)],
        compiler_params=_params("parallel", "arbitrary"),
        name="ffn_up",
    )(h, h, h, w_gate, w_up, cwb)


def _ffn_down_kernel(a_ref, w_ref, x_ref, g_ref, o_ref):
    o_ref[...] = _rms(x_ref[...] + _dot(a_ref[...], w_ref[...]), g_ref[...])


def _ffn_down(act, w_down, x, g_final, *, tm):
    t = x.shape[0]
    return pl.pallas_call(
        _ffn_down_kernel,
        grid=(t // tm,),
        in_specs=[
            pl.BlockSpec((tm, D_FF), lambda i: (i, 0)),
            _const_spec(w_down.shape),
            pl.BlockSpec((tm, D_MODEL), lambda i: (i, 0)),
            _const_spec(g_final.shape),
        ],
        out_specs=pl.BlockSpec((tm, D_MODEL), lambda i: (i, 0)),
        out_shape=jax.ShapeDtypeStruct((t, D_MODEL), F32),
        compiler_params=_params("parallel"),
        name="ffn_down",
    )(act, w_down, x, g_final)


def _dft_tables(seq):
    n1 = DFT_N1
    n2 = seq // n1
    c = np.arange(FOURIER_GROUP_DIM)
    ang = 2.0 * np.pi * ((c[:, None] * c[None, :]) % FOURIER_GROUP_DIM) / FOURIER_GROUP_DIM
    dft_c = np.concatenate([np.cos(ang), np.sin(ang)], axis=1)

    j = np.arange(n1)
    a1 = 2.0 * np.pi * ((j[:, None] * j[None, :]) % n1) / n1
    c1, s1 = np.cos(a1), np.sin(a1)
    m1 = np.block([[c1, -s1], [-s1, -c1]])

    k1 = np.arange(n1)[:, None, None]
    k2 = np.arange(n2)[None, :, None]
    m2 = np.arange(n2)[None, None, :]
    a2 = 2.0 * np.pi * ((m2 * (k1 + n1 * k2)) % seq) / seq
    e = np.concatenate([np.cos(a2), np.sin(a2)], axis=2)
    return (jnp.asarray(dft_c, BF16), jnp.asarray(m1, BF16), jnp.asarray(e, BF16))


def _rope_tables(seq):
    inv = ROPE_THETA ** (-jnp.arange(0, QK_ROPE_DIM, 2, dtype=F32) / QK_ROPE_DIM)
    ang = jnp.arange(seq, dtype=F32)[:, None] * inv[None, :]
    cos, sin = jnp.cos(ang), jnp.sin(ang)
    return (jnp.concatenate([cos, cos, cos, cos], axis=1), jnp.concatenate([-sin, sin, -sin, sin], axis=1),
            jnp.concatenate([cos.T, cos.T], axis=0), jnp.concatenate([-sin.T, sin.T], axis=0))


def _prep_weights(w_in, w_uq, w_ukv):
    half = QK_ROPE_DIM // 2
    c1 = FOURIER_DIM + Q_LORA_RANK + KV_LORA_RANK
    w_in_ext = jnp.concatenate(
        [w_in, w_in[:, c1 + half:c1 + QK_ROPE_DIM], w_in[:, c1:c1 + half]], axis=1).astype(BF16)
    q3 = w_uq.reshape(Q_LORA_RANK, N_HEADS, QK_DIM)
    w_uqt_ext = jnp.concatenate(
        [q3[:, :, :QK_NOPE_DIM].reshape(Q_LORA_RANK, -1), q3[:, :, QK_NOPE_DIM:].reshape(Q_LORA_RANK, -1)],
        axis=1).T.astype(BF16)
    kv3 = w_ukv.reshape(KV_LORA_RANK, N_HEADS, QK_NOPE_DIM + V_HEAD_DIM)
    w_k = kv3[:, :, :QK_NOPE_DIM].reshape(KV_LORA_RANK, -1).astype(BF16)
    w_vt = kv3[:, :, QK_NOPE_DIM:].reshape(KV_LORA_RANK, -1).T.astype(BF16)
    return w_in_ext, w_uqt_ext, w_k, w_vt


def _tile(n, pref):
    return pref if n % pref == 0 else n


def _tiles(s):
    attn_k = _tile(s, 512)
    return dict(
        in_proj=_tile(s, 512),
        attn_q=_tile(s, 1024), attn_tiles=4 if s % 4096 == 0 else 1,
        attn_k=attn_k, attn_group=4 if (s // attn_k) % 4 == 0 else 2,
        fourier_tn2=min(16, s // DFT_N1), fourier_kb=32,
        out_proj=_tile(s, 512), mem_kv_cols=1024, cross=_tile(s, 512),
        ffn_up=_tile(s, 1024), ffn_cols=512, ffn_down=_tile(s, 256),
    )


def _trunk(x, mem, p, tables):
    b, s, _ = x.shape
    t = b * s
    n1 = DFT_N1
    n2 = s // n1
    ts = _tiles(s)
    dft_c, m1, e, cos2, sin2, cos_t, sin_t = tables

    ab, qt, k, vt = _in_proj(x, p["g_mix"], p["w_in_ext"], p["g_q"], p["w_uqt_ext"], p["g_kv"], p["w_k"],
                             p["w_vt"], dft_c, cos2, sin2, cos_t, sin_t, tm=ts["in_proj"], tk=ts["attn_k"])

    gc = _fourier_a(ab.reshape(b, N_FOURIER_GROUPS, 2, n1, n2, FOURIER_GROUP_DIM), m1, tn2=ts["fourier_tn2"])
    f = _fourier_b(gc.reshape(b, N_FOURIER_GROUPS, 2, n1, n2, FOURIER_GROUP_DIM), e, kb=ts["fourier_kb"],
                   scale=float((s * FOURIER_GROUP_DIM) ** -0.5))

    a = _attention(qt, k, vt, tq=ts["attn_q"], tiles_per_step=ts["attn_tiles"], group_size=ts["attn_group"])

    x1 = _out_proj(x.reshape(t, D_MODEL), f.reshape(t, FOURIER_DIM), a.reshape(t, MLA_DIM),
                   p["g_f"], p["g_a"], p["w_out"], tm=ts["out_proj"])

    kvm = _mem_kv(mem, p["g_mem"], p["w_ckv"], tn=ts["mem_kv_cols"])
    m_all, n_all = _cross_fold(kvm, p["w_cq"], p["w_co"])
    x2, hf = _cross(x1.reshape(b, s, D_MODEL), m_all, n_all, p["g_cross"], p["g_ffn"], tm=ts["cross"])

    act = _ffn_up(hf.reshape(t, D_MODEL), p["w_gate"], p["w_up"], p["conv_w"], p["conv_b"], seq=s,
                  tm=ts["ffn_up"], tf=ts["ffn_cols"])
    y = _ffn_down(act, p["w_down"], x2.reshape(t, D_MODEL), p["g_final"], tm=ts["ffn_down"])
    return y.reshape(b, s, D_MODEL)


@jax.jit
def kernel(x_prompt, x_sample, mem_prompt, mem_sample, norm_mix_g, w_in, q_norm_g, w_uq, kv_norm_g, w_ukv,
           fourier_out_g, mla_out_g, w_out, norm_cross_g, norm_mem_g, w_cq, w_ckv, w_co, norm_ffn_g, w_gate,
           w_up, conv_w, conv_b, w_down, final_norm_g):
    assert norm_mix_g.shape[0] == 1, "single-layer trunk"
    assert x_prompt.shape[1] % DFT_N1 == 0 and x_sample.shape[1] % DFT_N1 == 0
    w_in_ext, w_uqt_ext, w_k, w_vt = _prep_weights(w_in[0], w_uq[0], w_ukv[0])
    row = lambda g: g.reshape(1, -1)
    p = dict(
        g_mix=row(norm_mix_g[0]), w_in_ext=w_in_ext, g_q=row(q_norm_g[0]), w_uqt_ext=w_uqt_ext,
        g_kv=row(kv_norm_g[0]), w_k=w_k, w_vt=w_vt, g_f=row(fourier_out_g[0]), g_a=row(mla_out_g[0]),
        w_out=w_out[0].astype(BF16), g_cross=row(norm_cross_g[0]), g_mem=row(norm_mem_g[0]),
        w_cq=w_cq[0].astype(BF16), w_ckv=w_ckv.reshape(w_ckv.shape[1:]), w_co=w_co[0].astype(BF16),
        g_ffn=row(norm_ffn_g[0]), w_gate=w_gate.reshape(w_gate.shape[1:]), w_up=w_up.reshape(w_up.shape[1:]),
        conv_w=conv_w[0], conv_b=row(conv_b[0]), w_down=w_down[0].astype(BF16), g_final=row(final_norm_g),
    )
    outs = []
    for x, mem in ((x_prompt, mem_prompt), (x_sample, mem_sample)):
        s = x.shape[1]
        tables = _dft_tables(s) + _rope_tables(s)
        outs.append(_trunk(x, mem, p, tables))
    return tuple(outs)
```

```python
import functools

import numpy as np
import jax
import jax.numpy as jnp
from jax import lax
from jax.experimental import pallas as pl
from jax.experimental.pallas import tpu as pltpu

F32 = jnp.float32
BF16 = jnp.bfloat16

D_MODEL = 2048
N_FOURIER_GROUPS = 4
FOURIER_GROUP_DIM = 256
FOURIER_DIM = N_FOURIER_GROUPS * FOURIER_GROUP_DIM
N_HEADS = 8
QK_NOPE_DIM = 128
QK_ROPE_DIM = 64
QK_DIM = QK_NOPE_DIM + QK_ROPE_DIM
V_HEAD_DIM = 128
V_EXT_DIM = V_HEAD_DIM + 16
Q_LORA_RANK = 512
KV_LORA_RANK = 512
MLA_DIM = N_HEADS * V_HEAD_DIM
ROPE_THETA = 10000.0
N_CROSS_HEADS = 4
CROSS_HEAD_DIM = D_MODEL // N_CROSS_HEADS
D_FF = 5632
EPS = 1e-6
LOG2_E = 1.4426950408889634

V7X_VMEM_LIMIT_BYTES = 56 * 1024 * 1024
MXU_DIM = 256
DFT_N1 = MXU_DIM // 2
HALO = 16


def _params(*sem):
    return pltpu.CompilerParams(dimension_semantics=sem, vmem_limit_bytes=V7X_VMEM_LIMIT_BYTES)


def _const_spec(shape):
    zeros = (0,) * len(shape)
    return pl.BlockSpec(shape, lambda *_: zeros, pipeline_mode=pl.Buffered(1))


def _rms(x, g):
    return x * lax.rsqrt(jnp.mean(x * x, axis=-1, keepdims=True) + EPS) * g


def _dot(a, b):
    return jnp.dot(a, b, preferred_element_type=F32)


def _dot_nt(a, b):
    return lax.dot_general(a, b, (((1,), (1,)), ((), ())), preferred_element_type=F32)


def _in_proj_kernel(x_ref, g_ref, win_ref, qg_ref, wuqt_ref, kvg_ref, wk_ref, wvt_ref, dft_ref, cos_ref,
                    sin_ref, cost_ref, sint_ref, ab_ref, qt_ref, k_ref, vt_ref):
    h = _rms(x_ref[...], g_ref[...]).astype(BF16)
    z = _dot(h, win_ref[...])

    for g in range(N_FOURIER_GROUPS):
        u = z[:, g * FOURIER_GROUP_DIM:(g + 1) * FOURIER_GROUP_DIM].astype(BF16)
        ab = _dot(u, dft_ref[...])
        ab_ref[g, 0] = ab[:, :FOURIER_GROUP_DIM]
        ab_ref[g, 1] = ab[:, FOURIER_GROUP_DIM:]

    c0 = FOURIER_DIM
    c_q = z[:, c0:c0 + Q_LORA_RANK]
    c_kv = z[:, c0 + Q_LORA_RANK:c0 + Q_LORA_RANK + KV_LORA_RANK]
    c1 = c0 + Q_LORA_RANK + KV_LORA_RANK
    k_r = z[:, c1:c1 + QK_ROPE_DIM]
    k_r_sw = z[:, c1 + QK_ROPE_DIM:c1 + 2 * QK_ROPE_DIM]

    cos = cos_ref[...]
    sin = sin_ref[...]
    k_rope = (k_r * cos[:, :QK_ROPE_DIM] + k_r_sw * sin[:, :QK_ROPE_DIM]).astype(BF16)

    hq = _rms(c_q, qg_ref[...]).astype(BF16)
    qe_t = _dot_nt(wuqt_ref[...], hq)
    q_nope_w = N_HEADS * QK_NOPE_DIM
    cos_t = jnp.concatenate([cost_ref[...]] * N_HEADS, axis=0)
    sin_t = jnp.concatenate([sint_ref[...]] * N_HEADS, axis=0)
    q_r = qe_t[q_nope_w:]
    half = QK_ROPE_DIM // 2
    q_sw = jnp.concatenate(
        [q_r[lo:lo + half] for hd in range(N_HEADS) for lo in (hd * QK_ROPE_DIM + half, hd * QK_ROPE_DIM)], axis=0)
    q_rope_t = q_r * cos_t + q_sw * sin_t

    hkv = _rms(c_kv, kvg_ref[...]).astype(BF16)
    k_nope = _dot(hkv, wk_ref[...])
    v_t = _dot_nt(wvt_ref[...], hkv)

    scale = QK_DIM ** -0.5 * LOG2_E
    for hd in range(N_HEADS):
        qt_ref[hd, 0:QK_NOPE_DIM, :] = (qe_t[hd * QK_NOPE_DIM:(hd + 1) * QK_NOPE_DIM] * scale).astype(BF16)
        qt_ref[hd, QK_NOPE_DIM:QK_DIM, :] = (
            q_rope_t[hd * QK_ROPE_DIM:(hd + 1) * QK_ROPE_DIM] * scale).astype(BF16)
        k_ref[hd, :, 0:QK_NOPE_DIM] = k_nope[:, hd * QK_NOPE_DIM:(hd + 1) * QK_NOPE_DIM].astype(BF16)
        k_ref[hd, :, QK_NOPE_DIM:QK_DIM] = k_rope
        vt_ref[hd, 0:V_HEAD_DIM, :] = v_t[hd * V_HEAD_DIM:(hd + 1) * V_HEAD_DIM, :].astype(BF16)
        pad_row = lax.broadcasted_iota(jnp.int32, (V_EXT_DIM - V_HEAD_DIM, v_t.shape[1]), 0)
        vt_ref[hd, V_HEAD_DIM:, :] = jnp.where(pad_row == 0, 1.0, 0.0).astype(BF16)


def _in_proj(x, g_mix, w_in_ext, q_g, w_uqt_ext, kv_g, w_k, w_vt, dft_c, cos2, sin2, cos_t, sin_t, *, tm, tk):
    b, s, _ = x.shape
    grid = (b, s // tm)
    r = tk // tm
    return pl.pallas_call(
        _in_proj_kernel,
        grid=grid,
        in_specs=[
            pl.BlockSpec((None, tm, D_MODEL), lambda bi, i: (bi, i, 0)),
            _const_spec(g_mix.shape),
            _const_spec(w_in_ext.shape),
            _const_spec(q_g.shape),
            _const_spec(w_uqt_ext.shape),
            _const_spec(kv_g.shape),
            _const_spec(w_k.shape),
            _const_spec(w_vt.shape),
            _const_spec(dft_c.shape),
            pl.BlockSpec((tm, 128), lambda bi, i: (i, 0)),
            pl.BlockSpec((tm, 128), lambda bi, i: (i, 0)),
            pl.BlockSpec((QK_ROPE_DIM, tm), lambda bi, i: (0, i)),
            pl.BlockSpec((QK_ROPE_DIM, tm), lambda bi, i: (0, i)),
        ],
        out_specs=[
            pl.BlockSpec((None, N_FOURIER_GROUPS, 2, tm, FOURIER_GROUP_DIM), lambda bi, i: (bi, 0, 0, i, 0)),
            pl.BlockSpec((None, N_HEADS, QK_DIM, tm), lambda bi, i: (bi, 0, 0, i)),
            pl.BlockSpec((None, N_HEADS, tm, QK_DIM), lambda bi, i: (bi, 0, i, 0)),
            pl.BlockSpec((None, N_HEADS, None, V_EXT_DIM, tm), lambda bi, i: (bi, 0, i // r, 0, i % r)),
        ],
        out_shape=[
            jax.ShapeDtypeStruct((b, N_FOURIER_GROUPS, 2, s, FOURIER_GROUP_DIM), F32),
            jax.ShapeDtypeStruct((b, N_HEADS, QK_DIM, s), BF16),
            jax.ShapeDtypeStruct((b, N_HEADS, s, QK_DIM), BF16),
            jax.ShapeDtypeStruct((b, N_HEADS, s // tk, V_EXT_DIM, tk), BF16),
        ],
        compiler_params=_params("parallel", "parallel"),
        name="in_proj",
    )(x, g_mix, w_in_ext, q_g, w_uqt_ext, kv_g, w_k, w_vt, dft_c, cos2, sin2, cos_t, sin_t)


def _fourier_a_kernel(m_ref, x_ref, o_ref):
    two, n1, tn2, c = x_ref.shape
    for j in range(tn2):
        x = jnp.concatenate([x_ref[0, :, j, :], x_ref[1, :, j, :]], axis=0).astype(BF16)
        o_ref[:, :, j * c:(j + 1) * c] = _dot(m_ref[...], x).astype(BF16).reshape(two, n1, c)


def _fourier_a(ab, m1, *, tn2):
    b, g, two, n1, n2, c = ab.shape
    return pl.pallas_call(
        _fourier_a_kernel,
        grid=(b, g, n2 // tn2),
        in_specs=[
            _const_spec(m1.shape),
            pl.BlockSpec((None, None, two, n1, tn2, c), lambda bi, gi, ci: (bi, gi, 0, 0, ci, 0)),
        ],
        out_specs=pl.BlockSpec((None, None, two, n1, tn2 * c), lambda bi, gi, ci: (bi, gi, 0, 0, ci)),
        out_shape=jax.ShapeDtypeStruct((b, g, two, n1, n2 * c), BF16),
        compiler_params=_params("parallel", "parallel", "parallel"),
        name="fourier_a",
    )(m1, ab)


def _fourier_b_kernel(e_ref, g_ref, o_ref, *, scale):
    kb = e_ref.shape[0]
    for j in range(kb):
        x = jnp.concatenate([g_ref[0, j], g_ref[1, j]], axis=0)
        o_ref[:, j, :] = _dot(e_ref[j], x) * scale


def _fourier_b(gc, e, *, kb, scale):
    b, g, two, n1, n2, c = gc.shape
    return pl.pallas_call(
        functools.partial(_fourier_b_kernel, scale=scale),
        grid=(b, g, n1 // kb),
        in_specs=[
            pl.BlockSpec((kb, n2, 2 * n2), lambda bi, gi, ki: (ki, 0, 0)),
            pl.BlockSpec((None, None, two, kb, n2, c), lambda bi, gi, ki: (bi, gi, 0, ki, 0, 0)),
        ],
        out_specs=pl.BlockSpec((None, n2, kb, c), lambda bi, gi, ki: (bi, 0, ki, gi)),
        out_shape=jax.ShapeDtypeStruct((b, n2, n1, g * c), F32),
        compiler_params=_params("parallel", "parallel", "parallel"),
        name="fourier_b",
    )(e, gc)


def _attn_kernel(qt_ref, k_ref, vt_ref, o_ref, sa_ref, sb_ref, acc_ref, *, group_size):
    tk, tq = sa_ref.shape
    nk = vt_ref.shape[0]
    n_tiles = qt_ref.shape[1] // tq
    bufs = (sa_ref, sb_ref)

    def scores(tile, j, s_ref):
        start = pl.multiple_of(j * tk, tk)
        qt = qt_ref[:, tile * tq:(tile + 1) * tq]
        s = _dot(k_ref[pl.ds(start, tk), :], qt)
        s_ref[...] = s
        return jnp.max(s, axis=0, keepdims=True)

    def update(j, s_ref, cmax, m):
        m_new = jnp.maximum(m, cmax)
        p = jnp.exp2(s_ref[...] - m_new)
        acc_ref[...] = jnp.exp2(m - m_new) * acc_ref[...] + _dot(vt_ref[j], p.astype(BF16))
        return m_new

    def group(tile, j, cmax, m, last):
        for t in range(group_size):
            if not last or t + 1 < group_size:
                nxt = scores(tile, j + t + 1, bufs[(t + 1) % 2])
            elif tile + 1 < n_tiles:
                nxt = scores(tile + 1, 0, bufs[0])
            else:
                nxt = None
            m = update(j + t, bufs[t % 2], cmax, m)
            cmax = nxt
        return cmax, m

    acc_ref[...] = jnp.zeros_like(acc_ref)
    cmax = scores(0, 0, sa_ref)
    for tile in range(n_tiles):
        carry = (cmax, jnp.full((1, tq), -jnp.inf, F32))
        carry = lax.fori_loop(0, nk // group_size - 1,
                              lambda i, c, tile=tile: group(tile, group_size * i, *c, False), carry)
        cmax, _ = group(tile, nk - group_size, *carry, True)
        acc = acc_ref[...]
        o_ref[tile * tq:(tile + 1) * tq, :] = (acc[0:V_HEAD_DIM] / acc[V_HEAD_DIM:V_HEAD_DIM + 1]).T


def _attention(qt, k, vt, *, tq, tiles_per_step, group_size):
    b, h, s, _ = k.shape
    nk, _, tk = vt.shape[2:]
    assert group_size % 2 == 0 and nk % group_size == 0
    rows = tq * tiles_per_step
    return pl.pallas_call(
        functools.partial(_attn_kernel, group_size=group_size),
        grid=(b, h, s // rows),
        in_specs=[
            pl.BlockSpec((None, None, QK_DIM, rows), lambda bi, hi, i: (bi, hi, 0, i)),
            pl.BlockSpec((None, None, s, QK_DIM), lambda bi, hi, i: (bi, hi, 0, 0)),
            pl.BlockSpec((None, None, nk, V_EXT_DIM, tk), lambda bi, hi, i: (bi, hi, 0, 0, 0)),
        ],
        out_specs=pl.BlockSpec((None, rows, V_HEAD_DIM), lambda bi, hi, i: (bi, i, hi)),
        out_shape=jax.ShapeDtypeStruct((b, s, h * V_HEAD_DIM), F32),
        scratch_shapes=[pltpu.VMEM((tk, tq), F32), pltpu.VMEM((tk, tq), F32), pltpu.VMEM((V_EXT_DIM, tq), F32)],
        compiler_params=_params("parallel", "parallel", "parallel"),
        name="attention",
    )(qt, k, vt)


def _out_proj_kernel(x_ref, f_ref, a_ref, gf_ref, ga_ref, w_ref, o_ref):
    hf = _rms(f_ref[...], gf_ref[...]).astype(BF16)
    ha = _rms(a_ref[...], ga_ref[...]).astype(BF16)
    o_ref[...] = x_ref[...] + _dot(hf, w_ref[0:FOURIER_DIM, :]) + _dot(ha, w_ref[FOURIER_DIM:, :])


def _out_proj(x, f, a, g_f, g_a, w_out, *, tm):
    t = x.shape[0]
    return pl.pallas_call(
        _out_proj_kernel,
        grid=(t // tm,),
        in_specs=[
            pl.BlockSpec((tm, D_MODEL), lambda i: (i, 0)),
            pl.BlockSpec((tm, FOURIER_DIM), lambda i: (i, 0)),
            pl.BlockSpec((tm, MLA_DIM), lambda i: (i, 0)),
            _const_spec(g_f.shape),
            _const_spec(g_a.shape),
            _const_spec(w_out.shape),
        ],
        out_specs=pl.BlockSpec((tm, D_MODEL), lambda i: (i, 0)),
        out_shape=jax.ShapeDtypeStruct((t, D_MODEL), F32),
        compiler_params=_params("parallel"),
        name="out_proj",
    )(x, f, a, g_f, g_a, w_out)


def _mem_kv_kernel(m_ref, g_ref, w_ref, o_ref, mn_ref):
    @pl.when(pl.program_id(0) == 0)
    def _():
        mn_ref[...] = _rms(m_ref[...], g_ref[...]).astype(BF16)

    o_ref[...] = _dot(mn_ref[...], w_ref[...].astype(BF16)).astype(BF16)


def _mem_kv(mem, g_mem, w_ckv, *, tn):
    b, m, _ = mem.shape
    n = w_ckv.shape[1]
    rows = b * m
    kv = pl.pallas_call(
        _mem_kv_kernel,
        grid=(n // tn,),
        in_specs=[
            _const_spec((rows, D_MODEL)),
            _const_spec(g_mem.shape),
            pl.BlockSpec((D_MODEL, tn), lambda j: (0, j)),
        ],
        out_specs=pl.BlockSpec((rows, tn), lambda j: (0, j)),
        out_shape=jax.ShapeDtypeStruct((rows, n), BF16),
        scratch_shapes=[pltpu.VMEM((rows, D_MODEL), BF16)],
        compiler_params=_params("arbitrary"),
        name="mem_kv",
    )(mem.reshape(rows, D_MODEL), g_mem, w_ckv)
    return kv.reshape(b, m, n)


def _cross_fold_kernel(k_ref, v_ref, wq_ref, wo_ref, m_ref, n_ref):
    m_ref[...] = (_dot_nt(wq_ref[...], k_ref[...]) * (CROSS_HEAD_DIM ** -0.5)).astype(BF16)
    n_ref[...] = _dot(v_ref[...], wo_ref[...]).astype(BF16)


def _cross_fold(kvm, w_cq, w_co):
    b, m, _ = kvm.shape
    return pl.pallas_call(
        _cross_fold_kernel,
        grid=(N_CROSS_HEADS, b),
        in_specs=[
            pl.BlockSpec((None, m, CROSS_HEAD_DIM), lambda hd, bi: (bi, 0, hd)),
            pl.BlockSpec((None, m, CROSS_HEAD_DIM), lambda hd, bi: (bi, 0, N_CROSS_HEADS + hd)),
            pl.BlockSpec((D_MODEL, CROSS_HEAD_DIM), lambda hd, bi: (0, hd)),
            pl.BlockSpec((CROSS_HEAD_DIM, D_MODEL), lambda hd, bi: (hd, 0)),
        ],
        out_specs=[
            pl.BlockSpec((None, D_MODEL, m), lambda hd, bi: (bi, 0, hd)),
            pl.BlockSpec((None, m, D_MODEL), lambda hd, bi: (bi, hd, 0)),
        ],
        out_shape=[jax.ShapeDtypeStruct((b, D_MODEL, N_CROSS_HEADS * m), BF16),
                   jax.ShapeDtypeStruct((b, N_CROSS_HEADS * m, D_MODEL), BF16)],
        compiler_params=_params("arbitrary", "arbitrary"),
        name="cross_fold",
    )(kvm, kvm, w_cq, w_co)


def _cross_kernel(x_ref, m_ref, n_ref, gc_ref, gn_ref, o_ref, h_ref):
    x = x_ref[...]
    h = _rms(x, gc_ref[...]).astype(BF16)
    s = _dot(h, m_ref[...])
    n_mem = m_ref.shape[1] // N_CROSS_HEADS
    probs = []
    for hd in range(N_CROSS_HEADS):
        sh = s[:, hd * n_mem:(hd + 1) * n_mem]
        p = jnp.exp(sh - jnp.max(sh, axis=-1, keepdims=True))
        probs.append((p / jnp.sum(p, axis=-1, keepdims=True)).astype(BF16))
    y = x + _dot(jnp.concatenate(probs, axis=1), n_ref[...])
    o_ref[...] = y
    h_ref[...] = _rms(y, gn_ref[...]).astype(BF16)


def _cross(x, m_all, n_all, g_cross, g_ffn, *, tm):
    b, s, _ = x.shape
    spec = pl.BlockSpec((None, tm, D_MODEL), lambda bi, i: (bi, i, 0))
    return pl.pallas_call(
        _cross_kernel,
        grid=(b, s // tm),
        in_specs=[
            spec,
            pl.BlockSpec((None,) + m_all.shape[1:], lambda bi, i: (bi, 0, 0)),
            pl.BlockSpec((None,) + n_all.shape[1:], lambda bi, i: (bi, 0, 0)),
            _const_spec(g_cross.shape),
            _const_spec(g_ffn.shape),
        ],
        out_specs=[spec, spec],
        out_shape=[jax.ShapeDtypeStruct((b, s, D_MODEL), F32), jax.ShapeDtypeStruct((b, s, D_MODEL), BF16)],
        compiler_params=_params("parallel", "parallel"),
        name="cross",
    )(x, m_all, n_all, g_cross, g_ffn)


def _ffn_up_kernel(h_ref, prev_ref, next_ref, wg_ref, wu_ref, cwb_ref, o_ref, hext_ref, *, seq):
    i = pl.program_id(0)
    j = pl.program_id(1)
    tm = h_ref.shape[0]

    @pl.when(j == 0)
    def _():
        first = (i * tm) % seq == 0
        last = ((i + 1) * tm) % seq == 0
        prev = prev_ref[...]
        nxt = next_ref[...]
        hext_ref[0:HALO, :] = jnp.where(first, jnp.zeros_like(prev), prev)
        hext_ref[HALO:HALO + tm, :] = h_ref[...]
        hext_ref[HALO + tm:, :] = jnp.where(last, jnp.zeros_like(nxt), nxt)

    g = _dot(hext_ref[...], wg_ref[...].astype(BF16))
    u = _dot(h_ref[...], wu_ref[...].astype(BF16))
    rows = g.shape[0]
    g_prev = pltpu.roll(g, 1, axis=0)[HALO:HALO + tm]
    g_next = pltpu.roll(g, rows - 1, axis=0)[HALO:HALO + tm]
    cwb = cwb_ref[j]
    c = g_prev * cwb[0:1] + g[HALO:HALO + tm] * cwb[1:2] + g_next * cwb[2:3] + cwb[3:4]
    o_ref[...] = (c / (1.0 + jnp.exp(-c)) * u).astype(BF16)


def _ffn_up(h, w_gate, w_up, conv_w, conv_b, *, seq, tm, tf):
    t = h.shape[0]
    hb = tm // HALO
    n_halo = t // HALO
    cwb = jnp.concatenate([conv_w, conv_b], axis=0).reshape(4, D_FF // tf, tf).transpose(1, 0, 2)
    return pl.pallas_call(
        functools.partial(_ffn_up_kernel, seq=seq),
        grid=(t // tm, D_FF // tf),
        in_specs=[
            pl.BlockSpec((tm, D_MODEL), lambda i, j: (i, 0)),
            pl.BlockSpec((HALO, D_MODEL), lambda i, j: (jnp.maximum(i * hb - 1, 0), 0)),
            pl.BlockSpec((HALO, D_MODEL), lambda i, j: (jnp.minimum((i + 1) * hb, n_halo - 1), 0)),
            pl.BlockSpec((D_MODEL, tf), lambda i, j: (0, j)),
            pl.BlockSpec((D_MODEL, tf), lambda i, j: (0, j)),
            _const_spec(cwb.shape),
        ],
        out_specs=pl.BlockSpec((tm, tf), lambda i, j: (i, j)),
        out_shape=jax.ShapeDtypeStruct((t, D_FF), BF16),
        scratch_shapes=[pltpu.VMEM((tm + 2 * HALO, D_MODEL), BF16)],
        compiler_params=_params("parallel", "arbitrary"),
        name="ffn_up",
    )(h, h, h, w_gate, w_up, cwb)


def _ffn_down_kernel(a_ref, w_ref, x_ref, g_ref, o_ref):
    o_ref[...] = _rms(x_ref[...] + _dot(a_ref[...], w_ref[...]), g_ref[...])


def _ffn_down(act, w_down, x, g_final, *, tm):
    t = x.shape[0]
    return pl.pallas_call(
        _ffn_down_kernel,
        grid=(t // tm,),
        in_specs=[
            pl.BlockSpec((tm, D_FF), lambda i: (i, 0)),
            _const_spec(w_down.shape),
            pl.BlockSpec((tm, D_MODEL), lambda i: (i, 0)),
            _const_spec(g_final.shape),
        ],
        out_specs=pl.BlockSpec((tm, D_MODEL), lambda i: (i, 0)),
        out_shape=jax.ShapeDtypeStruct((t, D_MODEL), F32),
        compiler_params=_params("parallel"),
        name="ffn_down",
    )(act, w_down, x, g_final)


def _dft_tables(seq):
    n1 = DFT_N1
    n2 = seq // n1
    c = np.arange(FOURIER_GROUP_DIM)
    ang = 2.0 * np.pi * ((c[:, None] * c[None, :]) % FOURIER_GROUP_DIM) / FOURIER_GROUP_DIM
    dft_c = np.concatenate([np.cos(ang), np.sin(ang)], axis=1)

    j = np.arange(n1)
    a1 = 2.0 * np.pi * ((j[:, None] * j[None, :]) % n1) / n1
    c1, s1 = np.cos(a1), np.sin(a1)
    m1 = np.block([[c1, -s1], [-s1, -c1]])

    k1 = np.arange(n1)[:, None, None]
    k2 = np.arange(n2)[None, :, None]
    m2 = np.arange(n2)[None, None, :]
    a2 = 2.0 * np.pi * ((m2 * (k1 + n1 * k2)) % seq) / seq
    e = np.concatenate([np.cos(a2), np.sin(a2)], axis=2)
    return (jnp.asarray(dft_c, BF16), jnp.asarray(m1, BF16), jnp.asarray(e, BF16))


def _rope_tables(seq):
    inv = ROPE_THETA ** (-jnp.arange(0, QK_ROPE_DIM, 2, dtype=F32) / QK_ROPE_DIM)
    ang = jnp.arange(seq, dtype=F32)[:, None] * inv[None, :]
    cos, sin = jnp.cos(ang), jnp.sin(ang)
    return (jnp.concatenate([cos, cos, cos, cos], axis=1), jnp.concatenate([-sin, sin, -sin, sin], axis=1),
            jnp.concatenate([cos.T, cos.T], axis=0), jnp.concatenate([-sin.T, sin.T], axis=0))


def _prep_weights(w_in, w_uq, w_ukv):
    half = QK_ROPE_DIM // 2
    c1 = FOURIER_DIM + Q_LORA_RANK + KV_LORA_RANK
    w_in_ext = jnp.concatenate(
        [w_in, w_in[:, c1 + half:c1 + QK_ROPE_DIM], w_in[:, c1:c1 + half]], axis=1).astype(BF16)
    q3 = w_uq.reshape(Q_LORA_RANK, N_HEADS, QK_DIM)
    w_uqt_ext = jnp.concatenate(
        [q3[:, :, :QK_NOPE_DIM].reshape(Q_LORA_RANK, -1), q3[:, :, QK_NOPE_DIM:].reshape(Q_LORA_RANK, -1)],
        axis=1).T.astype(BF16)
    kv3 = w_ukv.reshape(KV_LORA_RANK, N_HEADS, QK_NOPE_DIM + V_HEAD_DIM)
    w_k = kv3[:, :, :QK_NOPE_DIM].reshape(KV_LORA_RANK, -1).astype(BF16)
    w_vt = kv3[:, :, QK_NOPE_DIM:].reshape(KV_LORA_RANK, -1).T.astype(BF16)
    return w_in_ext, w_uqt_ext, w_k, w_vt


def _tile(n, pref):
    return pref if n % pref == 0 else n


def _tiles(s):
    attn_k = _tile(s, 512)
    return dict(
        in_proj=_tile(s, 512),
        attn_q=_tile(s, 1024), attn_tiles=4 if s % 4096 == 0 else 1,
        attn_k=attn_k, attn_group=4 if (s // attn_k) % 4 == 0 else 2,
        fourier_tn2=min(16, s // DFT_N1), fourier_kb=32,
        out_proj=_tile(s, 512), mem_kv_cols=1024, cross=_tile(s, 512),
        ffn_up=_tile(s, 1024), ffn_cols=512, ffn_down=_tile(s, 256),
    )


def _trunk(x, mem, p, tables):
    b, s, _ = x.shape
    t = b * s
    n1 = DFT_N1
    n2 = s // n1
    ts = _tiles(s)
    dft_c, m1, e, cos2, sin2, cos_t, sin_t = tables

    ab, qt, k, vt = _in_proj(x, p["g_mix"], p["w_in_ext"], p["g_q"], p["w_uqt_ext"], p["g_kv"], p["w_k"],
                             p["w_vt"], dft_c, cos2, sin2, cos_t, sin_t, tm=ts["in_proj"], tk=ts["attn_k"])

    gc = _fourier_a(ab.reshape(b, N_FOURIER_GROUPS, 2, n1, n2, FOURIER_GROUP_DIM), m1, tn2=ts["fourier_tn2"])
    f = _fourier_b(gc.reshape(b, N_FOURIER_GROUPS, 2, n1, n2, FOURIER_GROUP_DIM), e, kb=ts["fourier_kb"],
                   scale=float((s * FOURIER_GROUP_DIM) ** -0.5))

    a = _attention(qt, k, vt, tq=ts["attn_q"], tiles_per_step=ts["attn_tiles"], group_size=ts["attn_group"])

    x1 = _out_proj(x.reshape(t, D_MODEL), f.reshape(t, FOURIER_DIM), a.reshape(t, MLA_DIM),
                   p["g_f"], p["g_a"], p["w_out"], tm=ts["out_proj"])

    kvm = _mem_kv(mem, p["g_mem"], p["w_ckv"], tn=ts["mem_kv_cols"])
    m_all, n_all = _cross_fold(kvm, p["w_cq"], p["w_co"])
    x2, hf = _cross(x1.reshape(b, s, D_MODEL), m_all, n_all, p["g_cross"], p["g_ffn"], tm=ts["cross"])

    act = _ffn_up(hf.reshape(t, D_MODEL), p["w_gate"], p["w_up"], p["conv_w"], p["conv_b"], seq=s,
                  tm=ts["ffn_up"], tf=ts["ffn_cols"])
    y = _ffn_down(act, p["w_down"], x2.reshape(t, D_MODEL), p["g_final"], tm=ts["ffn_down"])
    return y.reshape(b, s, D_MODEL)


@jax.jit
def kernel(x_prompt, x_sample, mem_prompt, mem_sample, norm_mix_g, w_in, q_norm_g, w_uq, kv_norm_g, w_ukv,
           fourier_out_g, mla_out_g, w_out, norm_cross_g, norm_mem_g, w_cq, w_ckv, w_co, norm_ffn_g, w_gate,
           w_up, conv_w, conv_b, w_down, final_norm_g):
    assert norm_mix_g.shape[0] == 1, "single-layer trunk"
    assert x_prompt.shape[1] % DFT_N1 == 0 and x_sample.shape[1] % DFT_N1 == 0
    w_in_ext, w_uqt_ext, w_k, w_vt = _prep_weights(w_in[0], w_uq[0], w_ukv[0])
    row = lambda g: g.reshape(1, -1)
    p = dict(
        g_mix=row(norm_mix_g[0]), w_in_ext=w_in_ext, g_q=row(q_norm_g[0]), w_uqt_ext=w_uqt_ext,
        g_kv=row(kv_norm_g[0]), w_k=w_k, w_vt=w_vt, g_f=row(fourier_out_g[0]), g_a=row(mla_out_g[0]),
        w_out=w_out[0].astype(BF16), g_cross=row(norm_cross_g[0]), g_mem=row(norm_mem_g[0]),
        w_cq=w_cq[0].astype(BF16), w_ckv=w_ckv.reshape(w_ckv.shape[1:]), w_co=w_co[0].astype(BF16),
        g_ffn=row(norm_ffn_g[0]), w_gate=w_gate.reshape(w_gate.shape[1:]), w_up=w_up.reshape(w_up.shape[1:]),
        conv_w=conv_w[0], conv_b=row(conv_b[0]), w_down=w_down[0].astype(BF16), g_final=row(final_norm_g),
    )
    outs = []
    for x, mem in ((x_prompt, mem_prompt), (x_sample, mem_sample)):
        s = x.shape[1]
        tables = _dft_tables(s) + _rope_tables(s)
        outs.append(_trunk(x, mem, p, tables))
    return tuple(outs)
```

```python
import functools

import numpy as np
import jax
import jax.numpy as jnp
from jax import lax
from jax.experimental import pallas as pl
from jax.experimental.pallas import tpu as pltpu

F32 = jnp.float32
BF16 = jnp.bfloat16

D_MODEL = 2048
N_FOURIER_GROUPS = 4
FOURIER_GROUP_DIM = 256
FOURIER_DIM = N_FOURIER_GROUPS * FOURIER_GROUP_DIM
N_HEADS = 8
QK_NOPE_DIM = 128
QK_ROPE_DIM = 64
QK_DIM = QK_NOPE_DIM + QK_ROPE_DIM
V_HEAD_DIM = 128
V_EXT_DIM = V_HEAD_DIM + 16
Q_LORA_RANK = 512
KV_LORA_RANK = 512
MLA_DIM = N_HEADS * V_HEAD_DIM
ROPE_THETA = 10000.0
N_CROSS_HEADS = 4
CROSS_HEAD_DIM = D_MODEL // N_CROSS_HEADS
D_FF = 5632
EPS = 1e-6
LOG2_E = 1.4426950408889634

V7X_VMEM_LIMIT_BYTES = 56 * 1024 * 1024
MXU_DIM = 256
DFT_N1 = MXU_DIM // 2
HALO = 16


def _params(*sem):
    return pltpu.CompilerParams(dimension_semantics=sem, vmem_limit_bytes=V7X_VMEM_LIMIT_BYTES)


def _const_spec(shape):
    zeros = (0,) * len(shape)
    return pl.BlockSpec(shape, lambda *_: zeros, pipeline_mode=pl.Buffered(1))


def _rms(x, g):
    return x * lax.rsqrt(jnp.mean(x * x, axis=-1, keepdims=True) + EPS) * g


def _dot(a, b):
    return jnp.dot(a, b, preferred_element_type=F32)


def _dot_nt(a, b):
    return lax.dot_general(a, b, (((1,), (1,)), ((), ())), preferred_element_type=F32)


def _in_proj_kernel(x_ref, g_ref, win_ref, qg_ref, wuqt_ref, kvg_ref, wk_ref, wvt_ref, dft_ref, cos_ref,
                    sin_ref, cost_ref, sint_ref, ab_ref, qt_ref, k_ref, vt_ref):
    h = _rms(x_ref[...], g_ref[...]).astype(BF16)
    z = _dot(h, win_ref[...])

    for g in range(N_FOURIER_GROUPS):
        u = z[:, g * FOURIER_GROUP_DIM:(g + 1) * FOURIER_GROUP_DIM].astype(BF16)
        ab = _dot(u, dft_ref[...])
        ab_ref[g, 0] = ab[:, :FOURIER_GROUP_DIM]
        ab_ref[g, 1] = ab[:, FOURIER_GROUP_DIM:]

    c0 = FOURIER_DIM
    c_q = z[:, c0:c0 + Q_LORA_RANK]
    c_kv = z[:, c0 + Q_LORA_RANK:c0 + Q_LORA_RANK + KV_LORA_RANK]
    c1 = c0 + Q_LORA_RANK + KV_LORA_RANK
    k_r = z[:, c1:c1 + QK_ROPE_DIM]
    k_r_sw = z[:, c1 + QK_ROPE_DIM:c1 + 2 * QK_ROPE_DIM]

    cos = cos_ref[...]
    sin = sin_ref[...]
    k_rope = (k_r * cos[:, :QK_ROPE_DIM] + k_r_sw * sin[:, :QK_ROPE_DIM]).astype(BF16)

    hq = _rms(c_q, qg_ref[...]).astype(BF16)
    qe_t = _dot_nt(wuqt_ref[...], hq)
    q_nope_w = N_HEADS * QK_NOPE_DIM
    cos_t = jnp.concatenate([cost_ref[...]] * N_HEADS, axis=0)
    sin_t = jnp.concatenate([sint_ref[...]] * N_HEADS, axis=0)
    q_r = qe_t[q_nope_w:]
    half = QK_ROPE_DIM // 2
    q_sw = jnp.concatenate(
        [q_r[lo:lo + half] for hd in range(N_HEADS) for lo in (hd * QK_ROPE_DIM + half, hd * QK_ROPE_DIM)], axis=0)
    q_rope_t = q_r * cos_t + q_sw * sin_t

    hkv = _rms(c_kv, kvg_ref[...]).astype(BF16)
    k_nope = _dot(hkv, wk_ref[...])
    v_t = _dot_nt(wvt_ref[...], hkv)

    scale = QK_DIM ** -0.5 * LOG2_E
    for hd in range(N_HEADS):
        qt_ref[hd, 0:QK_NOPE_DIM, :] = (qe_t[hd * QK_NOPE_DIM:(hd + 1) * QK_NOPE_DIM] * scale).astype(BF16)
        qt_ref[hd, QK_NOPE_DIM:QK_DIM, :] = (
            q_rope_t[hd * QK_ROPE_DIM:(hd + 1) * QK_ROPE_DIM] * scale).astype(BF16)
        k_ref[hd, :, 0:QK_NOPE_DIM] = k_nope[:, hd * QK_NOPE_DIM:(hd + 1) * QK_NOPE_DIM].astype(BF16)
        k_ref[hd, :, QK_NOPE_DIM:QK_DIM] = k_rope
        vt_ref[hd, 0:V_HEAD_DIM, :] = v_t[hd * V_HEAD_DIM:(hd + 1) * V_HEAD_DIM, :].astype(BF16)
        pad_row = lax.broadcasted_iota(jnp.int32, (V_EXT_DIM - V_HEAD_DIM, v_t.shape[1]), 0)
        vt_ref[hd, V_HEAD_DIM:, :] = jnp.where(pad_row == 0, 1.0, 0.0).astype(BF16)


def _in_proj(x, g_mix, w_in_ext, q_g, w_uqt_ext, kv_g, w_k, w_vt, dft_c, cos2, sin2, cos_t, sin_t, *, tm, tk):
    b, s, _ = x.shape
    grid = (b, s // tm)
    r = tk // tm
    return pl.pallas_call(
        _in_proj_kernel,
        grid=grid,
        in_specs=[
            pl.BlockSpec((None, tm, D_MODEL), lambda bi, i: (bi, i, 0)),
            _const_spec(g_mix.shape),
            _const_spec(w_in_ext.shape),
            _const_spec(q_g.shape),
            _const_spec(w_uqt_ext.shape),
            _const_spec(kv_g.shape),
            _const_spec(w_k.shape),
            _const_spec(w_vt.shape),
            _const_spec(dft_c.shape),
            pl.BlockSpec((tm, 128), lambda bi, i: (i, 0)),
            pl.BlockSpec((tm, 128), lambda bi, i: (i, 0)),
            pl.BlockSpec((QK_ROPE_DIM, tm), lambda bi, i: (0, i)),
            pl.BlockSpec((QK_ROPE_DIM, tm), lambda bi, i: (0, i)),
        ],
        out_specs=[
            pl.BlockSpec((None, N_FOURIER_GROUPS, 2, tm, FOURIER_GROUP_DIM), lambda bi, i: (bi, 0, 0, i, 0)),
            pl.BlockSpec((None, N_HEADS, QK_DIM, tm), lambda bi, i: (bi, 0, 0, i)),
            pl.BlockSpec((None, N_HEADS, tm, QK_DIM), lambda bi, i: (bi, 0, i, 0)),
            pl.BlockSpec((None, N_HEADS, None, V_EXT_DIM, tm), lambda bi, i: (bi, 0, i // r, 0, i % r)),
        ],
        out_shape=[
            jax.ShapeDtypeStruct((b, N_FOURIER_GROUPS, 2, s, FOURIER_GROUP_DIM), F32),
            jax.ShapeDtypeStruct((b, N_HEADS, QK_DIM, s), BF16),
            jax.ShapeDtypeStruct((b, N_HEADS, s, QK_DIM), BF16),
            jax.ShapeDtypeStruct((b, N_HEADS, s // tk, V_EXT_DIM, tk), BF16),
        ],
        compiler_params=_params("parallel", "parallel"),
        name="in_proj",
    )(x, g_mix, w_in_ext, q_g, w_uqt_ext, kv_g, w_k, w_vt, dft_c, cos2, sin2, cos_t, sin_t)


def _fourier_a_kernel(m_ref, x_ref, o_ref):
    two, n1, tn2, c = x_ref.shape
    for j in range(tn2):
        x = jnp.concatenate([x_ref[0, :, j, :], x_ref[1, :, j, :]], axis=0).astype(BF16)
        o_ref[:, :, j * c:(j + 1) * c] = _dot(m_ref[...], x).astype(BF16).reshape(two, n1, c)


def _fourier_a(ab, m1, *, tn2):
    b, g, two, n1, n2, c = ab.shape
    return pl.pallas_call(
        _fourier_a_kernel,
        grid=(b, g, n2 // tn2),
        in_specs=[
            _const_spec(m1.shape),
            pl.BlockSpec((None, None, two, n1, tn2, c), lambda bi, gi, ci: (bi, gi, 0, 0, ci, 0)),
        ],
        out_specs=pl.BlockSpec((None, None, two, n1, tn2 * c), lambda bi, gi, ci: (bi, gi, 0, 0, ci)),
        out_shape=jax.ShapeDtypeStruct((b, g, two, n1, n2 * c), BF16),
        compiler_params=_params("parallel", "parallel", "parallel"),
        name="fourier_a",
    )(m1, ab)


def _fourier_b_kernel(e_ref, g_ref, o_ref, *, scale):
    kb = e_ref.shape[0]
    for j in range(kb):
        x = jnp.concatenate([g_ref[0, j], g_ref[1, j]], axis=0)
        o_ref[:, j, :] = _dot(e_ref[j], x) * scale


def _fourier_b(gc, e, *, kb, scale):
    b, g, two, n1, n2, c = gc.shape
    return pl.pallas_call(
        functools.partial(_fourier_b_kernel, scale=scale),
        grid=(b, g, n1 // kb),
        in_specs=[
            pl.BlockSpec((kb, n2, 2 * n2), lambda bi, gi, ki: (ki, 0, 0)),
            pl.BlockSpec((None, None, two, kb, n2, c), lambda bi, gi, ki: (bi, gi, 0, ki, 0, 0)),
        ],
        out_specs=pl.BlockSpec((None, n2, kb, c), lambda bi, gi, ki: (bi, 0, ki, gi)),
        out_shape=jax.ShapeDtypeStruct((b, n2, n1, g * c), F32),
        compiler_params=_params("parallel", "parallel", "parallel"),
        name="fourier_b",
    )(e, gc)


def _attn_kernel(qt_ref, k_ref, vt_ref, o_ref, *scratch, group_size, n_streams):
    s_bufs, accs = scratch[:2 * n_streams], scratch[2 * n_streams:]
    tk, tq = s_bufs[0].shape
    nk = vt_ref.shape[0]
    n_tiles = qt_ref.shape[1] // tq

    def scores(tile, j, s_ref):
        start = pl.multiple_of(j * tk, tk)
        qt = qt_ref[:, tile * tq:(tile + 1) * tq]
        s = _dot(k_ref[pl.ds(start, tk), :], qt)
        s_ref[...] = s
        return jnp.max(s, axis=0, keepdims=True)

    def update(acc_ref, j, s_ref, cmax, m):
        m_new = jnp.maximum(m, cmax)
        p = jnp.exp2(s_ref[...] - m_new)
        acc_ref[...] = jnp.exp2(m - m_new) * acc_ref[...] + _dot(vt_ref[j], p.astype(BF16))
        return m_new

    def group(first_tile, j, carry, last):
        carry = list(carry)
        for t in range(group_size):
            for st in range(n_streams):
                tile = first_tile + st
                bufs = s_bufs[2 * st:2 * st + 2]
                cmax, m = carry[st]
                if not last or t + 1 < group_size:
                    nxt = scores(tile, j + t + 1, bufs[(t + 1) % 2])
                elif tile + n_streams < n_tiles:
                    nxt = scores(tile + n_streams, 0, bufs[0])
                else:
                    nxt = cmax
                carry[st] = (nxt, update(accs[st], j + t, bufs[t % 2], cmax, m))
        return tuple(carry)

    for acc_ref in accs:
        acc_ref[...] = jnp.zeros_like(acc_ref)
    cmaxes = [scores(st, 0, s_bufs[2 * st]) for st in range(n_streams)]
    for first_tile in range(0, n_tiles, n_streams):
        carry = tuple((c, jnp.full((1, tq), -jnp.inf, F32)) for c in cmaxes)
        carry = lax.fori_loop(0, nk // group_size - 1,
                              lambda i, c, ft=first_tile: group(ft, group_size * i, c, False), carry)
        carry = group(first_tile, nk - group_size, carry, True)
        cmaxes = [c for c, _ in carry]
        for st in range(n_streams):
            acc = accs[st][...]
            lo = (first_tile + st) * tq
            o_ref[lo:lo + tq, :] = (acc[0:V_HEAD_DIM] / acc[V_HEAD_DIM:V_HEAD_DIM + 1]).T


def _attention(qt, k, vt, *, tq, tiles_per_step, n_streams, group_size):
    b, h, s, _ = k.shape
    nk, _, tk = vt.shape[2:]
    assert group_size % 2 == 0 and nk % group_size == 0 and tiles_per_step % n_streams == 0
    rows = tq * tiles_per_step
    return pl.pallas_call(
        functools.partial(_attn_kernel, group_size=group_size, n_streams=n_streams),
        grid=(b, h, s // rows),
        in_specs=[
            pl.BlockSpec((None, None, QK_DIM, rows), lambda bi, hi, i: (bi, hi, 0, i)),
            pl.BlockSpec((None, None, s, QK_DIM), lambda bi, hi, i: (bi, hi, 0, 0)),
            pl.BlockSpec((None, None, nk, V_EXT_DIM, tk), lambda bi, hi, i: (bi, hi, 0, 0, 0)),
        ],
        out_specs=pl.BlockSpec((None, rows, V_HEAD_DIM), lambda bi, hi, i: (bi, i, hi)),
        out_shape=jax.ShapeDtypeStruct((b, s, h * V_HEAD_DIM), F32),
        scratch_shapes=([pltpu.VMEM((tk, tq), F32)] * (2 * n_streams)
                        + [pltpu.VMEM((V_EXT_DIM, tq), F32)] * n_streams),
        compiler_params=_params("parallel", "parallel", "parallel"),
        name="attention",
    )(qt, k, vt)


def _out_proj_kernel(x_ref, f_ref, a_ref, gf_ref, ga_ref, w_ref, o_ref):
    hf = _rms(f_ref[...], gf_ref[...]).astype(BF16)
    ha = _rms(a_ref[...], ga_ref[...]).astype(BF16)
    o_ref[...] = x_ref[...] + _dot(hf, w_ref[0:FOURIER_DIM, :]) + _dot(ha, w_ref[FOURIER_DIM:, :])


def _out_proj(x, f, a, g_f, g_a, w_out, *, tm):
    t = x.shape[0]
    return pl.pallas_call(
        _out_proj_kernel,
        grid=(t // tm,),
        in_specs=[
            pl.BlockSpec((tm, D_MODEL), lambda i: (i, 0)),
            pl.BlockSpec((tm, FOURIER_DIM), lambda i: (i, 0)),
            pl.BlockSpec((tm, MLA_DIM), lambda i: (i, 0)),
            _const_spec(g_f.shape),
            _const_spec(g_a.shape),
            _const_spec(w_out.shape),
        ],
        out_specs=pl.BlockSpec((tm, D_MODEL), lambda i: (i, 0)),
        out_shape=jax.ShapeDtypeStruct((t, D_MODEL), F32),
        compiler_params=_params("parallel"),
        name="out_proj",
    )(x, f, a, g_f, g_a, w_out)


def _mem_kv_kernel(m_ref, g_ref, w_ref, o_ref, mn_ref):
    @pl.when(pl.program_id(0) == 0)
    def _():
        mn_ref[...] = _rms(m_ref[...], g_ref[...]).astype(BF16)

    o_ref[...] = _dot(mn_ref[...], w_ref[...].astype(BF16)).astype(BF16)


def _mem_kv(mem, g_mem, w_ckv, *, tn):
    b, m, _ = mem.shape
    n = w_ckv.shape[1]
    rows = b * m
    kv = pl.pallas_call(
        _mem_kv_kernel,
        grid=(n // tn,),
        in_specs=[
            _const_spec((rows, D_MODEL)),
            _const_spec(g_mem.shape),
            pl.BlockSpec((D_MODEL, tn), lambda j: (0, j)),
        ],
        out_specs=pl.BlockSpec((rows, tn), lambda j: (0, j)),
        out_shape=jax.ShapeDtypeStruct((rows, n), BF16),
        scratch_shapes=[pltpu.VMEM((rows, D_MODEL), BF16)],
        compiler_params=_params("arbitrary"),
        name="mem_kv",
    )(mem.reshape(rows, D_MODEL), g_mem, w_ckv)
    return kv.reshape(b, m, n)


def _cross_fold_kernel(k_ref, v_ref, wq_ref, wo_ref, m_ref, n_ref):
    m_ref[...] = (_dot_nt(wq_ref[...], k_ref[...]) * (CROSS_HEAD_DIM ** -0.5)).astype(BF16)
    n_ref[...] = _dot(v_ref[...], wo_ref[...]).astype(BF16)


def _cross_fold(kvm, w_cq, w_co):
    b, m, _ = kvm.shape
    return pl.pallas_call(
        _cross_fold_kernel,
        grid=(N_CROSS_HEADS, b),
        in_specs=[
            pl.BlockSpec((None, m, CROSS_HEAD_DIM), lambda hd, bi: (bi, 0, hd)),
            pl.BlockSpec((None, m, CROSS_HEAD_DIM), lambda hd, bi: (bi, 0, N_CROSS_HEADS + hd)),
            pl.BlockSpec((D_MODEL, CROSS_HEAD_DIM), lambda hd, bi: (0, hd)),
            pl.BlockSpec((CROSS_HEAD_DIM, D_MODEL), lambda hd, bi: (hd, 0)),
        ],
        out_specs=[
            pl.BlockSpec((None, D_MODEL, m), lambda hd, bi: (bi, 0, hd)),
            pl.BlockSpec((None, m, D_MODEL), lambda hd, bi: (bi, hd, 0)),
        ],
        out_shape=[jax.ShapeDtypeStruct((b, D_MODEL, N_CROSS_HEADS * m), BF16),
                   jax.ShapeDtypeStruct((b, N_CROSS_HEADS * m, D_MODEL), BF16)],
        compiler_params=_params("arbitrary", "arbitrary"),
        name="cross_fold",
    )(kvm, kvm, w_cq, w_co)


def _cross_kernel(x_ref, m_ref, n_ref, gc_ref, gn_ref, o_ref, h_ref):
    x = x_ref[...]
    h = _rms(x, gc_ref[...]).astype(BF16)
    s = _dot(h, m_ref[...])
    n_mem = m_ref.shape[1] // N_CROSS_HEADS
    probs = []
    for hd in range(N_CROSS_HEADS):
        sh = s[:, hd * n_mem:(hd + 1) * n_mem]
        p = jnp.exp(sh - jnp.max(sh, axis=-1, keepdims=True))
        probs.append((p / jnp.sum(p, axis=-1, keepdims=True)).astype(BF16))
    y = x + _dot(jnp.concatenate(probs, axis=1), n_ref[...])
    o_ref[...] = y
    h_ref[...] = _rms(y, gn_ref[...]).astype(BF16)


def _cross(x, m_all, n_all, g_cross, g_ffn, *, tm):
    b, s, _ = x.shape
    spec = pl.BlockSpec((None, tm, D_MODEL), lambda bi, i: (bi, i, 0))
    return pl.pallas_call(
        _cross_kernel,
        grid=(b, s // tm),
        in_specs=[
            spec,
            pl.BlockSpec((None,) + m_all.shape[1:], lambda bi, i: (bi, 0, 0)),
            pl.BlockSpec((None,) + n_all.shape[1:], lambda bi, i: (bi, 0, 0)),
            _const_spec(g_cross.shape),
            _const_spec(g_ffn.shape),
        ],
        out_specs=[spec, spec],
        out_shape=[jax.ShapeDtypeStruct((b, s, D_MODEL), F32), jax.ShapeDtypeStruct((b, s, D_MODEL), BF16)],
        compiler_params=_params("parallel", "parallel"),
        name="cross",
    )(x, m_all, n_all, g_cross, g_ffn)


def _ffn_up_kernel(h_ref, prev_ref, next_ref, wg_ref, wu_ref, cwb_ref, o_ref, hext_ref, *, seq):
    i = pl.program_id(0)
    j = pl.program_id(1)
    tm = h_ref.shape[0]

    @pl.when(j == 0)
    def _():
        first = (i * tm) % seq == 0
        last = ((i + 1) * tm) % seq == 0
        prev = prev_ref[...]
        nxt = next_ref[...]
        hext_ref[0:HALO, :] = jnp.where(first, jnp.zeros_like(prev), prev)
        hext_ref[HALO:HALO + tm, :] = h_ref[...]
        hext_ref[HALO + tm:, :] = jnp.where(last, jnp.zeros_like(nxt), nxt)

    g = _dot(hext_ref[...], wg_ref[...].astype(BF16))
    u = _dot(h_ref[...], wu_ref[...].astype(BF16))
    rows = g.shape[0]
    g_prev = pltpu.roll(g, 1, axis=0)[HALO:HALO + tm]
    g_next = pltpu.roll(g, rows - 1, axis=0)[HALO:HALO + tm]
    cwb = cwb_ref[j]
    c = g_prev * cwb[0:1] + g[HALO:HALO + tm] * cwb[1:2] + g_next * cwb[2:3] + cwb[3:4]
    o_ref[...] = (c / (1.0 + jnp.exp(-c)) * u).astype(BF16)


def _ffn_up(h, w_gate, w_up, conv_w, conv_b, *, seq, tm, tf):
    t = h.shape[0]
    hb = tm // HALO
    n_halo = t // HALO
    cwb = jnp.concatenate([conv_w, conv_b], axis=0).reshape(4, D_FF // tf, tf).transpose(1, 0, 2)
    return pl.pallas_call(
        functools.partial(_ffn_up_kernel, seq=seq),
        grid=(t // tm, D_FF // tf),
        in_specs=[
            pl.BlockSpec((tm, D_MODEL), lambda i, j: (i, 0)),
            pl.BlockSpec((HALO, D_MODEL), lambda i, j: (jnp.maximum(i * hb - 1, 0), 0)),
            pl.BlockSpec((HALO, D_MODEL), lambda i, j: (jnp.minimum((i + 1) * hb, n_halo - 1), 0)),
            pl.BlockSpec((D_MODEL, tf), lambda i, j: (0, j)),
            pl.BlockSpec((D_MODEL, tf), lambda i, j: (0, j)),
            _const_spec(cwb.shape),
        ],
        out_specs=pl.BlockSpec((tm, tf), lambda i, j: (i, j)),
        out_shape=jax.ShapeDtypeStruct((t, D_FF), BF16),
        scratch_shapes=[pltpu.VMEM((tm + 2 * HALO, D_MODEL), BF16)],
        compiler_params=_params("parallel", "arbitrary"),
        name="ffn_up",
    )(h, h, h, w_gate, w_up, cwb)


def _ffn_down_kernel(a_ref, w_ref, x_ref, g_ref, o_ref):
    o_ref[...] = _rms(x_ref[...] + _dot(a_ref[...], w_ref[...]), g_ref[...])


def _ffn_down(act, w_down, x, g_final, *, tm):
    t = x.shape[0]
    return pl.pallas_call(
        _ffn_down_kernel,
        grid=(t // tm,),
        in_specs=[
            pl.BlockSpec((tm, D_FF), lambda i: (i, 0)),
            _const_spec(w_down.shape),
            pl.BlockSpec((tm, D_MODEL), lambda i: (i, 0)),
            _const_spec(g_final.shape),
        ],
        out_specs=pl.BlockSpec((tm, D_MODEL), lambda i: (i, 0)),
        out_shape=jax.ShapeDtypeStruct((t, D_MODEL), F32),
        compiler_params=_params("parallel"),
        name="ffn_down",
    )(act, w_down, x, g_final)


def _dft_tables(seq):
    n1 = DFT_N1
    n2 = seq // n1
    c = np.arange(FOURIER_GROUP_DIM)
    ang = 2.0 * np.pi * ((c[:, None] * c[None, :]) % FOURIER_GROUP_DIM) / FOURIER_GROUP_DIM
    dft_c = np.concatenate([np.cos(ang), np.sin(ang)], axis=1)

    j = np.arange(n1)
    a1 = 2.0 * np.pi * ((j[:, None] * j[None, :]) % n1) / n1
    c1, s1 = np.cos(a1), np.sin(a1)
    m1 = np.block([[c1, -s1], [-s1, -c1]])

    k1 = np.arange(n1)[:, None, None]
    k2 = np.arange(n2)[None, :, None]
    m2 = np.arange(n2)[None, None, :]
    a2 = 2.0 * np.pi * ((m2 * (k1 + n1 * k2)) % seq) / seq
    e = np.concatenate([np.cos(a2), np.sin(a2)], axis=2)
    return (jnp.asarray(dft_c, BF16), jnp.asarray(m1, BF16), jnp.asarray(e, BF16))


def _rope_tables(seq):
    inv = ROPE_THETA ** (-jnp.arange(0, QK_ROPE_DIM, 2, dtype=F32) / QK_ROPE_DIM)
    ang = jnp.arange(seq, dtype=F32)[:, None] * inv[None, :]
    cos, sin = jnp.cos(ang), jnp.sin(ang)
    return (jnp.concatenate([cos, cos, cos, cos], axis=1), jnp.concatenate([-sin, sin, -sin, sin], axis=1),
            jnp.concatenate([cos.T, cos.T], axis=0), jnp.concatenate([-sin.T, sin.T], axis=0))


def _prep_weights(w_in, w_uq, w_ukv):
    half = QK_ROPE_DIM // 2
    c1 = FOURIER_DIM + Q_LORA_RANK + KV_LORA_RANK
    w_in_ext = jnp.concatenate(
        [w_in, w_in[:, c1 + half:c1 + QK_ROPE_DIM], w_in[:, c1:c1 + half]], axis=1).astype(BF16)
    q3 = w_uq.reshape(Q_LORA_RANK, N_HEADS, QK_DIM)
    w_uqt_ext = jnp.concatenate(
        [q3[:, :, :QK_NOPE_DIM].reshape(Q_LORA_RANK, -1), q3[:, :, QK_NOPE_DIM:].reshape(Q_LORA_RANK, -1)],
        axis=1).T.astype(BF16)
    kv3 = w_ukv.reshape(KV_LORA_RANK, N_HEADS, QK_NOPE_DIM + V_HEAD_DIM)
    w_k = kv3[:, :, :QK_NOPE_DIM].reshape(KV_LORA_RANK, -1).astype(BF16)
    w_vt = kv3[:, :, QK_NOPE_DIM:].reshape(KV_LORA_RANK, -1).T.astype(BF16)
    return w_in_ext, w_uqt_ext, w_k, w_vt


def _tile(n, pref):
    return pref if n % pref == 0 else n


def _tiles(s):
    attn_k = _tile(s, 512)
    return dict(
        in_proj=_tile(s, 512),
        attn_q=_tile(s, 1024), attn_tiles=4 if s % 4096 == 0 else 1, attn_streams=2 if s % 4096 == 0 else 1,
        attn_k=attn_k, attn_group=4 if (s // attn_k) % 4 == 0 else 2,
        fourier_tn2=min(16, s // DFT_N1), fourier_kb=32,
        out_proj=_tile(s, 512), mem_kv_cols=1024, cross=_tile(s, 512),
        ffn_up=_tile(s, 1024), ffn_cols=512, ffn_down=_tile(s, 256),
    )


def _trunk(x, mem, p, tables):
    b, s, _ = x.shape
    t = b * s
    n1 = DFT_N1
    n2 = s // n1
    ts = _tiles(s)
    dft_c, m1, e, cos2, sin2, cos_t, sin_t = tables

    ab, qt, k, vt = _in_proj(x, p["g_mix"], p["w_in_ext"], p["g_q"], p["w_uqt_ext"], p["g_kv"], p["w_k"],
                             p["w_vt"], dft_c, cos2, sin2, cos_t, sin_t, tm=ts["in_proj"], tk=ts["attn_k"])

    gc = _fourier_a(ab.reshape(b, N_FOURIER_GROUPS, 2, n1, n2, FOURIER_GROUP_DIM), m1, tn2=ts["fourier_tn2"])
    f = _fourier_b(gc.reshape(b, N_FOURIER_GROUPS, 2, n1, n2, FOURIER_GROUP_DIM), e, kb=ts["fourier_kb"],
                   scale=float((s * FOURIER_GROUP_DIM) ** -0.5))

    a = _attention(qt, k, vt, tq=ts["attn_q"], tiles_per_step=ts["attn_tiles"], n_streams=ts["attn_streams"],
                   group_size=ts["attn_group"])

    x1 = _out_proj(x.reshape(t, D_MODEL), f.reshape(t, FOURIER_DIM), a.reshape(t, MLA_DIM),
                   p["g_f"], p["g_a"], p["w_out"], tm=ts["out_proj"])

    kvm = _mem_kv(mem, p["g_mem"], p["w_ckv"], tn=ts["mem_kv_cols"])
    m_all, n_all = _cross_fold(kvm, p["w_cq"], p["w_co"])
    x2, hf = _cross(x1.reshape(b, s, D_MODEL), m_all, n_all, p["g_cross"], p["g_ffn"], tm=ts["cross"])

    act = _ffn_up(hf.reshape(t, D_MODEL), p["w_gate"], p["w_up"], p["conv_w"], p["conv_b"], seq=s,
                  tm=ts["ffn_up"], tf=ts["ffn_cols"])
    y = _ffn_down(act, p["w_down"], x2.reshape(t, D_MODEL), p["g_final"], tm=ts["ffn_down"])
    return y.reshape(b, s, D_MODEL)


@jax.jit
def kernel(x_prompt, x_sample, mem_prompt, mem_sample, norm_mix_g, w_in, q_norm_g, w_uq, kv_norm_g, w_ukv,
           fourier_out_g, mla_out_g, w_out, norm_cross_g, norm_mem_g, w_cq, w_ckv, w_co, norm_ffn_g, w_gate,
           w_up, conv_w, conv_b, w_down, final_norm_g):
    assert norm_mix_g.shape[0] == 1, "single-layer trunk"
    assert x_prompt.shape[1] % DFT_N1 == 0 and x_sample.shape[1] % DFT_N1 == 0
    w_in_ext, w_uqt_ext, w_k, w_vt = _prep_weights(w_in[0], w_uq[0], w_ukv[0])
    row = lambda g: g.reshape(1, -1)
    p = dict(
        g_mix=row(norm_mix_g[0]), w_in_ext=w_in_ext, g_q=row(q_norm_g[0]), w_uqt_ext=w_uqt_ext,
        g_kv=row(kv_norm_g[0]), w_k=w_k, w_vt=w_vt, g_f=row(fourier_out_g[0]), g_a=row(mla_out_g[0]),
        w_out=w_out[0].astype(BF16), g_cross=row(norm_cross_g[0]), g_mem=row(norm_mem_g[0]),
        w_cq=w_cq[0].astype(BF16), w_ckv=w_ckv.reshape(w_ckv.shape[1:]), w_co=w_co[0].astype(BF16),
        g_ffn=row(norm_ffn_g[0]), w_gate=w_gate.reshape(w_gate.shape[1:]), w_up=w_up.reshape(w_up.shape[1:]),
        conv_w=conv_w[0], conv_b=row(conv_b[0]), w_down=w_down[0].astype(BF16), g_final=row(final_norm_g),
    )
    outs = []
    for x, mem in ((x_prompt, mem_prompt), (x_sample, mem_sample)):
        s = x.shape[1]
        tables = _dft_tables(s) + _rope_tables(s)
        outs.append(_trunk(x, mem, p, tables))
    return tuple(outs)
```

```python
import functools

import numpy as np
import jax
import jax.numpy as jnp
from jax import lax
from jax.experimental import pallas as pl
from jax.experimental.pallas import tpu as pltpu

F32 = jnp.float32
BF16 = jnp.bfloat16

D_MODEL = 2048
N_FOURIER_GROUPS = 4
FOURIER_GROUP_DIM = 256
FOURIER_DIM = N_FOURIER_GROUPS * FOURIER_GROUP_DIM
N_HEADS = 8
QK_NOPE_DIM = 128
QK_ROPE_DIM = 64
QK_DIM = QK_NOPE_DIM + QK_ROPE_DIM
V_HEAD_DIM = 128
V_EXT_DIM = V_HEAD_DIM + 16
Q_LORA_RANK = 512
KV_LORA_RANK = 512
MLA_DIM = N_HEADS * V_HEAD_DIM
ROPE_THETA = 10000.0
N_CROSS_HEADS = 4
CROSS_HEAD_DIM = D_MODEL // N_CROSS_HEADS
D_FF = 5632
EPS = 1e-6
LOG2_E = 1.4426950408889634

V7X_VMEM_LIMIT_BYTES = 56 * 1024 * 1024
MXU_DIM = 256
DFT_N1 = MXU_DIM // 2
HALO = 16


def _params(*sem):
    return pltpu.CompilerParams(dimension_semantics=sem, vmem_limit_bytes=V7X_VMEM_LIMIT_BYTES)


def _const_spec(shape):
    zeros = (0,) * len(shape)
    return pl.BlockSpec(shape, lambda *_: zeros, pipeline_mode=pl.Buffered(1))


def _rms(x, g):
    return x * lax.rsqrt(jnp.mean(x * x, axis=-1, keepdims=True) + EPS) * g


def _dot(a, b):
    return jnp.dot(a, b, preferred_element_type=F32)


def _dot_nt(a, b):
    return lax.dot_general(a, b, (((1,), (1,)), ((), ())), preferred_element_type=F32)


def _in_proj_kernel(x_ref, g_ref, win_ref, qg_ref, wuqt_ref, kvg_ref, wk_ref, wvt_ref, dft_ref, cos_ref,
                    sin_ref, cost_ref, sint_ref, ab_ref, qt_ref, k_ref, vt_ref):
    h = _rms(x_ref[...], g_ref[...]).astype(BF16)
    z = _dot(h, win_ref[...])

    for g in range(N_FOURIER_GROUPS):
        u = z[:, g * FOURIER_GROUP_DIM:(g + 1) * FOURIER_GROUP_DIM].astype(BF16)
        ab = _dot(u, dft_ref[...])
        ab_ref[g, 0] = ab[:, :FOURIER_GROUP_DIM]
        ab_ref[g, 1] = ab[:, FOURIER_GROUP_DIM:]

    c0 = FOURIER_DIM
    c_q = z[:, c0:c0 + Q_LORA_RANK]
    c_kv = z[:, c0 + Q_LORA_RANK:c0 + Q_LORA_RANK + KV_LORA_RANK]
    c1 = c0 + Q_LORA_RANK + KV_LORA_RANK
    k_r = z[:, c1:c1 + QK_ROPE_DIM]
    k_r_sw = z[:, c1 + QK_ROPE_DIM:c1 + 2 * QK_ROPE_DIM]

    cos = cos_ref[...]
    sin = sin_ref[...]
    k_rope = (k_r * cos[:, :QK_ROPE_DIM] + k_r_sw * sin[:, :QK_ROPE_DIM]).astype(BF16)

    hq = _rms(c_q, qg_ref[...]).astype(BF16)
    qe_t = _dot_nt(wuqt_ref[...], hq)
    q_nope_w = N_HEADS * QK_NOPE_DIM
    cos_t = jnp.concatenate([cost_ref[...]] * N_HEADS, axis=0)
    sin_t = jnp.concatenate([sint_ref[...]] * N_HEADS, axis=0)
    q_r = qe_t[q_nope_w:]
    half = QK_ROPE_DIM // 2
    q_sw = jnp.concatenate(
        [q_r[lo:lo + half] for hd in range(N_HEADS) for lo in (hd * QK_ROPE_DIM + half, hd * QK_ROPE_DIM)], axis=0)
    q_rope_t = q_r * cos_t + q_sw * sin_t

    hkv = _rms(c_kv, kvg_ref[...]).astype(BF16)
    k_nope = _dot(hkv, wk_ref[...])
    v_t = _dot_nt(wvt_ref[...], hkv)

    scale = QK_DIM ** -0.5 * LOG2_E
    for hd in range(N_HEADS):
        qt_ref[hd, 0:QK_NOPE_DIM, :] = (qe_t[hd * QK_NOPE_DIM:(hd + 1) * QK_NOPE_DIM] * scale).astype(BF16)
        qt_ref[hd, QK_NOPE_DIM:QK_DIM, :] = (
            q_rope_t[hd * QK_ROPE_DIM:(hd + 1) * QK_ROPE_DIM] * scale).astype(BF16)
        k_ref[hd, :, 0:QK_NOPE_DIM] = k_nope[:, hd * QK_NOPE_DIM:(hd + 1) * QK_NOPE_DIM].astype(BF16)
        k_ref[hd, :, QK_NOPE_DIM:QK_DIM] = k_rope
        vt_ref[hd, 0:V_HEAD_DIM, :] = v_t[hd * V_HEAD_DIM:(hd + 1) * V_HEAD_DIM, :].astype(BF16)
        pad_row = lax.broadcasted_iota(jnp.int32, (V_EXT_DIM - V_HEAD_DIM, v_t.shape[1]), 0)
        vt_ref[hd, V_HEAD_DIM:, :] = jnp.where(pad_row == 0, 1.0, 0.0).astype(BF16)


def _in_proj(x, g_mix, w_in_ext, q_g, w_uqt_ext, kv_g, w_k, w_vt, dft_c, cos2, sin2, cos_t, sin_t, *, tm, tk):
    b, s, _ = x.shape
    grid = (b, s // tm)
    r = tk // tm
    return pl.pallas_call(
        _in_proj_kernel,
        grid=grid,
        in_specs=[
            pl.BlockSpec((None, tm, D_MODEL), lambda bi, i: (bi, i, 0)),
            _const_spec(g_mix.shape),
            _const_spec(w_in_ext.shape),
            _const_spec(q_g.shape),
            _const_spec(w_uqt_ext.shape),
            _const_spec(kv_g.shape),
            _const_spec(w_k.shape),
            _const_spec(w_vt.shape),
            _const_spec(dft_c.shape),
            pl.BlockSpec((tm, 128), lambda bi, i: (i, 0)),
            pl.BlockSpec((tm, 128), lambda bi, i: (i, 0)),
            pl.BlockSpec((QK_ROPE_DIM, tm), lambda bi, i: (0, i)),
            pl.BlockSpec((QK_ROPE_DIM, tm), lambda bi, i: (0, i)),
        ],
        out_specs=[
            pl.BlockSpec((None, N_FOURIER_GROUPS, 2, tm, FOURIER_GROUP_DIM), lambda bi, i: (bi, 0, 0, i, 0)),
            pl.BlockSpec((None, N_HEADS, QK_DIM, tm), lambda bi, i: (bi, 0, 0, i)),
            pl.BlockSpec((None, N_HEADS, tm, QK_DIM), lambda bi, i: (bi, 0, i, 0)),
            pl.BlockSpec((None, N_HEADS, None, V_EXT_DIM, tm), lambda bi, i: (bi, 0, i // r, 0, i % r)),
        ],
        out_shape=[
            jax.ShapeDtypeStruct((b, N_FOURIER_GROUPS, 2, s, FOURIER_GROUP_DIM), F32),
            jax.ShapeDtypeStruct((b, N_HEADS, QK_DIM, s), BF16),
            jax.ShapeDtypeStruct((b, N_HEADS, s, QK_DIM), BF16),
            jax.ShapeDtypeStruct((b, N_HEADS, s // tk, V_EXT_DIM, tk), BF16),
        ],
        compiler_params=_params("parallel", "parallel"),
        name="in_proj",
    )(x, g_mix, w_in_ext, q_g, w_uqt_ext, kv_g, w_k, w_vt, dft_c, cos2, sin2, cos_t, sin_t)


def _fourier_a_kernel(m_ref, x_ref, o_ref):
    two, n1, tn2, c = x_ref.shape
    for j in range(tn2):
        x = jnp.concatenate([x_ref[0, :, j, :], x_ref[1, :, j, :]], axis=0).astype(BF16)
        o_ref[:, :, j * c:(j + 1) * c] = _dot(m_ref[...], x).astype(BF16).reshape(two, n1, c)


def _fourier_a(ab, m1, *, tn2):
    b, g, two, n1, n2, c = ab.shape
    return pl.pallas_call(
        _fourier_a_kernel,
        grid=(b, g, n2 // tn2),
        in_specs=[
            _const_spec(m1.shape),
            pl.BlockSpec((None, None, two, n1, tn2, c), lambda bi, gi, ci: (bi, gi, 0, 0, ci, 0)),
        ],
        out_specs=pl.BlockSpec((None, None, two, n1, tn2 * c), lambda bi, gi, ci: (bi, gi, 0, 0, ci)),
        out_shape=jax.ShapeDtypeStruct((b, g, two, n1, n2 * c), BF16),
        compiler_params=_params("parallel", "parallel", "parallel"),
        name="fourier_a",
    )(m1, ab)


def _fourier_b_kernel(e_ref, g_ref, o_ref, *, scale):
    kb = e_ref.shape[0]
    for j in range(kb):
        x = jnp.concatenate([g_ref[0, j], g_ref[1, j]], axis=0)
        o_ref[:, j, :] = _dot(e_ref[j], x) * scale


def _fourier_b(gc, e, *, kb, scale):
    b, g, two, n1, n2, c = gc.shape
    return pl.pallas_call(
        functools.partial(_fourier_b_kernel, scale=scale),
        grid=(b, g, n1 // kb),
        in_specs=[
            pl.BlockSpec((kb, n2, 2 * n2), lambda bi, gi, ki: (ki, 0, 0)),
            pl.BlockSpec((None, None, two, kb, n2, c), lambda bi, gi, ki: (bi, gi, 0, ki, 0, 0)),
        ],
        out_specs=pl.BlockSpec((None, n2, kb, c), lambda bi, gi, ki: (bi, 0, ki, gi)),
        out_shape=jax.ShapeDtypeStruct((b, n2, n1, g * c), F32),
        compiler_params=_params("parallel", "parallel", "parallel"),
        name="fourier_b",
    )(e, gc)


def _attn_kernel(qt_ref, k_ref, vt_ref, o_ref, *scratch, group_size, n_streams):
    s_bufs, accs = scratch[:2 * n_streams], scratch[2 * n_streams:]
    tk, tq = s_bufs[0].shape
    nk = vt_ref.shape[0]
    n_tiles = qt_ref.shape[1] // tq

    def scores(tile, j, s_ref):
        start = pl.multiple_of(j * tk, tk)
        qt = qt_ref[:, tile * tq:(tile + 1) * tq]
        s = _dot(k_ref[pl.ds(start, tk), :], qt)
        s_ref[...] = s
        return jnp.max(s, axis=0, keepdims=True)

    def update(acc_ref, j, s_ref, cmax, m):
        m_new = jnp.maximum(m, cmax)
        p = jnp.exp2(s_ref[...] - m_new)
        acc_ref[...] = jnp.exp2(m - m_new) * acc_ref[...] + _dot(vt_ref[j], p.astype(BF16))
        return m_new

    def group(first_tile, j, carry, last):
        carry = list(carry)
        for t in range(group_size):
            for st in range(n_streams):
                tile = first_tile + st
                bufs = s_bufs[2 * st:2 * st + 2]
                cmax, m = carry[st]
                if not last or t + 1 < group_size:
                    nxt = scores(tile, j + t + 1, bufs[(t + 1) % 2])
                elif tile + n_streams < n_tiles:
                    nxt = scores(tile + n_streams, 0, bufs[0])
                else:
                    nxt = cmax
                carry[st] = (nxt, update(accs[st], j + t, bufs[t % 2], cmax, m))
        return tuple(carry)

    for acc_ref in accs:
        acc_ref[...] = jnp.zeros_like(acc_ref)
    cmaxes = [scores(st, 0, s_bufs[2 * st]) for st in range(n_streams)]
    for first_tile in range(0, n_tiles, n_streams):
        carry = tuple((c, jnp.full((1, tq), -jnp.inf, F32)) for c in cmaxes)
        carry = lax.fori_loop(0, nk // group_size - 1,
                              lambda i, c, ft=first_tile: group(ft, group_size * i, c, False), carry)
        carry = group(first_tile, nk - group_size, carry, True)
        cmaxes = [c for c, _ in carry]
        for st in range(n_streams):
            acc = accs[st][...]
            lo = (first_tile + st) * tq
            o_ref[lo:lo + tq, :] = (acc[0:V_HEAD_DIM] / acc[V_HEAD_DIM:V_HEAD_DIM + 1]).T


def _attention(qt, k, vt, *, tq, tiles_per_step, n_streams, group_size):
    b, h, s, _ = k.shape
    nk, _, tk = vt.shape[2:]
    assert group_size % 2 == 0 and nk % group_size == 0 and tiles_per_step % n_streams == 0
    rows = tq * tiles_per_step
    return pl.pallas_call(
        functools.partial(_attn_kernel, group_size=group_size, n_streams=n_streams),
        grid=(b, h, s // rows),
        in_specs=[
            pl.BlockSpec((None, None, QK_DIM, rows), lambda bi, hi, i: (bi, hi, 0, i)),
            pl.BlockSpec((None, None, s, QK_DIM), lambda bi, hi, i: (bi, hi, 0, 0)),
            pl.BlockSpec((None, None, nk, V_EXT_DIM, tk), lambda bi, hi, i: (bi, hi, 0, 0, 0)),
        ],
        out_specs=pl.BlockSpec((None, rows, V_HEAD_DIM), lambda bi, hi, i: (bi, i, hi)),
        out_shape=jax.ShapeDtypeStruct((b, s, h * V_HEAD_DIM), F32),
        scratch_shapes=([pltpu.VMEM((tk, tq), F32)] * (2 * n_streams)
                        + [pltpu.VMEM((V_EXT_DIM, tq), F32)] * n_streams),
        compiler_params=_params("parallel", "parallel", "parallel"),
        name="attention",
    )(qt, k, vt)


def _out_proj_kernel(x_ref, f_ref, a_ref, gf_ref, ga_ref, w_ref, o_ref):
    hf = _rms(f_ref[...], gf_ref[...]).astype(BF16)
    ha = _rms(a_ref[...], ga_ref[...]).astype(BF16)
    o_ref[...] = x_ref[...] + _dot(hf, w_ref[0:FOURIER_DIM, :]) + _dot(ha, w_ref[FOURIER_DIM:, :])


def _out_proj(x, f, a, g_f, g_a, w_out, *, tm):
    t = x.shape[0]
    return pl.pallas_call(
        _out_proj_kernel,
        grid=(t // tm,),
        in_specs=[
            pl.BlockSpec((tm, D_MODEL), lambda i: (i, 0)),
            pl.BlockSpec((tm, FOURIER_DIM), lambda i: (i, 0)),
            pl.BlockSpec((tm, MLA_DIM), lambda i: (i, 0)),
            _const_spec(g_f.shape),
            _const_spec(g_a.shape),
            _const_spec(w_out.shape),
        ],
        out_specs=pl.BlockSpec((tm, D_MODEL), lambda i: (i, 0)),
        out_shape=jax.ShapeDtypeStruct((t, D_MODEL), F32),
        compiler_params=_params("parallel"),
        name="out_proj",
    )(x, f, a, g_f, g_a, w_out)


def _mem_kv_kernel(m_ref, g_ref, w_ref, o_ref, mn_ref):
    @pl.when(pl.program_id(0) == 0)
    def _():
        mn_ref[...] = _rms(m_ref[...], g_ref[...]).astype(BF16)

    o_ref[...] = _dot(mn_ref[...], w_ref[...].astype(BF16)).astype(BF16)


def _mem_kv(mem, g_mem, w_ckv, *, tn):
    b, m, _ = mem.shape
    n = w_ckv.shape[1]
    rows = b * m
    kv = pl.pallas_call(
        _mem_kv_kernel,
        grid=(n // tn,),
        in_specs=[
            _const_spec((rows, D_MODEL)),
            _const_spec(g_mem.shape),
            pl.BlockSpec((D_MODEL, tn), lambda j: (0, j)),
        ],
        out_specs=pl.BlockSpec((rows, tn), lambda j: (0, j)),
        out_shape=jax.ShapeDtypeStruct((rows, n), BF16),
        scratch_shapes=[pltpu.VMEM((rows, D_MODEL), BF16)],
        compiler_params=_params("arbitrary"),
        name="mem_kv",
    )(mem.reshape(rows, D_MODEL), g_mem, w_ckv)
    return kv.reshape(b, m, n)


def _cross_fold_kernel(k_ref, v_ref, wq_ref, wo_ref, m_ref, n_ref):
    m_ref[...] = (_dot_nt(wq_ref[...], k_ref[...]) * (CROSS_HEAD_DIM ** -0.5)).astype(BF16)
    n_ref[...] = _dot(v_ref[...], wo_ref[...]).astype(BF16)


def _cross_fold(kvm, w_cq, w_co):
    b, m, _ = kvm.shape
    return pl.pallas_call(
        _cross_fold_kernel,
        grid=(N_CROSS_HEADS, b),
        in_specs=[
            pl.BlockSpec((None, m, CROSS_HEAD_DIM), lambda hd, bi: (bi, 0, hd)),
            pl.BlockSpec((None, m, CROSS_HEAD_DIM), lambda hd, bi: (bi, 0, N_CROSS_HEADS + hd)),
            pl.BlockSpec((D_MODEL, CROSS_HEAD_DIM), lambda hd, bi: (0, hd)),
            pl.BlockSpec((CROSS_HEAD_DIM, D_MODEL), lambda hd, bi: (hd, 0)),
        ],
        out_specs=[
            pl.BlockSpec((None, D_MODEL, m), lambda hd, bi: (bi, 0, hd)),
            pl.BlockSpec((None, m, D_MODEL), lambda hd, bi: (bi, hd, 0)),
        ],
        out_shape=[jax.ShapeDtypeStruct((b, D_MODEL, N_CROSS_HEADS * m), BF16),
                   jax.ShapeDtypeStruct((b, N_CROSS_HEADS * m, D_MODEL), BF16)],
        compiler_params=_params("arbitrary", "arbitrary"),
        name="cross_fold",
    )(kvm, kvm, w_cq, w_co)


def _cross_kernel(x_ref, m_ref, n_ref, gc_ref, gn_ref, o_ref, h_ref):
    x = x_ref[...]
    h = _rms(x, gc_ref[...]).astype(BF16)
    s = _dot(h, m_ref[...])
    n_mem = m_ref.shape[1] // N_CROSS_HEADS
    probs = []
    for hd in range(N_CROSS_HEADS):
        sh = s[:, hd * n_mem:(hd + 1) * n_mem]
        p = jnp.exp(sh - jnp.max(sh, axis=-1, keepdims=True))
        probs.append((p / jnp.sum(p, axis=-1, keepdims=True)).astype(BF16))
    y = x + _dot(jnp.concatenate(probs, axis=1), n_ref[...])
    o_ref[...] = y
    h_ref[...] = _rms(y, gn_ref[...]).astype(BF16)


def _cross(x, m_all, n_all, g_cross, g_ffn, *, tm):
    b, s, _ = x.shape
    spec = pl.BlockSpec((None, tm, D_MODEL), lambda bi, i: (bi, i, 0))
    return pl.pallas_call(
        _cross_kernel,
        grid=(b, s // tm),
        in_specs=[
            spec,
            pl.BlockSpec((None,) + m_all.shape[1:], lambda bi, i: (bi, 0, 0)),
            pl.BlockSpec((None,) + n_all.shape[1:], lambda bi, i: (bi, 0, 0)),
            _const_spec(g_cross.shape),
            _const_spec(g_ffn.shape),
        ],
        out_specs=[spec, spec],
        out_shape=[jax.ShapeDtypeStruct((b, s, D_MODEL), F32), jax.ShapeDtypeStruct((b, s, D_MODEL), BF16)],
        compiler_params=_params("parallel", "parallel"),
        name="cross",
    )(x, m_all, n_all, g_cross, g_ffn)


def _ffn_up_kernel(h_ref, prev_ref, next_ref, wg_ref, wu_ref, cwb_ref, o_ref, hext_ref, *, seq):
    i = pl.program_id(0)
    j = pl.program_id(1)
    tm = h_ref.shape[0]

    @pl.when(j == 0)
    def _():
        first = (i * tm) % seq == 0
        last = ((i + 1) * tm) % seq == 0
        prev = prev_ref[...]
        nxt = next_ref[...]
        hext_ref[0:HALO, :] = jnp.where(first, jnp.zeros_like(prev), prev)
        hext_ref[HALO:HALO + tm, :] = h_ref[...]
        hext_ref[HALO + tm:, :] = jnp.where(last, jnp.zeros_like(nxt), nxt)

    g = _dot(hext_ref[...], wg_ref[...].astype(BF16))
    u = _dot(h_ref[...], wu_ref[...].astype(BF16))
    rows = g.shape[0]
    g_prev = pltpu.roll(g, 1, axis=0)[HALO:HALO + tm]
    g_next = pltpu.roll(g, rows - 1, axis=0)[HALO:HALO + tm]
    cwb = cwb_ref[j]
    c = g_prev * cwb[0:1] + g[HALO:HALO + tm] * cwb[1:2] + g_next * cwb[2:3] + cwb[3:4]
    o_ref[...] = (c / (1.0 + jnp.exp(-c)) * u).astype(BF16)


def _ffn_up(h, w_gate, w_up, conv_w, conv_b, *, seq, tm, tf):
    t = h.shape[0]
    hb = tm // HALO
    n_halo = t // HALO
    cwb = jnp.concatenate([conv_w, conv_b], axis=0).reshape(4, D_FF // tf, tf).transpose(1, 0, 2)
    return pl.pallas_call(
        functools.partial(_ffn_up_kernel, seq=seq),
        grid=(t // tm, D_FF // tf),
        in_specs=[
            pl.BlockSpec((tm, D_MODEL), lambda i, j: (i, 0)),
            pl.BlockSpec((HALO, D_MODEL), lambda i, j: (jnp.maximum(i * hb - 1, 0), 0)),
            pl.BlockSpec((HALO, D_MODEL), lambda i, j: (jnp.minimum((i + 1) * hb, n_halo - 1), 0)),
            pl.BlockSpec((D_MODEL, tf), lambda i, j: (0, j)),
            pl.BlockSpec((D_MODEL, tf), lambda i, j: (0, j)),
            _const_spec(cwb.shape),
        ],
        out_specs=pl.BlockSpec((tm, tf), lambda i, j: (i, j)),
        out_shape=jax.ShapeDtypeStruct((t, D_FF), BF16),
        scratch_shapes=[pltpu.VMEM((tm + 2 * HALO, D_MODEL), BF16)],
        compiler_params=_params("parallel", "arbitrary"),
        name="ffn_up",
    )(h, h, h, w_gate, w_up, cwb)


def _ffn_down_kernel(a_ref, w_ref, x_ref, g_ref, o_ref):
    o_ref[...] = _rms(x_ref[...] + _dot(a_ref[...], w_ref[...]), g_ref[...])


def _ffn_down(act, w_down, x, g_final, *, tm):
    t = x.shape[0]
    return pl.pallas_call(
        _ffn_down_kernel,
        grid=(t // tm,),
        in_specs=[
            pl.BlockSpec((tm, D_FF), lambda i: (i, 0)),
            _const_spec(w_down.shape),
            pl.BlockSpec((tm, D_MODEL), lambda i: (i, 0)),
            _const_spec(g_final.shape),
        ],
        out_specs=pl.BlockSpec((tm, D_MODEL), lambda i: (i, 0)),
        out_shape=jax.ShapeDtypeStruct((t, D_MODEL), F32),
        compiler_params=_params("parallel"),
        name="ffn_down",
    )(act, w_down, x, g_final)


def _dft_tables(seq):
    n1 = DFT_N1
    n2 = seq // n1
    c = np.arange(FOURIER_GROUP_DIM)
    ang = 2.0 * np.pi * ((c[:, None] * c[None, :]) % FOURIER_GROUP_DIM) / FOURIER_GROUP_DIM
    dft_c = np.concatenate([np.cos(ang), np.sin(ang)], axis=1)

    j = np.arange(n1)
    a1 = 2.0 * np.pi * ((j[:, None] * j[None, :]) % n1) / n1
    c1, s1 = np.cos(a1), np.sin(a1)
    m1 = np.block([[c1, -s1], [-s1, -c1]])

    k1 = np.arange(n1)[:, None, None]
    k2 = np.arange(n2)[None, :, None]
    m2 = np.arange(n2)[None, None, :]
    a2 = 2.0 * np.pi * ((m2 * (k1 + n1 * k2)) % seq) / seq
    e = np.concatenate([np.cos(a2), np.sin(a2)], axis=2)
    return (jnp.asarray(dft_c, BF16), jnp.asarray(m1, BF16), jnp.asarray(e, BF16))


def _rope_tables(seq):
    inv = ROPE_THETA ** (-jnp.arange(0, QK_ROPE_DIM, 2, dtype=F32) / QK_ROPE_DIM)
    ang = jnp.arange(seq, dtype=F32)[:, None] * inv[None, :]
    cos, sin = jnp.cos(ang), jnp.sin(ang)
    return (jnp.concatenate([cos, cos, cos, cos], axis=1), jnp.concatenate([-sin, sin, -sin, sin], axis=1),
            jnp.concatenate([cos.T, cos.T], axis=0), jnp.concatenate([-sin.T, sin.T], axis=0))


def _prep_weights(w_in, w_uq, w_ukv):
    half = QK_ROPE_DIM // 2
    c1 = FOURIER_DIM + Q_LORA_RANK + KV_LORA_RANK
    w_in_ext = jnp.concatenate(
        [w_in, w_in[:, c1 + half:c1 + QK_ROPE_DIM], w_in[:, c1:c1 + half]], axis=1).astype(BF16)
    q3 = w_uq.reshape(Q_LORA_RANK, N_HEADS, QK_DIM)
    w_uqt_ext = jnp.concatenate(
        [q3[:, :, :QK_NOPE_DIM].reshape(Q_LORA_RANK, -1), q3[:, :, QK_NOPE_DIM:].reshape(Q_LORA_RANK, -1)],
        axis=1).T.astype(BF16)
    kv3 = w_ukv.reshape(KV_LORA_RANK, N_HEADS, QK_NOPE_DIM + V_HEAD_DIM)
    w_k = kv3[:, :, :QK_NOPE_DIM].reshape(KV_LORA_RANK, -1).astype(BF16)
    w_vt = kv3[:, :, QK_NOPE_DIM:].reshape(KV_LORA_RANK, -1).T.astype(BF16)
    return w_in_ext, w_uqt_ext, w_k, w_vt


def _tile(n, pref):
    return pref if n % pref == 0 else n


def _tiles(s):
    attn_k = _tile(s, 512)
    return dict(
        in_proj=_tile(s, 512),
        attn_q=_tile(s, 1024), attn_tiles=4 if s % 4096 == 0 else 1, attn_streams=4 if s % 4096 == 0 else 1,
        attn_k=attn_k, attn_group=4 if (s // attn_k) % 4 == 0 else 2,
        fourier_tn2=min(16, s // DFT_N1), fourier_kb=32,
        out_proj=_tile(s, 512), mem_kv_cols=1024, cross=_tile(s, 512),
        ffn_up=_tile(s, 1024), ffn_cols=512, ffn_down=_tile(s, 256),
    )


def _trunk(x, mem, p, tables):
    b, s, _ = x.shape
    t = b * s
    n1 = DFT_N1
    n2 = s // n1
    ts = _tiles(s)
    dft_c, m1, e, cos2, sin2, cos_t, sin_t = tables

    ab, qt, k, vt = _in_proj(x, p["g_mix"], p["w_in_ext"], p["g_q"], p["w_uqt_ext"], p["g_kv"], p["w_k"],
                             p["w_vt"], dft_c, cos2, sin2, cos_t, sin_t, tm=ts["in_proj"], tk=ts["attn_k"])

    gc = _fourier_a(ab.reshape(b, N_FOURIER_GROUPS, 2, n1, n2, FOURIER_GROUP_DIM), m1, tn2=ts["fourier_tn2"])
    f = _fourier_b(gc.reshape(b, N_FOURIER_GROUPS, 2, n1, n2, FOURIER_GROUP_DIM), e, kb=ts["fourier_kb"],
                   scale=float((s * FOURIER_GROUP_DIM) ** -0.5))

    a = _attention(qt, k, vt, tq=ts["attn_q"], tiles_per_step=ts["attn_tiles"], n_streams=ts["attn_streams"],
                   group_size=ts["attn_group"])

    x1 = _out_proj(x.reshape(t, D_MODEL), f.reshape(t, FOURIER_DIM), a.reshape(t, MLA_DIM),
                   p["g_f"], p["g_a"], p["w_out"], tm=ts["out_proj"])

    kvm = _mem_kv(mem, p["g_mem"], p["w_ckv"], tn=ts["mem_kv_cols"])
    m_all, n_all = _cross_fold(kvm, p["w_cq"], p["w_co"])
    x2, hf = _cross(x1.reshape(b, s, D_MODEL), m_all, n_all, p["g_cross"], p["g_ffn"], tm=ts["cross"])

    act = _ffn_up(hf.reshape(t, D_MODEL), p["w_gate"], p["w_up"], p["conv_w"], p["conv_b"], seq=s,
                  tm=ts["ffn_up"], tf=ts["ffn_cols"])
    y = _ffn_down(act, p["w_down"], x2.reshape(t, D_MODEL), p["g_final"], tm=ts["ffn_down"])
    return y.reshape(b, s, D_MODEL)


@jax.jit
def kernel(x_prompt, x_sample, mem_prompt, mem_sample, norm_mix_g, w_in, q_norm_g, w_uq, kv_norm_g, w_ukv,
           fourier_out_g, mla_out_g, w_out, norm_cross_g, norm_mem_g, w_cq, w_ckv, w_co, norm_ffn_g, w_gate,
           w_up, conv_w, conv_b, w_down, final_norm_g):
    assert norm_mix_g.shape[0] == 1, "single-layer trunk"
    assert x_prompt.shape[1] % DFT_N1 == 0 and x_sample.shape[1] % DFT_N1 == 0
    w_in_ext, w_uqt_ext, w_k, w_vt = _prep_weights(w_in[0], w_uq[0], w_ukv[0])
    row = lambda g: g.reshape(1, -1)
    p = dict(
        g_mix=row(norm_mix_g[0]), w_in_ext=w_in_ext, g_q=row(q_norm_g[0]), w_uqt_ext=w_uqt_ext,
        g_kv=row(kv_norm_g[0]), w_k=w_k, w_vt=w_vt, g_f=row(fourier_out_g[0]), g_a=row(mla_out_g[0]),
        w_out=w_out[0].astype(BF16), g_cross=row(norm_cross_g[0]), g_mem=row(norm_mem_g[0]),
        w_cq=w_cq[0].astype(BF16), w_ckv=w_ckv.reshape(w_ckv.shape[1:]), w_co=w_co[0].astype(BF16),
        g_ffn=row(norm_ffn_g[0]), w_gate=w_gate.reshape(w_gate.shape[1:]), w_up=w_up.reshape(w_up.shape[1:]),
        conv_w=conv_w[0], conv_b=row(conv_b[0]), w_down=w_down[0].astype(BF16), g_final=row(final_norm_g),
    )
    outs = []
    for x, mem in ((x_prompt, mem_prompt), (x_sample, mem_sample)):
        s = x.shape[1]
        tables = _dft_tables(s) + _rope_tables(s)
        outs.append(_trunk(x, mem, p, tables))
    return tuple(outs)
```

```python
import functools

import numpy as np
import jax
import jax.numpy as jnp
from jax import lax
from jax.experimental import pallas as pl
from jax.experimental.pallas import tpu as pltpu

F32 = jnp.float32
BF16 = jnp.bfloat16

D_MODEL = 2048
N_FOURIER_GROUPS = 4
FOURIER_GROUP_DIM = 256
FOURIER_DIM = N_FOURIER_GROUPS * FOURIER_GROUP_DIM
N_HEADS = 8
QK_NOPE_DIM = 128
QK_ROPE_DIM = 64
QK_DIM = QK_NOPE_DIM + QK_ROPE_DIM
ROPE_TABLE_DIM = 2 * QK_ROPE_DIM
V_HEAD_DIM = 128
V_EXT_DIM = V_HEAD_DIM + 16
Q_LORA_RANK = 512
KV_LORA_RANK = 512
MLA_DIM = N_HEADS * V_HEAD_DIM
ROPE_THETA = 10000.0
N_CROSS_HEADS = 4
CROSS_HEAD_DIM = D_MODEL // N_CROSS_HEADS
D_FF = 5632
EPS = 1e-6
LOG2_E = 1.4426950408889634

V7X_VMEM_LIMIT_BYTES = 56 * 1024 * 1024
MXU_DIM = 256
DFT_N1 = MXU_DIM // 2
HALO = 16


def _params(*sem):
    return pltpu.CompilerParams(dimension_semantics=sem, vmem_limit_bytes=V7X_VMEM_LIMIT_BYTES)


def _const_spec(shape):
    zeros = (0,) * len(shape)
    return pl.BlockSpec(shape, lambda *_: zeros, pipeline_mode=pl.Buffered(1))


def _rms(x, g):
    return x * lax.rsqrt(jnp.mean(x * x, axis=-1, keepdims=True) + EPS) * g


def _dot(a, b):
    return jnp.dot(a, b, preferred_element_type=F32)


def _dot_nt(a, b):
    return lax.dot_general(a, b, (((1,), (1,)), ((), ())), preferred_element_type=F32)


def _in_proj_kernel(x_ref, g_ref, win_ref, qg_ref, wuqt_ref, kvg_ref, wk_ref, wvt_ref, dft_ref, cos_ref,
                    sin_ref, cost_ref, sint_ref, ab_ref, qt_ref, k_ref, vt_ref):
    h = _rms(x_ref[...], g_ref[...]).astype(BF16)
    z = _dot(h, win_ref[...])

    for g in range(N_FOURIER_GROUPS):
        u = z[:, g * FOURIER_GROUP_DIM:(g + 1) * FOURIER_GROUP_DIM].astype(BF16)
        ab = _dot(u, dft_ref[...])
        ab_ref[g, 0] = ab[:, :FOURIER_GROUP_DIM]
        ab_ref[g, 1] = ab[:, FOURIER_GROUP_DIM:]

    c0 = FOURIER_DIM
    c_q = z[:, c0:c0 + Q_LORA_RANK]
    c_kv = z[:, c0 + Q_LORA_RANK:c0 + Q_LORA_RANK + KV_LORA_RANK]
    c1 = c0 + Q_LORA_RANK + KV_LORA_RANK
    k_r = z[:, c1:c1 + QK_ROPE_DIM]
    k_r_sw = z[:, c1 + QK_ROPE_DIM:c1 + 2 * QK_ROPE_DIM]

    cos = cos_ref[...]
    sin = sin_ref[...]
    k_rope = (k_r * cos[:, :QK_ROPE_DIM] + k_r_sw * sin[:, :QK_ROPE_DIM]).astype(BF16)

    hq = _rms(c_q, qg_ref[...]).astype(BF16)
    qe_t = _dot_nt(wuqt_ref[...], hq)
    q_nope_w = N_HEADS * QK_NOPE_DIM
    cos_t = jnp.concatenate([cost_ref[...]] * N_HEADS, axis=0)
    sin_t = jnp.concatenate([sint_ref[...]] * N_HEADS, axis=0)
    q_r = qe_t[q_nope_w:]
    half = QK_ROPE_DIM // 2
    q_sw = jnp.concatenate(
        [q_r[lo:lo + half] for hd in range(N_HEADS) for lo in (hd * QK_ROPE_DIM + half, hd * QK_ROPE_DIM)], axis=0)
    q_rope_t = q_r * cos_t + q_sw * sin_t

    hkv = _rms(c_kv, kvg_ref[...]).astype(BF16)
    k_nope = _dot(hkv, wk_ref[...])
    v_t = _dot_nt(wvt_ref[...], hkv)

    scale = QK_DIM ** -0.5 * LOG2_E
    for hd in range(N_HEADS):
        qt_ref[hd, 0:QK_NOPE_DIM, :] = (qe_t[hd * QK_NOPE_DIM:(hd + 1) * QK_NOPE_DIM] * scale).astype(BF16)
        qt_ref[hd, QK_NOPE_DIM:QK_DIM, :] = (
            q_rope_t[hd * QK_ROPE_DIM:(hd + 1) * QK_ROPE_DIM] * scale).astype(BF16)
        k_ref[hd, :, 0:QK_NOPE_DIM] = k_nope[:, hd * QK_NOPE_DIM:(hd + 1) * QK_NOPE_DIM].astype(BF16)
        k_ref[hd, :, QK_NOPE_DIM:QK_DIM] = k_rope
        vt_ref[hd, 0:V_HEAD_DIM, :] = v_t[hd * V_HEAD_DIM:(hd + 1) * V_HEAD_DIM, :].astype(BF16)
        pad_row = lax.broadcasted_iota(jnp.int32, (V_EXT_DIM - V_HEAD_DIM, v_t.shape[1]), 0)
        vt_ref[hd, V_HEAD_DIM:, :] = jnp.where(pad_row == 0, 1.0, 0.0).astype(BF16)


def _in_proj(x, g_mix, w_in_ext, q_g, w_uqt_ext, kv_g, w_k, w_vt, dft_c, cos2, sin2, cos_t, sin_t, *, tm, tk):
    b, s, _ = x.shape
    grid = (b, s // tm)
    r = tk // tm
    return pl.pallas_call(
        _in_proj_kernel,
        grid=grid,
        in_specs=[
            pl.BlockSpec((None, tm, D_MODEL), lambda bi, i: (bi, i, 0)),
            _const_spec(g_mix.shape),
            _const_spec(w_in_ext.shape),
            _const_spec(q_g.shape),
            _const_spec(w_uqt_ext.shape),
            _const_spec(kv_g.shape),
            _const_spec(w_k.shape),
            _const_spec(w_vt.shape),
            _const_spec(dft_c.shape),
            pl.BlockSpec((tm, ROPE_TABLE_DIM), lambda bi, i: (i, 0)),
            pl.BlockSpec((tm, ROPE_TABLE_DIM), lambda bi, i: (i, 0)),
            pl.BlockSpec((QK_ROPE_DIM, tm), lambda bi, i: (0, i)),
            pl.BlockSpec((QK_ROPE_DIM, tm), lambda bi, i: (0, i)),
        ],
        out_specs=[
            pl.BlockSpec((None, N_FOURIER_GROUPS, 2, tm, FOURIER_GROUP_DIM), lambda bi, i: (bi, 0, 0, i, 0)),
            pl.BlockSpec((None, N_HEADS, QK_DIM, tm), lambda bi, i: (bi, 0, 0, i)),
            pl.BlockSpec((None, N_HEADS, tm, QK_DIM), lambda bi, i: (bi, 0, i, 0)),
            pl.BlockSpec((None, N_HEADS, None, V_EXT_DIM, tm), lambda bi, i: (bi, 0, i // r, 0, i % r)),
        ],
        out_shape=[
            jax.ShapeDtypeStruct((b, N_FOURIER_GROUPS, 2, s, FOURIER_GROUP_DIM), F32),
            jax.ShapeDtypeStruct((b, N_HEADS, QK_DIM, s), BF16),
            jax.ShapeDtypeStruct((b, N_HEADS, s, QK_DIM), BF16),
            jax.ShapeDtypeStruct((b, N_HEADS, s // tk, V_EXT_DIM, tk), BF16),
        ],
        compiler_params=_params("parallel", "parallel"),
        name="in_proj",
    )(x, g_mix, w_in_ext, q_g, w_uqt_ext, kv_g, w_k, w_vt, dft_c, cos2, sin2, cos_t, sin_t)


def _fourier_a_kernel(m_ref, x_ref, o_ref):
    two, n1, tn2, c = x_ref.shape
    for j in range(tn2):
        x = jnp.concatenate([x_ref[0, :, j, :], x_ref[1, :, j, :]], axis=0).astype(BF16)
        o_ref[:, :, j * c:(j + 1) * c] = _dot(m_ref[...], x).astype(BF16).reshape(two, n1, c)


def _fourier_a(ab, m1, *, tn2):
    b, g, two, n1, n2, c = ab.shape
    return pl.pallas_call(
        _fourier_a_kernel,
        grid=(b, g, n2 // tn2),
        in_specs=[
            _const_spec(m1.shape),
            pl.BlockSpec((None, None, two, n1, tn2, c), lambda bi, gi, ci: (bi, gi, 0, 0, ci, 0)),
        ],
        out_specs=pl.BlockSpec((None, None, two, n1, tn2 * c), lambda bi, gi, ci: (bi, gi, 0, 0, ci)),
        out_shape=jax.ShapeDtypeStruct((b, g, two, n1, n2 * c), BF16),
        compiler_params=_params("parallel", "parallel", "parallel"),
        name="fourier_a",
    )(m1, ab)


def _fourier_b_kernel(e_ref, g_ref, o_ref, *, scale):
    kb = e_ref.shape[0]
    for j in range(kb):
        x = jnp.concatenate([g_ref[0, j], g_ref[1, j]], axis=0)
        o_ref[:, j, :] = _dot(e_ref[j], x) * scale


def _fourier_b(gc, e, *, kb, scale):
    b, g, two, n1, n2, c = gc.shape
    return pl.pallas_call(
        functools.partial(_fourier_b_kernel, scale=scale),
        grid=(b, g, n1 // kb),
        in_specs=[
            pl.BlockSpec((kb, n2, 2 * n2), lambda bi, gi, ki: (ki, 0, 0)),
            pl.BlockSpec((None, None, two, kb, n2, c), lambda bi, gi, ki: (bi, gi, 0, ki, 0, 0)),
        ],
        out_specs=pl.BlockSpec((None, n2, kb, c), lambda bi, gi, ki: (bi, 0, ki, gi)),
        out_shape=jax.ShapeDtypeStruct((b, n2, n1, g * c), F32),
        compiler_params=_params("parallel", "parallel", "parallel"),
        name="fourier_b",
    )(e, gc)


def _attn_kernel(qt_ref, k_ref, vt_ref, o_ref, *scratch, group_size, n_streams):
    s_bufs, accs = scratch[:2 * n_streams], scratch[2 * n_streams:]
    tk, tq = s_bufs[0].shape
    nk = vt_ref.shape[0]
    n_tiles = qt_ref.shape[1] // tq

    def scores(tile, j, s_ref):
        start = pl.multiple_of(j * tk, tk)
        qt = qt_ref[:, tile * tq:(tile + 1) * tq]
        s = _dot(k_ref[pl.ds(start, tk), :], qt)
        s_ref[...] = s
        return jnp.max(s, axis=0, keepdims=True)

    def update(acc_ref, j, s_ref, cmax, m):
        m_new = jnp.maximum(m, cmax)
        p = jnp.exp2(s_ref[...] - m_new)
        acc_ref[...] = jnp.exp2(m - m_new) * acc_ref[...] + _dot(vt_ref[j], p.astype(BF16))
        return m_new

    def group(first_tile, j, carry, last):
        carry = list(carry)
        for t in range(group_size):
            for st in range(n_streams):
                tile = first_tile + st
                bufs = s_bufs[2 * st:2 * st + 2]
                cmax, m = carry[st]
                if not last or t + 1 < group_size:
                    nxt = scores(tile, j + t + 1, bufs[(t + 1) % 2])
                elif tile + n_streams < n_tiles:
                    nxt = scores(tile + n_streams, 0, bufs[0])
                else:
                    nxt = cmax
                carry[st] = (nxt, update(accs[st], j + t, bufs[t % 2], cmax, m))
        return tuple(carry)

    for acc_ref in accs:
        acc_ref[...] = jnp.zeros_like(acc_ref)
    cmaxes = [scores(st, 0, s_bufs[2 * st]) for st in range(n_streams)]
    for first_tile in range(0, n_tiles, n_streams):
        carry = tuple((c, jnp.full((1, tq), -jnp.inf, F32)) for c in cmaxes)
        carry = lax.fori_loop(0, nk // group_size - 1,
                              lambda i, c, ft=first_tile: group(ft, group_size * i, c, False), carry)
        carry = group(first_tile, nk - group_size, carry, True)
        cmaxes = [c for c, _ in carry]
        for st in range(n_streams):
            acc = accs[st][...]
            lo = (first_tile + st) * tq
            o_ref[lo:lo + tq, :] = (acc[0:V_HEAD_DIM] / acc[V_HEAD_DIM:V_HEAD_DIM + 1]).T


def _attention(qt, k, vt, *, tq, tiles_per_step, n_streams, group_size):
    b, h, s, _ = k.shape
    nk, _, tk = vt.shape[2:]
    assert group_size % 2 == 0 and nk % group_size == 0 and tiles_per_step % n_streams == 0
    rows = tq * tiles_per_step
    return pl.pallas_call(
        functools.partial(_attn_kernel, group_size=group_size, n_streams=n_streams),
        grid=(b, h, s // rows),
        in_specs=[
            pl.BlockSpec((None, None, QK_DIM, rows), lambda bi, hi, i: (bi, hi, 0, i)),
            pl.BlockSpec((None, None, s, QK_DIM), lambda bi, hi, i: (bi, hi, 0, 0)),
            pl.BlockSpec((None, None, nk, V_EXT_DIM, tk), lambda bi, hi, i: (bi, hi, 0, 0, 0)),
        ],
        out_specs=pl.BlockSpec((None, rows, V_HEAD_DIM), lambda bi, hi, i: (bi, i, hi)),
        out_shape=jax.ShapeDtypeStruct((b, s, h * V_HEAD_DIM), F32),
        scratch_shapes=([pltpu.VMEM((tk, tq), F32)] * (2 * n_streams)
                        + [pltpu.VMEM((V_EXT_DIM, tq), F32)] * n_streams),
        compiler_params=_params("parallel", "parallel", "parallel"),
        name="attention",
    )(qt, k, vt)


def _out_proj_kernel(x_ref, f_ref, a_ref, gf_ref, ga_ref, w_ref, o_ref):
    hf = _rms(f_ref[...], gf_ref[...]).astype(BF16)
    ha = _rms(a_ref[...], ga_ref[...]).astype(BF16)
    o_ref[...] = x_ref[...] + _dot(hf, w_ref[0:FOURIER_DIM, :]) + _dot(ha, w_ref[FOURIER_DIM:, :])


def _out_proj(x, f, a, g_f, g_a, w_out, *, tm):
    t = x.shape[0]
    return pl.pallas_call(
        _out_proj_kernel,
        grid=(t // tm,),
        in_specs=[
            pl.BlockSpec((tm, D_MODEL), lambda i: (i, 0)),
            pl.BlockSpec((tm, FOURIER_DIM), lambda i: (i, 0)),
            pl.BlockSpec((tm, MLA_DIM), lambda i: (i, 0)),
            _const_spec(g_f.shape),
            _const_spec(g_a.shape),
            _const_spec(w_out.shape),
        ],
        out_specs=pl.BlockSpec((tm, D_MODEL), lambda i: (i, 0)),
        out_shape=jax.ShapeDtypeStruct((t, D_MODEL), F32),
        compiler_params=_params("parallel"),
        name="out_proj",
    )(x, f, a, g_f, g_a, w_out)


def _mem_kv_kernel(m_ref, g_ref, w_ref, o_ref, mn_ref):
    @pl.when(pl.program_id(0) == 0)
    def _():
        mn_ref[...] = _rms(m_ref[...], g_ref[...]).astype(BF16)

    o_ref[...] = _dot(mn_ref[...], w_ref[...].astype(BF16)).astype(BF16)


def _mem_kv(mem, g_mem, w_ckv, *, tn):
    b, m, _ = mem.shape
    n = w_ckv.shape[1]
    rows = b * m
    kv = pl.pallas_call(
        _mem_kv_kernel,
        grid=(n // tn,),
        in_specs=[
            _const_spec((rows, D_MODEL)),
            _const_spec(g_mem.shape),
            pl.BlockSpec((D_MODEL, tn), lambda j: (0, j)),
        ],
        out_specs=pl.BlockSpec((rows, tn), lambda j: (0, j)),
        out_shape=jax.ShapeDtypeStruct((rows, n), BF16),
        scratch_shapes=[pltpu.VMEM((rows, D_MODEL), BF16)],
        compiler_params=_params("arbitrary"),
        name="mem_kv",
    )(mem.reshape(rows, D_MODEL), g_mem, w_ckv)
    return kv.reshape(b, m, n)


def _cross_fold_kernel(k_ref, v_ref, wq_ref, wo_ref, m_ref, n_ref):
    m_ref[...] = (_dot_nt(wq_ref[...], k_ref[...]) * (CROSS_HEAD_DIM ** -0.5)).astype(BF16)
    n_ref[...] = _dot(v_ref[...], wo_ref[...]).astype(BF16)


def _cross_fold(kvm, w_cq, w_co):
    b, m, _ = kvm.shape
    return pl.pallas_call(
        _cross_fold_kernel,
        grid=(N_CROSS_HEADS, b),
        in_specs=[
            pl.BlockSpec((None, m, CROSS_HEAD_DIM), lambda hd, bi: (bi, 0, hd)),
            pl.BlockSpec((None, m, CROSS_HEAD_DIM), lambda hd, bi: (bi, 0, N_CROSS_HEADS + hd)),
            pl.BlockSpec((D_MODEL, CROSS_HEAD_DIM), lambda hd, bi: (0, hd)),
            pl.BlockSpec((CROSS_HEAD_DIM, D_MODEL), lambda hd, bi: (hd, 0)),
        ],
        out_specs=[
            pl.BlockSpec((None, D_MODEL, m), lambda hd, bi: (bi, 0, hd)),
            pl.BlockSpec((None, m, D_MODEL), lambda hd, bi: (bi, hd, 0)),
        ],
        out_shape=[jax.ShapeDtypeStruct((b, D_MODEL, N_CROSS_HEADS * m), BF16),
                   jax.ShapeDtypeStruct((b, N_CROSS_HEADS * m, D_MODEL), BF16)],
        compiler_params=_params("arbitrary", "arbitrary"),
        name="cross_fold",
    )(kvm, kvm, w_cq, w_co)


def _cross_kernel(x_ref, m_ref, n_ref, gc_ref, gn_ref, o_ref, h_ref):
    x = x_ref[...]
    h = _rms(x, gc_ref[...]).astype(BF16)
    s = _dot(h, m_ref[...])
    n_mem = m_ref.shape[1] // N_CROSS_HEADS
    probs = []
    for hd in range(N_CROSS_HEADS):
        sh = s[:, hd * n_mem:(hd + 1) * n_mem]
        p = jnp.exp(sh - jnp.max(sh, axis=-1, keepdims=True))
        probs.append((p / jnp.sum(p, axis=-1, keepdims=True)).astype(BF16))
    y = x + _dot(jnp.concatenate(probs, axis=1), n_ref[...])
    o_ref[...] = y
    h_ref[...] = _rms(y, gn_ref[...]).astype(BF16)


def _cross(x, m_all, n_all, g_cross, g_ffn, *, tm):
    b, s, _ = x.shape
    spec = pl.BlockSpec((None, tm, D_MODEL), lambda bi, i: (bi, i, 0))
    return pl.pallas_call(
        _cross_kernel,
        grid=(b, s // tm),
        in_specs=[
            spec,
            pl.BlockSpec((None,) + m_all.shape[1:], lambda bi, i: (bi, 0, 0)),
            pl.BlockSpec((None,) + n_all.shape[1:], lambda bi, i: (bi, 0, 0)),
            _const_spec(g_cross.shape),
            _const_spec(g_ffn.shape),
        ],
        out_specs=[spec, spec],
        out_shape=[jax.ShapeDtypeStruct((b, s, D_MODEL), F32), jax.ShapeDtypeStruct((b, s, D_MODEL), BF16)],
        compiler_params=_params("parallel", "parallel"),
        name="cross",
    )(x, m_all, n_all, g_cross, g_ffn)


def _ffn_up_kernel(h_ref, prev_ref, next_ref, wg_ref, wu_ref, cwb_ref, o_ref, hext_ref, *, seq):
    i = pl.program_id(0)
    j = pl.program_id(1)
    tm = h_ref.shape[0]

    @pl.when(j == 0)
    def _():
        first = (i * tm) % seq == 0
        last = ((i + 1) * tm) % seq == 0
        prev = prev_ref[...]
        nxt = next_ref[...]
        hext_ref[0:HALO, :] = jnp.where(first, jnp.zeros_like(prev), prev)
        hext_ref[HALO:HALO + tm, :] = h_ref[...]
        hext_ref[HALO + tm:, :] = jnp.where(last, jnp.zeros_like(nxt), nxt)

    g = _dot(hext_ref[...], wg_ref[...].astype(BF16))
    u = _dot(h_ref[...], wu_ref[...].astype(BF16))
    rows = g.shape[0]
    g_prev = pltpu.roll(g, 1, axis=0)[HALO:HALO + tm]
    g_next = pltpu.roll(g, rows - 1, axis=0)[HALO:HALO + tm]
    cwb = cwb_ref[j]
    c = g_prev * cwb[0:1] + g[HALO:HALO + tm] * cwb[1:2] + g_next * cwb[2:3] + cwb[3:4]
    o_ref[...] = (c / (1.0 + jnp.exp(-c)) * u).astype(BF16)


def _ffn_up(h, w_gate, w_up, conv_w, conv_b, *, seq, tm, tf):
    t = h.shape[0]
    hb = tm // HALO
    n_halo = t // HALO
    cwb = jnp.concatenate([conv_w, conv_b], axis=0).reshape(4, D_FF // tf, tf).transpose(1, 0, 2)
    return pl.pallas_call(
        functools.partial(_ffn_up_kernel, seq=seq),
        grid=(t // tm, D_FF // tf),
        in_specs=[
            pl.BlockSpec((tm, D_MODEL), lambda i, j: (i, 0)),
            pl.BlockSpec((HALO, D_MODEL), lambda i, j: (jnp.maximum(i * hb - 1, 0), 0)),
            pl.BlockSpec((HALO, D_MODEL), lambda i, j: (jnp.minimum((i + 1) * hb, n_halo - 1), 0)),
            pl.BlockSpec((D_MODEL, tf), lambda i, j: (0, j)),
            pl.BlockSpec((D_MODEL, tf), lambda i, j: (0, j)),
            _const_spec(cwb.shape),
        ],
        out_specs=pl.BlockSpec((tm, tf), lambda i, j: (i, j)),
        out_shape=jax.ShapeDtypeStruct((t, D_FF), BF16),
        scratch_shapes=[pltpu.VMEM((tm + 2 * HALO, D_MODEL), BF16)],
        compiler_params=_params("parallel", "arbitrary"),
        name="ffn_up",
    )(h, h, h, w_gate, w_up, cwb)


def _ffn_down_kernel(a_ref, w_ref, x_ref, g_ref, o_ref):
    o_ref[...] = _rms(x_ref[...] + _dot(a_ref[...], w_ref[...]), g_ref[...])


def _ffn_down(act, w_down, x, g_final, *, tm):
    t = x.shape[0]
    return pl.pallas_call(
        _ffn_down_kernel,
        grid=(t // tm,),
        in_specs=[
            pl.BlockSpec((tm, D_FF), lambda i: (i, 0)),
            _const_spec(w_down.shape),
            pl.BlockSpec((tm, D_MODEL), lambda i: (i, 0)),
            _const_spec(g_final.shape),
        ],
        out_specs=pl.BlockSpec((tm, D_MODEL), lambda i: (i, 0)),
        out_shape=jax.ShapeDtypeStruct((t, D_MODEL), F32),
        compiler_params=_params("parallel"),
        name="ffn_down",
    )(act, w_down, x, g_final)


def _dft_tables(seq):
    n1 = DFT_N1
    n2 = seq // n1
    c = np.arange(FOURIER_GROUP_DIM)
    ang = 2.0 * np.pi * ((c[:, None] * c[None, :]) % FOURIER_GROUP_DIM) / FOURIER_GROUP_DIM
    dft_c = np.concatenate([np.cos(ang), np.sin(ang)], axis=1)

    j = np.arange(n1)
    a1 = 2.0 * np.pi * ((j[:, None] * j[None, :]) % n1) / n1
    c1, s1 = np.cos(a1), np.sin(a1)
    m1 = np.block([[c1, -s1], [-s1, -c1]])

    k1 = np.arange(n1)[:, None, None]
    k2 = np.arange(n2)[None, :, None]
    m2 = np.arange(n2)[None, None, :]
    a2 = 2.0 * np.pi * ((m2 * (k1 + n1 * k2)) % seq) / seq
    e = np.concatenate([np.cos(a2), np.sin(a2)], axis=2)
    return (jnp.asarray(dft_c, BF16), jnp.asarray(m1, BF16), jnp.asarray(e, BF16))


def _rope_tables(seq):
    inv = ROPE_THETA ** (-jnp.arange(0, QK_ROPE_DIM, 2, dtype=F32) / QK_ROPE_DIM)
    ang = jnp.arange(seq, dtype=F32)[:, None] * inv[None, :]
    cos, sin = jnp.cos(ang), jnp.sin(ang)
    reps = ROPE_TABLE_DIM // QK_ROPE_DIM
    return (jnp.concatenate([cos, cos] * reps, axis=1), jnp.concatenate([-sin, sin] * reps, axis=1),
            jnp.concatenate([cos.T, cos.T], axis=0), jnp.concatenate([-sin.T, sin.T], axis=0))


def _prep_weights(w_in, w_uq, w_ukv):
    half = QK_ROPE_DIM // 2
    c1 = FOURIER_DIM + Q_LORA_RANK + KV_LORA_RANK
    w_in_ext = jnp.concatenate(
        [w_in, w_in[:, c1 + half:c1 + QK_ROPE_DIM], w_in[:, c1:c1 + half]], axis=1).astype(BF16)
    q3 = w_uq.reshape(Q_LORA_RANK, N_HEADS, QK_DIM)
    w_uqt_ext = jnp.concatenate(
        [q3[:, :, :QK_NOPE_DIM].reshape(Q_LORA_RANK, -1), q3[:, :, QK_NOPE_DIM:].reshape(Q_LORA_RANK, -1)],
        axis=1).T.astype(BF16)
    kv3 = w_ukv.reshape(KV_LORA_RANK, N_HEADS, QK_NOPE_DIM + V_HEAD_DIM)
    w_k = kv3[:, :, :QK_NOPE_DIM].reshape(KV_LORA_RANK, -1).astype(BF16)
    w_vt = kv3[:, :, QK_NOPE_DIM:].reshape(KV_LORA_RANK, -1).T.astype(BF16)
    return w_in_ext, w_uqt_ext, w_k, w_vt


def _tile(n, pref):
    return pref if n % pref == 0 else n


def _tiles(s):
    attn_k = _tile(s, 512)
    return dict(
        in_proj=_tile(s, 512),
        attn_q=_tile(s, 1024), attn_tiles=4 if s % 4096 == 0 else 1, attn_streams=4 if s % 4096 == 0 else 1,
        attn_k=attn_k, attn_group=4 if (s // attn_k) % 4 == 0 else 2,
        fourier_tn2=min(16, s // DFT_N1), fourier_kb=32,
        out_proj=_tile(s, 512), mem_kv_cols=1024, cross=_tile(s, 512),
        ffn_up=_tile(s, 1024), ffn_cols=512, ffn_down=_tile(s, 256),
    )


def _trunk(x, mem, p, tables):
    b, s, _ = x.shape
    t = b * s
    n1 = DFT_N1
    n2 = s // n1
    ts = _tiles(s)
    dft_c, m1, e, cos2, sin2, cos_t, sin_t = tables

    ab, qt, k, vt = _in_proj(x, p["g_mix"], p["w_in_ext"], p["g_q"], p["w_uqt_ext"], p["g_kv"], p["w_k"],
                             p["w_vt"], dft_c, cos2, sin2, cos_t, sin_t, tm=ts["in_proj"], tk=ts["attn_k"])

    gc = _fourier_a(ab.reshape(b, N_FOURIER_GROUPS, 2, n1, n2, FOURIER_GROUP_DIM), m1, tn2=ts["fourier_tn2"])
    f = _fourier_b(gc.reshape(b, N_FOURIER_GROUPS, 2, n1, n2, FOURIER_GROUP_DIM), e, kb=ts["fourier_kb"],
                   scale=float((s * FOURIER_GROUP_DIM) ** -0.5))

    a = _attention(qt, k, vt, tq=ts["attn_q"], tiles_per_step=ts["attn_tiles"], n_streams=ts["attn_streams"],
                   group_size=ts["attn_group"])

    x1 = _out_proj(x.reshape(t, D_MODEL), f.reshape(t, FOURIER_DIM), a.reshape(t, MLA_DIM),
                   p["g_f"], p["g_a"], p["w_out"], tm=ts["out_proj"])

    kvm = _mem_kv(mem, p["g_mem"], p["w_ckv"], tn=ts["mem_kv_cols"])
    m_all, n_all = _cross_fold(kvm, p["w_cq"], p["w_co"])
    x2, hf = _cross(x1.reshape(b, s, D_MODEL), m_all, n_all, p["g_cross"], p["g_ffn"], tm=ts["cross"])

    act = _ffn_up(hf.reshape(t, D_MODEL), p["w_gate"], p["w_up"], p["conv_w"], p["conv_b"], seq=s,
                  tm=ts["ffn_up"], tf=ts["ffn_cols"])
    y = _ffn_down(act, p["w_down"], x2.reshape(t, D_MODEL), p["g_final"], tm=ts["ffn_down"])
    return y.reshape(b, s, D_MODEL)


@jax.jit
def kernel(x_prompt, x_sample, mem_prompt, mem_sample, norm_mix_g, w_in, q_norm_g, w_uq, kv_norm_g, w_ukv,
           fourier_out_g, mla_out_g, w_out, norm_cross_g, norm_mem_g, w_cq, w_ckv, w_co, norm_ffn_g, w_gate,
           w_up, conv_w, conv_b, w_down, final_norm_g):
    assert norm_mix_g.shape[0] == 1, "single-layer trunk"
    assert x_prompt.shape[1] % DFT_N1 == 0 and x_sample.shape[1] % DFT_N1 == 0
    w_in_ext, w_uqt_ext, w_k, w_vt = _prep_weights(w_in[0], w_uq[0], w_ukv[0])
    row = lambda g: g.reshape(1, -1)
    p = dict(
        g_mix=row(norm_mix_g[0]), w_in_ext=w_in_ext, g_q=row(q_norm_g[0]), w_uqt_ext=w_uqt_ext,
        g_kv=row(kv_norm_g[0]), w_k=w_k, w_vt=w_vt, g_f=row(fourier_out_g[0]), g_a=row(mla_out_g[0]),
        w_out=w_out[0].astype(BF16), g_cross=row(norm_cross_g[0]), g_mem=row(norm_mem_g[0]),
        w_cq=w_cq[0].astype(BF16), w_ckv=w_ckv.reshape(w_ckv.shape[1:]), w_co=w_co[0].astype(BF16),
        g_ffn=row(norm_ffn_g[0]), w_gate=w_gate.reshape(w_gate.shape[1:]), w_up=w_up.reshape(w_up.shape[1:]),
        conv_w=conv_w[0], conv_b=row(conv_b[0]), w_down=w_down[0].astype(BF16), g_final=row(final_norm_g),
    )
    outs = []
    for x, mem in ((x_prompt, mem_prompt), (x_sample, mem_sample)):
        s = x.shape[1]
        tables = _dft_tables(s) + _rope_tables(s)
        outs.append(_trunk(x, mem, p, tables))
    return tuple(outs)
```

```python
import functools

import numpy as np
import jax
import jax.numpy as jnp
from jax import lax
from jax.experimental import pallas as pl
from jax.experimental.pallas import tpu as pltpu

F32 = jnp.float32
BF16 = jnp.bfloat16

D_MODEL = 2048
N_FOURIER_GROUPS = 4
FOURIER_GROUP_DIM = 256
FOURIER_DIM = N_FOURIER_GROUPS * FOURIER_GROUP_DIM
N_HEADS = 8
QK_NOPE_DIM = 128
QK_ROPE_DIM = 64
QK_DIM = QK_NOPE_DIM + QK_ROPE_DIM
ROPE_TABLE_DIM = 2 * QK_ROPE_DIM
V_HEAD_DIM = 128
V_EXT_DIM = V_HEAD_DIM + 16
Q_LORA_RANK = 512
KV_LORA_RANK = 512
MLA_DIM = N_HEADS * V_HEAD_DIM
ROPE_THETA = 10000.0
N_CROSS_HEADS = 4
CROSS_HEAD_DIM = D_MODEL // N_CROSS_HEADS
D_FF = 5632
EPS = 1e-6
LOG2_E = 1.4426950408889634

V7X_VMEM_LIMIT_BYTES = 56 * 1024 * 1024
MXU_DIM = 256
DFT_N1 = MXU_DIM // 2
HALO = 16


def _params(*sem):
    return pltpu.CompilerParams(dimension_semantics=sem, vmem_limit_bytes=V7X_VMEM_LIMIT_BYTES)


def _const_spec(shape):
    zeros = (0,) * len(shape)
    return pl.BlockSpec(shape, lambda *_: zeros, pipeline_mode=pl.Buffered(1))


def _rms(x, g):
    return x * lax.rsqrt(jnp.mean(x * x, axis=-1, keepdims=True) + EPS) * g


def _dot(a, b):
    return jnp.dot(a, b, preferred_element_type=F32)


def _dot_nt(a, b):
    return lax.dot_general(a, b, (((1,), (1,)), ((), ())), preferred_element_type=F32)


def _in_proj_kernel(x_ref, g_ref, win_ref, qg_ref, wuqt_ref, kvg_ref, wk_ref, wvt_ref, dft_ref, cos_ref,
                    sin_ref, cost_ref, sint_ref, ab_ref, qt_ref, k_ref, vt_ref):
    h = _rms(x_ref[...], g_ref[...]).astype(BF16)
    z = _dot(h, win_ref[...])

    for g in range(N_FOURIER_GROUPS):
        u = z[:, g * FOURIER_GROUP_DIM:(g + 1) * FOURIER_GROUP_DIM].astype(BF16)
        ab = _dot(u, dft_ref[...])
        ab_ref[g, 0] = ab[:, :FOURIER_GROUP_DIM]
        ab_ref[g, 1] = ab[:, FOURIER_GROUP_DIM:]

    c0 = FOURIER_DIM
    c_q = z[:, c0:c0 + Q_LORA_RANK]
    c_kv = z[:, c0 + Q_LORA_RANK:c0 + Q_LORA_RANK + KV_LORA_RANK]
    c1 = c0 + Q_LORA_RANK + KV_LORA_RANK
    k_r = z[:, c1:c1 + QK_ROPE_DIM]
    k_r_sw = z[:, c1 + QK_ROPE_DIM:c1 + 2 * QK_ROPE_DIM]

    cos = cos_ref[...]
    sin = sin_ref[...]
    k_rope = (k_r * cos[:, :QK_ROPE_DIM] + k_r_sw * sin[:, :QK_ROPE_DIM]).astype(BF16)

    hq = _rms(c_q, qg_ref[...]).astype(BF16)
    qe_t = _dot_nt(wuqt_ref[...], hq)
    q_nope_w = N_HEADS * QK_NOPE_DIM
    cos_t = jnp.concatenate([cost_ref[...]] * N_HEADS, axis=0)
    sin_t = jnp.concatenate([sint_ref[...]] * N_HEADS, axis=0)
    q_r = qe_t[q_nope_w:]
    half = QK_ROPE_DIM // 2
    q_sw = jnp.concatenate(
        [q_r[lo:lo + half] for hd in range(N_HEADS) for lo in (hd * QK_ROPE_DIM + half, hd * QK_ROPE_DIM)], axis=0)
    q_rope_t = q_r * cos_t + q_sw * sin_t

    hkv = _rms(c_kv, kvg_ref[...]).astype(BF16)
    k_nope = _dot(hkv, wk_ref[...])
    v_t = _dot_nt(wvt_ref[...], hkv)

    scale = QK_DIM ** -0.5 * LOG2_E
    for hd in range(N_HEADS):
        qt_ref[hd, 0:QK_NOPE_DIM, :] = (qe_t[hd * QK_NOPE_DIM:(hd + 1) * QK_NOPE_DIM] * scale).astype(BF16)
        qt_ref[hd, QK_NOPE_DIM:QK_DIM, :] = (
            q_rope_t[hd * QK_ROPE_DIM:(hd + 1) * QK_ROPE_DIM] * scale).astype(BF16)
        k_ref[hd, :, 0:QK_NOPE_DIM] = k_nope[:, hd * QK_NOPE_DIM:(hd + 1) * QK_NOPE_DIM].astype(BF16)
        k_ref[hd, :, QK_NOPE_DIM:QK_DIM] = k_rope
        vt_ref[hd, 0:V_HEAD_DIM, :] = v_t[hd * V_HEAD_DIM:(hd + 1) * V_HEAD_DIM, :].astype(BF16)
        pad_row = lax.broadcasted_iota(jnp.int32, (V_EXT_DIM - V_HEAD_DIM, v_t.shape[1]), 0)
        vt_ref[hd, V_HEAD_DIM:, :] = jnp.where(pad_row == 0, 1.0, 0.0).astype(BF16)


def _in_proj(x, g_mix, w_in_ext, q_g, w_uqt_ext, kv_g, w_k, w_vt, dft_c, cos2, sin2, cos_t, sin_t, *, tm, tk):
    b, s, _ = x.shape
    grid = (b, s // tm)
    r = tk // tm
    return pl.pallas_call(
        _in_proj_kernel,
        grid=grid,
        in_specs=[
            pl.BlockSpec((None, tm, D_MODEL), lambda bi, i: (bi, i, 0)),
            _const_spec(g_mix.shape),
            _const_spec(w_in_ext.shape),
            _const_spec(q_g.shape),
            _const_spec(w_uqt_ext.shape),
            _const_spec(kv_g.shape),
            _const_spec(w_k.shape),
            _const_spec(w_vt.shape),
            _const_spec(dft_c.shape),
            pl.BlockSpec((tm, ROPE_TABLE_DIM), lambda bi, i: (i, 0)),
            pl.BlockSpec((tm, ROPE_TABLE_DIM), lambda bi, i: (i, 0)),
            pl.BlockSpec((QK_ROPE_DIM, tm), lambda bi, i: (0, i)),
            pl.BlockSpec((QK_ROPE_DIM, tm), lambda bi, i: (0, i)),
        ],
        out_specs=[
            pl.BlockSpec((None, N_FOURIER_GROUPS, 2, tm, FOURIER_GROUP_DIM), lambda bi, i: (bi, 0, 0, i, 0)),
            pl.BlockSpec((None, N_HEADS, QK_DIM, tm), lambda bi, i: (bi, 0, 0, i)),
            pl.BlockSpec((None, N_HEADS, tm, QK_DIM), lambda bi, i: (bi, 0, i, 0)),
            pl.BlockSpec((None, N_HEADS, None, V_EXT_DIM, tm), lambda bi, i: (bi, 0, i // r, 0, i % r)),
        ],
        out_shape=[
            jax.ShapeDtypeStruct((b, N_FOURIER_GROUPS, 2, s, FOURIER_GROUP_DIM), F32),
            jax.ShapeDtypeStruct((b, N_HEADS, QK_DIM, s), BF16),
            jax.ShapeDtypeStruct((b, N_HEADS, s, QK_DIM), BF16),
            jax.ShapeDtypeStruct((b, N_HEADS, s // tk, V_EXT_DIM, tk), BF16),
        ],
        compiler_params=_params("parallel", "parallel"),
        name="in_proj",
    )(x, g_mix, w_in_ext, q_g, w_uqt_ext, kv_g, w_k, w_vt, dft_c, cos2, sin2, cos_t, sin_t)


def _fourier_a_kernel(m_ref, x_ref, o_ref):
    two, n1, tn2, c = x_ref.shape
    for j in range(tn2):
        x = jnp.concatenate([x_ref[0, :, j, :], x_ref[1, :, j, :]], axis=0).astype(BF16)
        o_ref[:, :, j * c:(j + 1) * c] = _dot(m_ref[...], x).astype(BF16).reshape(two, n1, c)


def _fourier_a(ab, m1, *, tn2):
    b, g, two, n1, n2, c = ab.shape
    return pl.pallas_call(
        _fourier_a_kernel,
        grid=(b, g, n2 // tn2),
        in_specs=[
            _const_spec(m1.shape),
            pl.BlockSpec((None, None, two, n1, tn2, c), lambda bi, gi, ci: (bi, gi, 0, 0, ci, 0)),
        ],
        out_specs=pl.BlockSpec((None, None, two, n1, tn2 * c), lambda bi, gi, ci: (bi, gi, 0, 0, ci)),
        out_shape=jax.ShapeDtypeStruct((b, g, two, n1, n2 * c), BF16),
        compiler_params=_params("parallel", "parallel", "parallel"),
        name="fourier_a",
    )(m1, ab)


def _fourier_b_kernel(e_ref, g_ref, o_ref, *, scale):
    kb = e_ref.shape[0]
    for j in range(kb):
        x = jnp.concatenate([g_ref[0, j], g_ref[1, j]], axis=0)
        o_ref[:, j, :] = _dot(e_ref[j], x) * scale


def _fourier_b(gc, e, *, kb, scale):
    b, g, two, n1, n2, c = gc.shape
    return pl.pallas_call(
        functools.partial(_fourier_b_kernel, scale=scale),
        grid=(b, g, n1 // kb),
        in_specs=[
            pl.BlockSpec((kb, n2, 2 * n2), lambda bi, gi, ki: (ki, 0, 0)),
            pl.BlockSpec((None, None, two, kb, n2, c), lambda bi, gi, ki: (bi, gi, 0, ki, 0, 0)),
        ],
        out_specs=pl.BlockSpec((None, n2, kb, c), lambda bi, gi, ki: (bi, 0, ki, gi)),
        out_shape=jax.ShapeDtypeStruct((b, n2, n1, g * c), F32),
        compiler_params=_params("parallel", "parallel", "parallel"),
        name="fourier_b",
    )(e, gc)


def _attn_kernel(qt_ref, k_ref, vt_ref, o_ref, *scratch, group_size, n_streams):
    s_bufs, accs = scratch[:2 * n_streams], scratch[2 * n_streams:]
    tk, tq = s_bufs[0].shape
    nk = vt_ref.shape[0]
    n_tiles = qt_ref.shape[1] // tq

    def scores(tile, j, s_ref):
        start = pl.multiple_of(j * tk, tk)
        qt = qt_ref[:, tile * tq:(tile + 1) * tq]
        s = _dot(k_ref[pl.ds(start, tk), :], qt)
        s_ref[...] = s
        return jnp.max(s, axis=0, keepdims=True)

    def update(acc_ref, j, s_ref, cmax, m):
        m_new = jnp.maximum(m, cmax)
        p = jnp.exp2(s_ref[...] - m_new)
        acc_ref[...] = jnp.exp2(m - m_new) * acc_ref[...] + _dot(vt_ref[j], p.astype(BF16))
        return m_new

    def group(first_tile, j, carry, last):
        carry = list(carry)
        for t in range(group_size):
            for st in range(n_streams):
                tile = first_tile + st
                bufs = s_bufs[2 * st:2 * st + 2]
                cmax, m = carry[st]
                if not last or t + 1 < group_size:
                    nxt = scores(tile, j + t + 1, bufs[(t + 1) % 2])
                elif tile + n_streams < n_tiles:
                    nxt = scores(tile + n_streams, 0, bufs[0])
                else:
                    nxt = cmax
                carry[st] = (nxt, update(accs[st], j + t, bufs[t % 2], cmax, m))
        return tuple(carry)

    for acc_ref in accs:
        acc_ref[...] = jnp.zeros_like(acc_ref)
    cmaxes = [scores(st, 0, s_bufs[2 * st]) for st in range(n_streams)]
    for first_tile in range(0, n_tiles, n_streams):
        carry = tuple((c, jnp.full((1, tq), -jnp.inf, F32)) for c in cmaxes)
        carry = lax.fori_loop(0, nk // group_size - 1,
                              lambda i, c, ft=first_tile: group(ft, group_size * i, c, False), carry)
        carry = group(first_tile, nk - group_size, carry, True)
        cmaxes = [c for c, _ in carry]
        for st in range(n_streams):
            acc = accs[st][...]
            lo = (first_tile + st) * tq
            o_ref[lo:lo + tq, :] = (acc[0:V_HEAD_DIM] / acc[V_HEAD_DIM:V_HEAD_DIM + 1]).T


def _attention(qt, k, vt, *, tq, tiles_per_step, n_streams, group_size):
    b, h, s, _ = k.shape
    nk, _, tk = vt.shape[2:]
    assert group_size % 2 == 0 and nk % group_size == 0 and tiles_per_step % n_streams == 0
    rows = tq * tiles_per_step
    return pl.pallas_call(
        functools.partial(_attn_kernel, group_size=group_size, n_streams=n_streams),
        grid=(b, h, s // rows),
        in_specs=[
            pl.BlockSpec((None, None, QK_DIM, rows), lambda bi, hi, i: (bi, hi, 0, i)),
            pl.BlockSpec((None, None, s, QK_DIM), lambda bi, hi, i: (bi, hi, 0, 0)),
            pl.BlockSpec((None, None, nk, V_EXT_DIM, tk), lambda bi, hi, i: (bi, hi, 0, 0, 0)),
        ],
        out_specs=pl.BlockSpec((None, rows, V_HEAD_DIM), lambda bi, hi, i: (bi, i, hi)),
        out_shape=jax.ShapeDtypeStruct((b, s, h * V_HEAD_DIM), F32),
        scratch_shapes=([pltpu.VMEM((tk, tq), F32)] * (2 * n_streams)
                        + [pltpu.VMEM((V_EXT_DIM, tq), F32)] * n_streams),
        compiler_params=_params("parallel", "parallel", "parallel"),
        name="attention",
    )(qt, k, vt)


def _out_proj_kernel(x_ref, f_ref, a_ref, gf_ref, ga_ref, w_ref, o_ref):
    hf = _rms(f_ref[...], gf_ref[...]).astype(BF16)
    ha = _rms(a_ref[...], ga_ref[...]).astype(BF16)
    o_ref[...] = x_ref[...] + _dot(hf, w_ref[0:FOURIER_DIM, :]) + _dot(ha, w_ref[FOURIER_DIM:, :])


def _out_proj(x, f, a, g_f, g_a, w_out, *, tm):
    t = x.shape[0]
    return pl.pallas_call(
        _out_proj_kernel,
        grid=(t // tm,),
        in_specs=[
            pl.BlockSpec((tm, D_MODEL), lambda i: (i, 0)),
            pl.BlockSpec((tm, FOURIER_DIM), lambda i: (i, 0)),
            pl.BlockSpec((tm, MLA_DIM), lambda i: (i, 0)),
            _const_spec(g_f.shape),
            _const_spec(g_a.shape),
            _const_spec(w_out.shape),
        ],
        out_specs=pl.BlockSpec((tm, D_MODEL), lambda i: (i, 0)),
        out_shape=jax.ShapeDtypeStruct((t, D_MODEL), F32),
        compiler_params=_params("parallel"),
        name="out_proj",
    )(x, f, a, g_f, g_a, w_out)


def _mem_kv_kernel(m_ref, g_ref, w_ref, o_ref, mn_ref):
    @pl.when(pl.program_id(0) == 0)
    def _():
        mn_ref[...] = _rms(m_ref[...], g_ref[...]).astype(BF16)

    o_ref[...] = _dot(mn_ref[...], w_ref[...].astype(BF16)).astype(BF16)


def _mem_kv(mem, g_mem, w_ckv, *, tn):
    b, m, _ = mem.shape
    n = w_ckv.shape[1]
    rows = b * m
    kv = pl.pallas_call(
        _mem_kv_kernel,
        grid=(n // tn,),
        in_specs=[
            _const_spec((rows, D_MODEL)),
            _const_spec(g_mem.shape),
            pl.BlockSpec((D_MODEL, tn), lambda j: (0, j)),
        ],
        out_specs=pl.BlockSpec((rows, tn), lambda j: (0, j)),
        out_shape=jax.ShapeDtypeStruct((rows, n), BF16),
        scratch_shapes=[pltpu.VMEM((rows, D_MODEL), BF16)],
        compiler_params=_params("arbitrary"),
        name="mem_kv",
    )(mem.reshape(rows, D_MODEL), g_mem, w_ckv)
    return kv.reshape(b, m, n)


def _cross_fold_kernel(k_ref, v_ref, wq_ref, wo_ref, m_ref, n_ref):
    m_ref[...] = (_dot_nt(wq_ref[...], k_ref[...]) * (CROSS_HEAD_DIM ** -0.5)).astype(BF16)
    n_ref[...] = _dot(v_ref[...], wo_ref[...]).astype(BF16)


def _cross_fold(kvm, w_cq, w_co):
    b, m, _ = kvm.shape
    return pl.pallas_call(
        _cross_fold_kernel,
        grid=(N_CROSS_HEADS, b),
        in_specs=[
            pl.BlockSpec((None, m, CROSS_HEAD_DIM), lambda hd, bi: (bi, 0, hd)),
            pl.BlockSpec((None, m, CROSS_HEAD_DIM), lambda hd, bi: (bi, 0, N_CROSS_HEADS + hd)),
            pl.BlockSpec((D_MODEL, CROSS_HEAD_DIM), lambda hd, bi: (0, hd)),
            pl.BlockSpec((CROSS_HEAD_DIM, D_MODEL), lambda hd, bi: (hd, 0)),
        ],
        out_specs=[
            pl.BlockSpec((None, D_MODEL, m), lambda hd, bi: (bi, 0, hd)),
            pl.BlockSpec((None, m, D_MODEL), lambda hd, bi: (bi, hd, 0)),
        ],
        out_shape=[jax.ShapeDtypeStruct((b, D_MODEL, N_CROSS_HEADS * m), BF16),
                   jax.ShapeDtypeStruct((b, N_CROSS_HEADS * m, D_MODEL), BF16)],
        compiler_params=_params("arbitrary", "arbitrary"),
        name="cross_fold",
    )(kvm, kvm, w_cq, w_co)


def _cross_kernel(x_ref, m_ref, n_ref, gc_ref, gn_ref, o_ref, h_ref):
    x = x_ref[...]
    h = _rms(x, gc_ref[...]).astype(BF16)
    s = _dot(h, m_ref[...])
    n_mem = m_ref.shape[1] // N_CROSS_HEADS
    probs = []
    for hd in range(N_CROSS_HEADS):
        sh = s[:, hd * n_mem:(hd + 1) * n_mem]
        p = jnp.exp(sh - jnp.max(sh, axis=-1, keepdims=True))
        probs.append((p / jnp.sum(p, axis=-1, keepdims=True)).astype(BF16))
    y = x + _dot(jnp.concatenate(probs, axis=1), n_ref[...])
    o_ref[...] = y
    h_ref[...] = _rms(y, gn_ref[...]).astype(BF16)


def _cross(x, m_all, n_all, g_cross, g_ffn, *, tm):
    b, s, _ = x.shape
    spec = pl.BlockSpec((None, tm, D_MODEL), lambda bi, i: (bi, i, 0))
    return pl.pallas_call(
        _cross_kernel,
        grid=(b, s // tm),
        in_specs=[
            spec,
            pl.BlockSpec((None,) + m_all.shape[1:], lambda bi, i: (bi, 0, 0)),
            pl.BlockSpec((None,) + n_all.shape[1:], lambda bi, i: (bi, 0, 0)),
            _const_spec(g_cross.shape),
            _const_spec(g_ffn.shape),
        ],
        out_specs=[spec, spec],
        out_shape=[jax.ShapeDtypeStruct((b, s, D_MODEL), F32), jax.ShapeDtypeStruct((b, s, D_MODEL), BF16)],
        compiler_params=_params("parallel", "parallel"),
        name="cross",
    )(x, m_all, n_all, g_cross, g_ffn)


def _ffn_up_kernel(h_ref, prev_ref, next_ref, wg_ref, wu_ref, cwb_ref, o_ref, hext_ref, *, seq):
    i = pl.program_id(0)
    j = pl.program_id(1)
    tm = h_ref.shape[0]

    @pl.when(j == 0)
    def _():
        first = (i * tm) % seq == 0
        last = ((i + 1) * tm) % seq == 0
        prev = prev_ref[...]
        nxt = next_ref[...]
        hext_ref[0:HALO, :] = jnp.where(first, jnp.zeros_like(prev), prev)
        hext_ref[HALO:HALO + tm, :] = h_ref[...]
        hext_ref[HALO + tm:, :] = jnp.where(last, jnp.zeros_like(nxt), nxt)

    g = _dot(hext_ref[...], wg_ref[...].astype(BF16))
    u = _dot(h_ref[...], wu_ref[...].astype(BF16))
    rows = g.shape[0]
    g_prev = pltpu.roll(g, 1, axis=0)[HALO:HALO + tm]
    g_next = pltpu.roll(g, rows - 1, axis=0)[HALO:HALO + tm]
    cwb = cwb_ref[j]
    c = g_prev * cwb[0:1] + g[HALO:HALO + tm] * cwb[1:2] + g_next * cwb[2:3] + cwb[3:4]
    o_ref[...] = (c / (1.0 + jnp.exp(-c)) * u).astype(BF16)


def _ffn_up(h, w_gate, w_up, conv_w, conv_b, *, seq, tm, tf):
    t = h.shape[0]
    hb = tm // HALO
    n_halo = t // HALO
    cwb = jnp.concatenate([conv_w, conv_b], axis=0).reshape(4, D_FF // tf, tf).transpose(1, 0, 2)
    return pl.pallas_call(
        functools.partial(_ffn_up_kernel, seq=seq),
        grid=(t // tm, D_FF // tf),
        in_specs=[
            pl.BlockSpec((tm, D_MODEL), lambda i, j: (i, 0)),
            pl.BlockSpec((HALO, D_MODEL), lambda i, j: (jnp.maximum(i * hb - 1, 0), 0)),
            pl.BlockSpec((HALO, D_MODEL), lambda i, j: (jnp.minimum((i + 1) * hb, n_halo - 1), 0)),
            pl.BlockSpec((D_MODEL, tf), lambda i, j: (0, j)),
            pl.BlockSpec((D_MODEL, tf), lambda i, j: (0, j)),
            _const_spec(cwb.shape),
        ],
        out_specs=pl.BlockSpec((tm, tf), lambda i, j: (i, j)),
        out_shape=jax.ShapeDtypeStruct((t, D_FF), BF16),
        scratch_shapes=[pltpu.VMEM((tm + 2 * HALO, D_MODEL), BF16)],
        compiler_params=_params("parallel", "arbitrary"),
        name="ffn_up",
    )(h, h, h, w_gate, w_up, cwb)


def _ffn_down_kernel(a_ref, w_ref, x_ref, g_ref, o_ref):
    o_ref[...] = _rms(x_ref[...] + _dot(a_ref[...], w_ref[...]), g_ref[...])


def _ffn_down(act, w_down, x, g_final, *, tm):
    t = x.shape[0]
    return pl.pallas_call(
        _ffn_down_kernel,
        grid=(t // tm,),
        in_specs=[
            pl.BlockSpec((tm, D_FF), lambda i: (i, 0)),
            _const_spec(w_down.shape),
            pl.BlockSpec((tm, D_MODEL), lambda i: (i, 0)),
            _const_spec(g_final.shape),
        ],
        out_specs=pl.BlockSpec((tm, D_MODEL), lambda i: (i, 0)),
        out_shape=jax.ShapeDtypeStruct((t, D_MODEL), F32),
        compiler_params=_params("parallel"),
        name="ffn_down",
    )(act, w_down, x, g_final)


def _dft_tables(seq):
    n1 = DFT_N1
    n2 = seq // n1
    c = np.arange(FOURIER_GROUP_DIM)
    ang = 2.0 * np.pi * ((c[:, None] * c[None, :]) % FOURIER_GROUP_DIM) / FOURIER_GROUP_DIM
    dft_c = np.concatenate([np.cos(ang), np.sin(ang)], axis=1)

    j = np.arange(n1)
    a1 = 2.0 * np.pi * ((j[:, None] * j[None, :]) % n1) / n1
    c1, s1 = np.cos(a1), np.sin(a1)
    m1 = np.block([[c1, -s1], [-s1, -c1]])

    k1 = np.arange(n1)[:, None, None]
    k2 = np.arange(n2)[None, :, None]
    m2 = np.arange(n2)[None, None, :]
    a2 = 2.0 * np.pi * ((m2 * (k1 + n1 * k2)) % seq) / seq
    e = np.concatenate([np.cos(a2), np.sin(a2)], axis=2)
    return (jnp.asarray(dft_c, BF16), jnp.asarray(m1, BF16), jnp.asarray(e, BF16))


def _rope_tables(seq):
    inv = ROPE_THETA ** (-jnp.arange(0, QK_ROPE_DIM, 2, dtype=F32) / QK_ROPE_DIM)
    ang = jnp.arange(seq, dtype=F32)[:, None] * inv[None, :]
    cos, sin = jnp.cos(ang), jnp.sin(ang)
    reps = ROPE_TABLE_DIM // QK_ROPE_DIM
    return (jnp.concatenate([cos, cos] * reps, axis=1), jnp.concatenate([-sin, sin] * reps, axis=1),
            jnp.concatenate([cos.T, cos.T], axis=0), jnp.concatenate([-sin.T, sin.T], axis=0))


def _prep_weights(w_in, w_uq, w_ukv):
    half = QK_ROPE_DIM // 2
    c1 = FOURIER_DIM + Q_LORA_RANK + KV_LORA_RANK
    w_in_ext = jnp.concatenate(
        [w_in, w_in[:, c1 + half:c1 + QK_ROPE_DIM], w_in[:, c1:c1 + half]], axis=1).astype(BF16)
    q3 = w_uq.reshape(Q_LORA_RANK, N_HEADS, QK_DIM)
    w_uqt_ext = jnp.concatenate(
        [q3[:, :, :QK_NOPE_DIM].reshape(Q_LORA_RANK, -1), q3[:, :, QK_NOPE_DIM:].reshape(Q_LORA_RANK, -1)],
        axis=1).T.astype(BF16)
    kv3 = w_ukv.reshape(KV_LORA_RANK, N_HEADS, QK_NOPE_DIM + V_HEAD_DIM)
    w_k = kv3[:, :, :QK_NOPE_DIM].reshape(KV_LORA_RANK, -1).astype(BF16)
    w_vt = kv3[:, :, QK_NOPE_DIM:].reshape(KV_LORA_RANK, -1).T.astype(BF16)
    return w_in_ext, w_uqt_ext, w_k, w_vt


def _tile(n, pref):
    return pref if n % pref == 0 else n


def _tiles(s):
    attn_k = _tile(s, 512)
    return dict(
        in_proj=_tile(s, 512),
        attn_q=_tile(s, 512), attn_tiles=8 if s % 4096 == 0 else 1, attn_streams=8 if s % 4096 == 0 else 1,
        attn_k=attn_k, attn_group=4 if (s // attn_k) % 4 == 0 else 2,
        fourier_tn2=min(16, s // DFT_N1), fourier_kb=32,
        out_proj=_tile(s, 512), mem_kv_cols=1024, cross=_tile(s, 512),
        ffn_up=_tile(s, 1024), ffn_cols=512, ffn_down=_tile(s, 256),
    )


def _trunk(x, mem, p, tables):
    b, s, _ = x.shape
    t = b * s
    n1 = DFT_N1
    n2 = s // n1
    ts = _tiles(s)
    dft_c, m1, e, cos2, sin2, cos_t, sin_t = tables

    ab, qt, k, vt = _in_proj(x, p["g_mix"], p["w_in_ext"], p["g_q"], p["w_uqt_ext"], p["g_kv"], p["w_k"],
                             p["w_vt"], dft_c, cos2, sin2, cos_t, sin_t, tm=ts["in_proj"], tk=ts["attn_k"])

    gc = _fourier_a(ab.reshape(b, N_FOURIER_GROUPS, 2, n1, n2, FOURIER_GROUP_DIM), m1, tn2=ts["fourier_tn2"])
    f = _fourier_b(gc.reshape(b, N_FOURIER_GROUPS, 2, n1, n2, FOURIER_GROUP_DIM), e, kb=ts["fourier_kb"],
                   scale=float((s * FOURIER_GROUP_DIM) ** -0.5))

    a = _attention(qt, k, vt, tq=ts["attn_q"], tiles_per_step=ts["attn_tiles"], n_streams=ts["attn_streams"],
                   group_size=ts["attn_group"])

    x1 = _out_proj(x.reshape(t, D_MODEL), f.reshape(t, FOURIER_DIM), a.reshape(t, MLA_DIM),
                   p["g_f"], p["g_a"], p["w_out"], tm=ts["out_proj"])

    kvm = _mem_kv(mem, p["g_mem"], p["w_ckv"], tn=ts["mem_kv_cols"])
    m_all, n_all = _cross_fold(kvm, p["w_cq"], p["w_co"])
    x2, hf = _cross(x1.reshape(b, s, D_MODEL), m_all, n_all, p["g_cross"], p["g_ffn"], tm=ts["cross"])

    act = _ffn_up(hf.reshape(t, D_MODEL), p["w_gate"], p["w_up"], p["conv_w"], p["conv_b"], seq=s,
                  tm=ts["ffn_up"], tf=ts["ffn_cols"])
    y = _ffn_down(act, p["w_down"], x2.reshape(t, D_MODEL), p["g_final"], tm=ts["ffn_down"])
    return y.reshape(b, s, D_MODEL)


@jax.jit
def kernel(x_prompt, x_sample, mem_prompt, mem_sample, norm_mix_g, w_in, q_norm_g, w_uq, kv_norm_g, w_ukv,
           fourier_out_g, mla_out_g, w_out, norm_cross_g, norm_mem_g, w_cq, w_ckv, w_co, norm_ffn_g, w_gate,
           w_up, conv_w, conv_b, w_down, final_norm_g):
    assert norm_mix_g.shape[0] == 1, "single-layer trunk"
    assert x_prompt.shape[1] % DFT_N1 == 0 and x_sample.shape[1] % DFT_N1 == 0
    w_in_ext, w_uqt_ext, w_k, w_vt = _prep_weights(w_in[0], w_uq[0], w_ukv[0])
    row = lambda g: g.reshape(1, -1)
    p = dict(
        g_mix=row(norm_mix_g[0]), w_in_ext=w_in_ext, g_q=row(q_norm_g[0]), w_uqt_ext=w_uqt_ext,
        g_kv=row(kv_norm_g[0]), w_k=w_k, w_vt=w_vt, g_f=row(fourier_out_g[0]), g_a=row(mla_out_g[0]),
        w_out=w_out[0].astype(BF16), g_cross=row(norm_cross_g[0]), g_mem=row(norm_mem_g[0]),
        w_cq=w_cq[0].astype(BF16), w_ckv=w_ckv.reshape(w_ckv.shape[1:]), w_co=w_co[0].astype(BF16),
        g_ffn=row(norm_ffn_g[0]), w_gate=w_gate.reshape(w_gate.shape[1:]), w_up=w_up.reshape(w_up.shape[1:]),
        conv_w=conv_w[0], conv_b=row(conv_b[0]), w_down=w_down[0].astype(BF16), g_final=row(final_norm_g),
    )
    outs = []
    for x, mem in ((x_prompt, mem_prompt), (x_sample, mem_sample)):
        s = x.shape[1]
        tables = _dft_tables(s) + _rope_tables(s)
        outs.append(_trunk(x, mem, p, tables))
    return tuple(outs)
```

```python
import functools

import numpy as np
import jax
import jax.numpy as jnp
from jax import lax
from jax.experimental import pallas as pl
from jax.experimental.pallas import tpu as pltpu

F32 = jnp.float32
BF16 = jnp.bfloat16

D_MODEL = 2048
N_FOURIER_GROUPS = 4
FOURIER_GROUP_DIM = 256
FOURIER_DIM = N_FOURIER_GROUPS * FOURIER_GROUP_DIM
N_HEADS = 8
QK_NOPE_DIM = 128
QK_ROPE_DIM = 64
QK_DIM = QK_NOPE_DIM + QK_ROPE_DIM
ROPE_TABLE_DIM = 2 * QK_ROPE_DIM
V_HEAD_DIM = 128
V_EXT_DIM = V_HEAD_DIM + 16
Q_LORA_RANK = 512
KV_LORA_RANK = 512
MLA_DIM = N_HEADS * V_HEAD_DIM
ROPE_THETA = 10000.0
N_CROSS_HEADS = 4
CROSS_HEAD_DIM = D_MODEL // N_CROSS_HEADS
D_FF = 5632
EPS = 1e-6
LOG2_E = 1.4426950408889634

V7X_VMEM_LIMIT_BYTES = 56 * 1024 * 1024
MXU_DIM = 256
DFT_N1 = MXU_DIM // 2
HALO = 16


def _params(*sem):
    return pltpu.CompilerParams(dimension_semantics=sem, vmem_limit_bytes=V7X_VMEM_LIMIT_BYTES)


def _const_spec(shape):
    zeros = (0,) * len(shape)
    return pl.BlockSpec(shape, lambda *_: zeros, pipeline_mode=pl.Buffered(1))


def _rms(x, g):
    return x * lax.rsqrt(jnp.mean(x * x, axis=-1, keepdims=True) + EPS) * g


def _dot(a, b):
    return jnp.dot(a, b, preferred_element_type=F32)


def _dot_nt(a, b):
    return lax.dot_general(a, b, (((1,), (1,)), ((), ())), preferred_element_type=F32)


def _in_proj_kernel(x_ref, g_ref, win_ref, qg_ref, wuqt_ref, kvg_ref, wk_ref, wvt_ref, dft_ref, cos_ref,
                    sin_ref, cost_ref, sint_ref, ab_ref, qt_ref, k_ref, vt_ref):
    h = _rms(x_ref[...], g_ref[...]).astype(BF16)
    z = _dot(h, win_ref[...])

    for g in range(N_FOURIER_GROUPS):
        u = z[:, g * FOURIER_GROUP_DIM:(g + 1) * FOURIER_GROUP_DIM].astype(BF16)
        ab = _dot(u, dft_ref[...])
        ab_ref[g, 0] = ab[:, :FOURIER_GROUP_DIM]
        ab_ref[g, 1] = ab[:, FOURIER_GROUP_DIM:]

    c0 = FOURIER_DIM
    c_q = z[:, c0:c0 + Q_LORA_RANK]
    c_kv = z[:, c0 + Q_LORA_RANK:c0 + Q_LORA_RANK + KV_LORA_RANK]
    c1 = c0 + Q_LORA_RANK + KV_LORA_RANK
    k_r = z[:, c1:c1 + QK_ROPE_DIM]
    k_r_sw = z[:, c1 + QK_ROPE_DIM:c1 + 2 * QK_ROPE_DIM]

    cos = cos_ref[...]
    sin = sin_ref[...]
    k_rope = (k_r * cos[:, :QK_ROPE_DIM] + k_r_sw * sin[:, :QK_ROPE_DIM]).astype(BF16)

    hq = _rms(c_q, qg_ref[...]).astype(BF16)
    qe_t = _dot_nt(wuqt_ref[...], hq)
    q_nope_w = N_HEADS * QK_NOPE_DIM
    cos_t = jnp.concatenate([cost_ref[...]] * N_HEADS, axis=0)
    sin_t = jnp.concatenate([sint_ref[...]] * N_HEADS, axis=0)
    q_r = qe_t[q_nope_w:]
    half = QK_ROPE_DIM // 2
    q_sw = jnp.concatenate(
        [q_r[lo:lo + half] for hd in range(N_HEADS) for lo in (hd * QK_ROPE_DIM + half, hd * QK_ROPE_DIM)], axis=0)
    q_rope_t = q_r * cos_t + q_sw * sin_t

    hkv = _rms(c_kv, kvg_ref[...]).astype(BF16)
    k_nope = _dot(hkv, wk_ref[...])
    v_t = _dot_nt(wvt_ref[...], hkv)

    scale = QK_DIM ** -0.5 * LOG2_E
    for hd in range(N_HEADS):
        qt_ref[hd, 0:QK_NOPE_DIM, :] = (qe_t[hd * QK_NOPE_DIM:(hd + 1) * QK_NOPE_DIM] * scale).astype(BF16)
        qt_ref[hd, QK_NOPE_DIM:QK_DIM, :] = (
            q_rope_t[hd * QK_ROPE_DIM:(hd + 1) * QK_ROPE_DIM] * scale).astype(BF16)
        k_ref[hd, :, 0:QK_NOPE_DIM] = k_nope[:, hd * QK_NOPE_DIM:(hd + 1) * QK_NOPE_DIM].astype(BF16)
        k_ref[hd, :, QK_NOPE_DIM:QK_DIM] = k_rope
        vt_ref[hd, 0:V_HEAD_DIM, :] = v_t[hd * V_HEAD_DIM:(hd + 1) * V_HEAD_DIM, :].astype(BF16)
        pad_row = lax.broadcasted_iota(jnp.int32, (V_EXT_DIM - V_HEAD_DIM, v_t.shape[1]), 0)
        vt_ref[hd, V_HEAD_DIM:, :] = jnp.where(pad_row == 0, 1.0, 0.0).astype(BF16)


def _in_proj(x, g_mix, w_in_ext, q_g, w_uqt_ext, kv_g, w_k, w_vt, dft_c, cos2, sin2, cos_t, sin_t, *, tm, tk):
    b, s, _ = x.shape
    grid = (b, s // tm)
    r = tk // tm
    return pl.pallas_call(
        _in_proj_kernel,
        grid=grid,
        in_specs=[
            pl.BlockSpec((None, tm, D_MODEL), lambda bi, i: (bi, i, 0)),
            _const_spec(g_mix.shape),
            _const_spec(w_in_ext.shape),
            _const_spec(q_g.shape),
            _const_spec(w_uqt_ext.shape),
            _const_spec(kv_g.shape),
            _const_spec(w_k.shape),
            _const_spec(w_vt.shape),
            _const_spec(dft_c.shape),
            pl.BlockSpec((tm, ROPE_TABLE_DIM), lambda bi, i: (i, 0)),
            pl.BlockSpec((tm, ROPE_TABLE_DIM), lambda bi, i: (i, 0)),
            pl.BlockSpec((QK_ROPE_DIM, tm), lambda bi, i: (0, i)),
            pl.BlockSpec((QK_ROPE_DIM, tm), lambda bi, i: (0, i)),
        ],
        out_specs=[
            pl.BlockSpec((None, N_FOURIER_GROUPS, 2, tm, FOURIER_GROUP_DIM), lambda bi, i: (bi, 0, 0, i, 0)),
            pl.BlockSpec((None, N_HEADS, QK_DIM, tm), lambda bi, i: (bi, 0, 0, i)),
            pl.BlockSpec((None, N_HEADS, tm, QK_DIM), lambda bi, i: (bi, 0, i, 0)),
            pl.BlockSpec((None, N_HEADS, None, V_EXT_DIM, tm), lambda bi, i: (bi, 0, i // r, 0, i % r)),
        ],
        out_shape=[
            jax.ShapeDtypeStruct((b, N_FOURIER_GROUPS, 2, s, FOURIER_GROUP_DIM), F32),
            jax.ShapeDtypeStruct((b, N_HEADS, QK_DIM, s), BF16),
            jax.ShapeDtypeStruct((b, N_HEADS, s, QK_DIM), BF16),
            jax.ShapeDtypeStruct((b, N_HEADS, s // tk, V_EXT_DIM, tk), BF16),
        ],
        compiler_params=_params("parallel", "parallel"),
        name="in_proj",
    )(x, g_mix, w_in_ext, q_g, w_uqt_ext, kv_g, w_k, w_vt, dft_c, cos2, sin2, cos_t, sin_t)


def _fourier_a_kernel(m_ref, x_ref, o_ref):
    two, n1, tn2, c = x_ref.shape
    for j in range(tn2):
        x = jnp.concatenate([x_ref[0, :, j, :], x_ref[1, :, j, :]], axis=0).astype(BF16)
        o_ref[:, :, j * c:(j + 1) * c] = _dot(m_ref[...], x).astype(BF16).reshape(two, n1, c)


def _fourier_a(ab, m1, *, tn2):
    b, g, two, n1, n2, c = ab.shape
    return pl.pallas_call(
        _fourier_a_kernel,
        grid=(b, g, n2 // tn2),
        in_specs=[
            _const_spec(m1.shape),
            pl.BlockSpec((None, None, two, n1, tn2, c), lambda bi, gi, ci: (bi, gi, 0, 0, ci, 0)),
        ],
        out_specs=pl.BlockSpec((None, None, two, n1, tn2 * c), lambda bi, gi, ci: (bi, gi, 0, 0, ci)),
        out_shape=jax.ShapeDtypeStruct((b, g, two, n1, n2 * c), BF16),
        compiler_params=_params("parallel", "parallel", "parallel"),
        name="fourier_a",
    )(m1, ab)


def _fourier_b_kernel(e_ref, g_ref, o_ref, *, scale):
    kb = e_ref.shape[0]
    for j in range(kb):
        x = jnp.concatenate([g_ref[0, j], g_ref[1, j]], axis=0)
        o_ref[:, j, :] = _dot(e_ref[j], x) * scale


def _fourier_b(gc, e, *, kb, scale):
    b, g, two, n1, n2, c = gc.shape
    return pl.pallas_call(
        functools.partial(_fourier_b_kernel, scale=scale),
        grid=(b, g, n1 // kb),
        in_specs=[
            pl.BlockSpec((kb, n2, 2 * n2), lambda bi, gi, ki: (ki, 0, 0)),
            pl.BlockSpec((None, None, two, kb, n2, c), lambda bi, gi, ki: (bi, gi, 0, ki, 0, 0)),
        ],
        out_specs=pl.BlockSpec((None, n2, kb, c), lambda bi, gi, ki: (bi, 0, ki, gi)),
        out_shape=jax.ShapeDtypeStruct((b, n2, n1, g * c), F32),
        compiler_params=_params("parallel", "parallel", "parallel"),
        name="fourier_b",
    )(e, gc)


def _attn_kernel(qt_ref, k_ref, vt_ref, o_ref, *scratch, group_size, n_streams):
    s_bufs, accs = scratch[:2 * n_streams], scratch[2 * n_streams:]
    tk, tq = s_bufs[0].shape
    nk = vt_ref.shape[0]
    n_tiles = qt_ref.shape[1] // tq

    def scores(tile, j, s_ref):
        start = pl.multiple_of(j * tk, tk)
        qt = qt_ref[:, tile * tq:(tile + 1) * tq]
        s = _dot(k_ref[pl.ds(start, tk), :], qt)
        s_ref[...] = s
        return jnp.max(s, axis=0, keepdims=True)

    def update(acc_ref, j, s_ref, cmax, m):
        m_new = jnp.maximum(m, cmax)
        p = jnp.exp2(s_ref[...] - m_new)
        acc_ref[...] = jnp.exp2(m - m_new) * acc_ref[...] + _dot(vt_ref[j], p.astype(BF16))
        return m_new

    def group(first_tile, j, carry, last):
        carry = list(carry)
        for t in range(group_size):
            for st in range(n_streams):
                tile = first_tile + st
                bufs = s_bufs[2 * st:2 * st + 2]
                cmax, m = carry[st]
                if not last or t + 1 < group_size:
                    nxt = scores(tile, j + t + 1, bufs[(t + 1) % 2])
                elif tile + n_streams < n_tiles:
                    nxt = scores(tile + n_streams, 0, bufs[0])
                else:
                    nxt = cmax
                carry[st] = (nxt, update(accs[st], j + t, bufs[t % 2], cmax, m))
        return tuple(carry)

    for acc_ref in accs:
        acc_ref[...] = jnp.zeros_like(acc_ref)
    cmaxes = [scores(st, 0, s_bufs[2 * st]) for st in range(n_streams)]
    for first_tile in range(0, n_tiles, n_streams):
        carry = tuple((c, jnp.full((1, tq), -jnp.inf, F32)) for c in cmaxes)
        carry = lax.fori_loop(0, nk // group_size - 1,
                              lambda i, c, ft=first_tile: group(ft, group_size * i, c, False), carry)
        carry = group(first_tile, nk - group_size, carry, True)
        cmaxes = [c for c, _ in carry]
        for st in range(n_streams):
            acc = accs[st][...]
            lo = (first_tile + st) * tq
            o_ref[lo:lo + tq, :] = (acc[0:V_HEAD_DIM] / acc[V_HEAD_DIM:V_HEAD_DIM + 1]).T


def _attention(qt, k, vt, *, tq, tiles_per_step, n_streams, group_size):
    b, h, s, _ = k.shape
    nk, _, tk = vt.shape[2:]
    assert group_size % 2 == 0 and nk % group_size == 0 and tiles_per_step % n_streams == 0
    rows = tq * tiles_per_step
    return pl.pallas_call(
        functools.partial(_attn_kernel, group_size=group_size, n_streams=n_streams),
        grid=(b, h, s // rows),
        in_specs=[
            pl.BlockSpec((None, None, QK_DIM, rows), lambda bi, hi, i: (bi, hi, 0, i)),
            pl.BlockSpec((None, None, s, QK_DIM), lambda bi, hi, i: (bi, hi, 0, 0)),
            pl.BlockSpec((None, None, nk, V_EXT_DIM, tk), lambda bi, hi, i: (bi, hi, 0, 0, 0)),
        ],
        out_specs=pl.BlockSpec((None, rows, V_HEAD_DIM), lambda bi, hi, i: (bi, i, hi)),
        out_shape=jax.ShapeDtypeStruct((b, s, h * V_HEAD_DIM), F32),
        scratch_shapes=([pltpu.VMEM((tk, tq), F32)] * (2 * n_streams)
                        + [pltpu.VMEM((V_EXT_DIM, tq), F32)] * n_streams),
        compiler_params=_params("parallel", "parallel", "parallel"),
        name="attention",
    )(qt, k, vt)


def _out_proj_kernel(x_ref, f_ref, a_ref, gf_ref, ga_ref, w_ref, o_ref):
    hf = _rms(f_ref[...], gf_ref[...]).astype(BF16)
    ha = _rms(a_ref[...], ga_ref[...]).astype(BF16)
    o_ref[...] = x_ref[...] + _dot(hf, w_ref[0:FOURIER_DIM, :]) + _dot(ha, w_ref[FOURIER_DIM:, :])


def _out_proj(x, f, a, g_f, g_a, w_out, *, tm):
    t = x.shape[0]
    return pl.pallas_call(
        _out_proj_kernel,
        grid=(t // tm,),
        in_specs=[
            pl.BlockSpec((tm, D_MODEL), lambda i: (i, 0)),
            pl.BlockSpec((tm, FOURIER_DIM), lambda i: (i, 0)),
            pl.BlockSpec((tm, MLA_DIM), lambda i: (i, 0)),
            _const_spec(g_f.shape),
            _const_spec(g_a.shape),
            _const_spec(w_out.shape),
        ],
        out_specs=pl.BlockSpec((tm, D_MODEL), lambda i: (i, 0)),
        out_shape=jax.ShapeDtypeStruct((t, D_MODEL), F32),
        compiler_params=_params("parallel"),
        name="out_proj",
    )(x, f, a, g_f, g_a, w_out)


def _mem_kv_kernel(m_ref, g_ref, w_ref, o_ref, mn_ref):
    @pl.when(pl.program_id(0) == 0)
    def _():
        mn_ref[...] = _rms(m_ref[...], g_ref[...]).astype(BF16)

    o_ref[...] = _dot(mn_ref[...], w_ref[...].astype(BF16)).astype(BF16)


def _mem_kv(mem, g_mem, w_ckv, *, tn):
    b, m, _ = mem.shape
    n = w_ckv.shape[1]
    rows = b * m
    kv = pl.pallas_call(
        _mem_kv_kernel,
        grid=(n // tn,),
        in_specs=[
            _const_spec((rows, D_MODEL)),
            _const_spec(g_mem.shape),
            pl.BlockSpec((D_MODEL, tn), lambda j: (0, j)),
        ],
        out_specs=pl.BlockSpec((rows, tn), lambda j: (0, j)),
        out_shape=jax.ShapeDtypeStruct((rows, n), BF16),
        scratch_shapes=[pltpu.VMEM((rows, D_MODEL), BF16)],
        compiler_params=_params("arbitrary"),
        name="mem_kv",
    )(mem.reshape(rows, D_MODEL), g_mem, w_ckv)
    return kv.reshape(b, m, n)


def _cross_fold_kernel(k_ref, v_ref, wq_ref, wo_ref, m_ref, n_ref):
    m_ref[...] = (_dot_nt(wq_ref[...], k_ref[...]) * (CROSS_HEAD_DIM ** -0.5)).astype(BF16)
    n_ref[...] = _dot(v_ref[...], wo_ref[...]).astype(BF16)


def _cross_fold(kvm, w_cq, w_co):
    b, m, _ = kvm.shape
    return pl.pallas_call(
        _cross_fold_kernel,
        grid=(N_CROSS_HEADS, b),
        in_specs=[
            pl.BlockSpec((None, m, CROSS_HEAD_DIM), lambda hd, bi: (bi, 0, hd)),
            pl.BlockSpec((None, m, CROSS_HEAD_DIM), lambda hd, bi: (bi, 0, N_CROSS_HEADS + hd)),
            pl.BlockSpec((D_MODEL, CROSS_HEAD_DIM), lambda hd, bi: (0, hd)),
            pl.BlockSpec((CROSS_HEAD_DIM, D_MODEL), lambda hd, bi: (hd, 0)),
        ],
        out_specs=[
            pl.BlockSpec((None, D_MODEL, m), lambda hd, bi: (bi, 0, hd)),
            pl.BlockSpec((None, m, D_MODEL), lambda hd, bi: (bi, hd, 0)),
        ],
        out_shape=[jax.ShapeDtypeStruct((b, D_MODEL, N_CROSS_HEADS * m), BF16),
                   jax.ShapeDtypeStruct((b, N_CROSS_HEADS * m, D_MODEL), BF16)],
        compiler_params=_params("arbitrary", "arbitrary"),
        name="cross_fold",
    )(kvm, kvm, w_cq, w_co)


def _cross_kernel(x_ref, m_ref, n_ref, gc_ref, gn_ref, o_ref, h_ref):
    x = x_ref[...]
    h = _rms(x, gc_ref[...]).astype(BF16)
    s = _dot(h, m_ref[...])
    n_mem = m_ref.shape[1] // N_CROSS_HEADS
    probs = []
    for hd in range(N_CROSS_HEADS):
        sh = s[:, hd * n_mem:(hd + 1) * n_mem]
        p = jnp.exp(sh - jnp.max(sh, axis=-1, keepdims=True))
        probs.append((p / jnp.sum(p, axis=-1, keepdims=True)).astype(BF16))
    y = x + _dot(jnp.concatenate(probs, axis=1), n_ref[...])
    o_ref[...] = y
    h_ref[...] = _rms(y, gn_ref[...]).astype(BF16)


def _cross(x, m_all, n_all, g_cross, g_ffn, *, tm):
    b, s, _ = x.shape
    spec = pl.BlockSpec((None, tm, D_MODEL), lambda bi, i: (bi, i, 0))
    return pl.pallas_call(
        _cross_kernel,
        grid=(b, s // tm),
        in_specs=[
            spec,
            pl.BlockSpec((None,) + m_all.shape[1:], lambda bi, i: (bi, 0, 0)),
            pl.BlockSpec((None,) + n_all.shape[1:], lambda bi, i: (bi, 0, 0)),
            _const_spec(g_cross.shape),
            _const_spec(g_ffn.shape),
        ],
        out_specs=[spec, spec],
        out_shape=[jax.ShapeDtypeStruct((b, s, D_MODEL), F32), jax.ShapeDtypeStruct((b, s, D_MODEL), BF16)],
        compiler_params=_params("parallel", "parallel"),
        name="cross",
    )(x, m_all, n_all, g_cross, g_ffn)


def _ffn_up_kernel(h_ref, prev_ref, next_ref, wg_ref, wu_ref, cwb_ref, o_ref, hext_ref, *, seq):
    i = pl.program_id(0)
    j = pl.program_id(1)
    tm = h_ref.shape[0]

    @pl.when(j == 0)
    def _():
        first = (i * tm) % seq == 0
        last = ((i + 1) * tm) % seq == 0
        prev = prev_ref[...]
        nxt = next_ref[...]
        hext_ref[0:HALO, :] = jnp.where(first, jnp.zeros_like(prev), prev)
        hext_ref[HALO:HALO + tm, :] = h_ref[...]
        hext_ref[HALO + tm:, :] = jnp.where(last, jnp.zeros_like(nxt), nxt)

    g = _dot(hext_ref[...], wg_ref[...].astype(BF16))
    u = _dot(h_ref[...], wu_ref[...].astype(BF16))
    rows = g.shape[0]
    g_prev = pltpu.roll(g, 1, axis=0)[HALO:HALO + tm]
    g_next = pltpu.roll(g, rows - 1, axis=0)[HALO:HALO + tm]
    cwb = cwb_ref[j]
    c = g_prev * cwb[0:1] + g[HALO:HALO + tm] * cwb[1:2] + g_next * cwb[2:3] + cwb[3:4]
    o_ref[...] = (c / (1.0 + jnp.exp(-c)) * u).astype(BF16)


def _ffn_up(h, w_gate, w_up, conv_w, conv_b, *, seq, tm, tf):
    t = h.shape[0]
    hb = tm // HALO
    n_halo = t // HALO
    cwb = jnp.concatenate([conv_w, conv_b], axis=0).reshape(4, D_FF // tf, tf).transpose(1, 0, 2)
    return pl.pallas_call(
        functools.partial(_ffn_up_kernel, seq=seq),
        grid=(t // tm, D_FF // tf),
        in_specs=[
            pl.BlockSpec((tm, D_MODEL), lambda i, j: (i, 0)),
            pl.BlockSpec((HALO, D_MODEL), lambda i, j: (jnp.maximum(i * hb - 1, 0), 0)),
            pl.BlockSpec((HALO, D_MODEL), lambda i, j: (jnp.minimum((i + 1) * hb, n_halo - 1), 0)),
            pl.BlockSpec((D_MODEL, tf), lambda i, j: (0, j)),
            pl.BlockSpec((D_MODEL, tf), lambda i, j: (0, j)),
            _const_spec(cwb.shape),
        ],
        out_specs=pl.BlockSpec((tm, tf), lambda i, j: (i, j)),
        out_shape=jax.ShapeDtypeStruct((t, D_FF), BF16),
        scratch_shapes=[pltpu.VMEM((tm + 2 * HALO, D_MODEL), BF16)],
        compiler_params=_params("parallel", "arbitrary"),
        name="ffn_up",
    )(h, h, h, w_gate, w_up, cwb)


def _ffn_down_kernel(a_ref, w_ref, x_ref, g_ref, o_ref):
    o_ref[...] = _rms(x_ref[...] + _dot(a_ref[...], w_ref[...]), g_ref[...])


def _ffn_down(act, w_down, x, g_final, *, tm):
    t = x.shape[0]
    return pl.pallas_call(
        _ffn_down_kernel,
        grid=(t // tm,),
        in_specs=[
            pl.BlockSpec((tm, D_FF), lambda i: (i, 0)),
            _const_spec(w_down.shape),
            pl.BlockSpec((tm, D_MODEL), lambda i: (i, 0)),
            _const_spec(g_final.shape),
        ],
        out_specs=pl.BlockSpec((tm, D_MODEL), lambda i: (i, 0)),
        out_shape=jax.ShapeDtypeStruct((t, D_MODEL), F32),
        compiler_params=_params("parallel"),
        name="ffn_down",
    )(act, w_down, x, g_final)


def _dft_tables(seq):
    n1 = DFT_N1
    n2 = seq // n1
    c = np.arange(FOURIER_GROUP_DIM)
    ang = 2.0 * np.pi * ((c[:, None] * c[None, :]) % FOURIER_GROUP_DIM) / FOURIER_GROUP_DIM
    dft_c = np.concatenate([np.cos(ang), np.sin(ang)], axis=1)

    j = np.arange(n1)
    a1 = 2.0 * np.pi * ((j[:, None] * j[None, :]) % n1) / n1
    c1, s1 = np.cos(a1), np.sin(a1)
    m1 = np.block([[c1, -s1], [-s1, -c1]])

    k1 = np.arange(n1)[:, None, None]
    k2 = np.arange(n2)[None, :, None]
    m2 = np.arange(n2)[None, None, :]
    a2 = 2.0 * np.pi * ((m2 * (k1 + n1 * k2)) % seq) / seq
    e = np.concatenate([np.cos(a2), np.sin(a2)], axis=2)
    return (jnp.asarray(dft_c, BF16), jnp.asarray(m1, BF16), jnp.asarray(e, BF16))


def _rope_tables(seq):
    inv = ROPE_THETA ** (-jnp.arange(0, QK_ROPE_DIM, 2, dtype=F32) / QK_ROPE_DIM)
    ang = jnp.arange(seq, dtype=F32)[:, None] * inv[None, :]
    cos, sin = jnp.cos(ang), jnp.sin(ang)
    reps = ROPE_TABLE_DIM // QK_ROPE_DIM
    return (jnp.concatenate([cos, cos] * reps, axis=1), jnp.concatenate([-sin, sin] * reps, axis=1),
            jnp.concatenate([cos.T, cos.T], axis=0), jnp.concatenate([-sin.T, sin.T], axis=0))


def _prep_weights(w_in, w_uq, w_ukv):
    half = QK_ROPE_DIM // 2
    c1 = FOURIER_DIM + Q_LORA_RANK + KV_LORA_RANK
    w_in_ext = jnp.concatenate(
        [w_in, w_in[:, c1 + half:c1 + QK_ROPE_DIM], w_in[:, c1:c1 + half]], axis=1).astype(BF16)
    q3 = w_uq.reshape(Q_LORA_RANK, N_HEADS, QK_DIM)
    w_uqt_ext = jnp.concatenate(
        [q3[:, :, :QK_NOPE_DIM].reshape(Q_LORA_RANK, -1), q3[:, :, QK_NOPE_DIM:].reshape(Q_LORA_RANK, -1)],
        axis=1).T.astype(BF16)
    kv3 = w_ukv.reshape(KV_LORA_RANK, N_HEADS, QK_NOPE_DIM + V_HEAD_DIM)
    w_k = kv3[:, :, :QK_NOPE_DIM].reshape(KV_LORA_RANK, -1).astype(BF16)
    w_vt = kv3[:, :, QK_NOPE_DIM:].reshape(KV_LORA_RANK, -1).T.astype(BF16)
    return w_in_ext, w_uqt_ext, w_k, w_vt


def _tile(n, pref):
    return pref if n % pref == 0 else n


def _tiles(s):
    attn_k = _tile(s, 512)
    return dict(
        in_proj=_tile(s, 512),
        attn_q=_tile(s, 256), attn_tiles=16 if s % 4096 == 0 else 1, attn_streams=16 if s % 4096 == 0 else 1,
        attn_k=attn_k, attn_group=4 if (s // attn_k) % 4 == 0 else 2,
        fourier_tn2=min(16, s // DFT_N1), fourier_kb=32,
        out_proj=_tile(s, 512), mem_kv_cols=1024, cross=_tile(s, 512),
        ffn_up=_tile(s, 1024), ffn_cols=512, ffn_down=_tile(s, 256),
    )


def _trunk(x, mem, p, tables):
    b, s, _ = x.shape
    t = b * s
    n1 = DFT_N1
    n2 = s // n1
    ts = _tiles(s)
    dft_c, m1, e, cos2, sin2, cos_t, sin_t = tables

    ab, qt, k, vt = _in_proj(x, p["g_mix"], p["w_in_ext"], p["g_q"], p["w_uqt_ext"], p["g_kv"], p["w_k"],
                             p["w_vt"], dft_c, cos2, sin2, cos_t, sin_t, tm=ts["in_proj"], tk=ts["attn_k"])

    gc = _fourier_a(ab.reshape(b, N_FOURIER_GROUPS, 2, n1, n2, FOURIER_GROUP_DIM), m1, tn2=ts["fourier_tn2"])
    f = _fourier_b(gc.reshape(b, N_FOURIER_GROUPS, 2, n1, n2, FOURIER_GROUP_DIM), e, kb=ts["fourier_kb"],
                   scale=float((s * FOURIER_GROUP_DIM) ** -0.5))

    a = _attention(qt, k, vt, tq=ts["attn_q"], tiles_per_step=ts["attn_tiles"], n_streams=ts["attn_streams"],
                   group_size=ts["attn_group"])

    x1 = _out_proj(x.reshape(t, D_MODEL), f.reshape(t, FOURIER_DIM), a.reshape(t, MLA_DIM),
                   p["g_f"], p["g_a"], p["w_out"], tm=ts["out_proj"])

    kvm = _mem_kv(mem, p["g_mem"], p["w_ckv"], tn=ts["mem_kv_cols"])
    m_all, n_all = _cross_fold(kvm, p["w_cq"], p["w_co"])
    x2, hf = _cross(x1.reshape(b, s, D_MODEL), m_all, n_all, p["g_cross"], p["g_ffn"], tm=ts["cross"])

    act = _ffn_up(hf.reshape(t, D_MODEL), p["w_gate"], p["w_up"], p["conv_w"], p["conv_b"], seq=s,
                  tm=ts["ffn_up"], tf=ts["ffn_cols"])
    y = _ffn_down(act, p["w_down"], x2.reshape(t, D_MODEL), p["g_final"], tm=ts["ffn_down"])
    return y.reshape(b, s, D_MODEL)


@jax.jit
def kernel(x_prompt, x_sample, mem_prompt, mem_sample, norm_mix_g, w_in, q_norm_g, w_uq, kv_norm_g, w_ukv,
           fourier_out_g, mla_out_g, w_out, norm_cross_g, norm_mem_g, w_cq, w_ckv, w_co, norm_ffn_g, w_gate,
           w_up, conv_w, conv_b, w_down, final_norm_g):
    assert norm_mix_g.shape[0] == 1, "single-layer trunk"
    assert x_prompt.shape[1] % DFT_N1 == 0 and x_sample.shape[1] % DFT_N1 == 0
    w_in_ext, w_uqt_ext, w_k, w_vt = _prep_weights(w_in[0], w_uq[0], w_ukv[0])
    row = lambda g: g.reshape(1, -1)
    p = dict(
        g_mix=row(norm_mix_g[0]), w_in_ext=w_in_ext, g_q=row(q_norm_g[0]), w_uqt_ext=w_uqt_ext,
        g_kv=row(kv_norm_g[0]), w_k=w_k, w_vt=w_vt, g_f=row(fourier_out_g[0]), g_a=row(mla_out_g[0]),
        w_out=w_out[0].astype(BF16), g_cross=row(norm_cross_g[0]), g_mem=row(norm_mem_g[0]),
        w_cq=w_cq[0].astype(BF16), w_ckv=w_ckv.reshape(w_ckv.shape[1:]), w_co=w_co[0].astype(BF16),
        g_ffn=row(norm_ffn_g[0]), w_gate=w_gate.reshape(w_gate.shape[1:]), w_up=w_up.reshape(w_up.shape[1:]),
        conv_w=conv_w[0], conv_b=row(conv_b[0]), w_down=w_down[0].astype(BF16), g_final=row(final_norm_g),
    )
    outs = []
    for x, mem in ((x_prompt, mem_prompt), (x_sample, mem_sample)):
        s = x.shape[1]
        tables = _dft_tables(s) + _rope_tables(s)
        outs.append(_trunk(x, mem, p, tables))
    return tuple(outs)
```
